```python
import math
import jax
import jax.numpy as jnp
from jax import lax
import numpy as np

D_MODEL = 1024
BATCH = 8
SEQ = 4096
DEPTH = 4

D_MIX = D_MODEL
ML_HEADS = 4
ML_DH = 64
ML_W = ML_HEADS * ML_DH
ML_CHUNK = 64
SSM_HEADS = 4
SSM_P = 64
SSM_W = SSM_HEADS * SSM_P
SSM_GROUPS = 2
SSM_N = 64
SSM_CONV = 4
SSM_CONV_DIM = SSM_W + 2 * SSM_GROUPS * SSM_N
SSM_CHUNK = 64
AT_HEADS = 4
AT_DH = 64
AT_W = AT_HEADS * AT_DH
DILATED = ((128, 1), (512, 4), (2048, 16))
ROPE_THETA = 10000.0
GLA_HEADS = 4
GLA_DK = 32
GLA_DV = 64
GLA_W = GLA_HEADS * GLA_DV
GLA_GATE_RANK = 16
GLA_TAU = 16.0
GLA_CHUNK = 64
N_EXPERTS = 64
TOP_K = 8
N_GROUPS = 8
TOPK_GROUPS = 4
D_EXPERT = 256
D_SHARED = 256
ROUTED_SCALE = 2.5
MOE_BLOCK = 128
ALPHA = (2 * DEPTH) ** 0.25
BETA = (8 * DEPTH) ** -0.25
LN_EPS = 1e-5

IN_SIZES = (ML_W, ML_W, ML_W, ML_W, ML_HEADS, ML_HEADS,
            SSM_W, SSM_CONV_DIM, SSM_HEADS,
            AT_W, AT_W, AT_W,
            GLA_HEADS * GLA_DK, GLA_HEADS * GLA_DK, GLA_W, GLA_W, GLA_GATE_RANK)
D_IN = sum(IN_SIZES)

kernel_name = 'hybrid_mlstm_ssd_dilated_gla_moe'


def layer_norm(x, g, b):
    xf = x.astype(jnp.float32)
    mu = xf.mean(-1, keepdims=True)
    var = jnp.mean(jnp.square(xf - mu), -1, keepdims=True)
    y = (xf - mu) * lax.rsqrt(var + LN_EPS) * g.astype(jnp.float32) + b.astype(jnp.float32)
    return y.astype(x.dtype)


def head_norm(x, w, center):
    if center:
        x = x - x.mean(-1, keepdims=True)
    x = x * lax.rsqrt(jnp.mean(x * x, -1, keepdims=True) + LN_EPS)
    return x.reshape(*x.shape[:-2], -1) * w


def to_chunks(t, L):
    B, H, S = t.shape[:3]
    return jnp.moveaxis(t.reshape(B, H, S // L, L, *t.shape[3:]), 2, 0)


def from_chunks(t):
    nc, B, H, L = t.shape[:4]
    return jnp.moveaxis(t, 0, 2).reshape(B, H, nc * L, *t.shape[4:])


def mlstm_chunked(q, k, v, li, lf):
    B, H, S, dh = q.shape
    L = ML_CHUNK
    k = k * dh ** -0.5
    causal = jnp.tril(jnp.ones((L, L), bool))

    def step(carry, inp):
        C, n, m = carry
        qc, kc, vc, lic, lfc = inp
        b = jnp.cumsum(lfc, -1)
        log_d = jnp.where(causal, b[..., :, None] - b[..., None, :] + lic[..., None, :], -jnp.inf)
        log_inter = b + m[..., None]
        m_t = jnp.maximum(log_inter, log_d.max(-1))
        w_inter = jnp.exp(log_inter - m_t)
        s = jnp.einsum('bhtd,bhsd->bhts', qc, kc) * jnp.exp(log_d - m_t[..., None])
        num = jnp.einsum('bhts,bhsv->bhtv', s, vc) + w_inter[..., None] * jnp.einsum('bhtd,bhdv->bhtv', qc, C)
        den = s.sum(-1) + w_inter * jnp.einsum('bhtd,bhd->bht', qc, n)
        h = num / jnp.maximum(jnp.abs(den), jnp.exp(-m_t))[..., None]
        b_last = b[..., -1]
        log_s = b_last[..., None] - b + lic
        m_new = jnp.maximum(b_last + m, log_s.max(-1))
        w_s = jnp.exp(log_s - m_new[..., None])
        w_c = jnp.exp(b_last + m - m_new)
        kw = kc * w_s[..., None]
        C = w_c[..., None, None] * C + jnp.einsum('bhsd,bhsv->bhdv', kw, vc)
        n = w_c[..., None] * n + kw.sum(2)
        return (C, n, m_new), h

    init = (jnp.zeros((B, H, dh, dh), q.dtype), jnp.zeros((B, H, dh), q.dtype),
            jnp.full((B, H), -jnp.inf, q.dtype))
    _, h = lax.scan(step, init, tuple(to_chunks(t, L) for t in (q, k, v, li, lf)))
    return from_chunks(h)


def causal_conv(x, w, b):
    K, C = w.shape
    xp = jnp.pad(x, ((0, 0), (K - 1, 0), (0, 0)))
    y = lax.conv_general_dilated(xp, w[:, None, :], (1,), 'VALID',
                                 dimension_numbers=('NWC', 'WIO', 'NWC'), feature_group_count=C)
    return y + b


def ssd(x, dt, A, Bm, Cm, Dskip):
    Bsz, S, H, P = x.shape
    G, N = Bm.shape[2], Bm.shape[3]
    R = H // G
    L = SSM_CHUNK
    nc = S // L
    a = (dt * A).reshape(Bsz, nc, L, G, R)
    xc = (x * dt[..., None]).reshape(Bsz, nc, L, G, R, P)
    Bc = Bm.reshape(Bsz, nc, L, G, N)
    Cc = Cm.reshape(Bsz, nc, L, G, N)
    acs = jnp.cumsum(a, axis=2)
    causal = jnp.tril(jnp.ones((L, L), bool))
    seg = jnp.where(causal[:, :, None, None], acs[:, :, :, None] - acs[:, :, None, :], -jnp.inf)
    cb = jnp.einsum('bclgn,bcsgn->bclsg', Cc, Bc)
    y_diag = jnp.einsum('bclsgr,bcsgrp->bclgrp', cb[..., None] * jnp.exp(seg), xc)
    decay_states = jnp.exp(acs[:, :, -1:] - acs)
    states = jnp.einsum('bclgn,bclgrp->bcgrpn', Bc, decay_states[..., None] * xc)
    chunk_decay = jnp.exp(acs[:, :, -1])

    def step(s, inp):
        st, dec = inp
        return dec[..., None, None] * s + st, s

    s0 = jnp.zeros((Bsz, G, R, P, N), x.dtype)
    _, prev = lax.scan(step, s0, (jnp.moveaxis(states, 1, 0), jnp.moveaxis(chunk_decay, 1, 0)))
    prev = jnp.moveaxis(prev, 0, 1)
    y_off = jnp.einsum('bclgn,bcgrpn->bclgrp', Cc, prev) * jnp.exp(acs)[..., None]
    return (y_diag + y_off).reshape(Bsz, S, H, P) + x * Dskip[:, None]


def rope(x, pos):
    half = x.shape[-1] // 2
    inv = ROPE_THETA ** (-jnp.arange(half, dtype=jnp.float32) / half)
    ang = pos.astype(jnp.float32)[:, None] * inv
    cos = jnp.cos(ang)[None, :, None, :]
    sin = jnp.sin(ang)[None, :, None, :]
    x1, x2 = x[..., :half], x[..., half:]
    return jnp.concatenate([x1 * cos - x2 * sin, x2 * cos + x1 * sin], -1)


def dilated_branch(q, k, v, window, dilation):
    B, S, H, E = q.shape
    span = window // dilation
    nb = -(-S // window)
    pad = nb * window - S

    def blocks(t):
        t = jnp.pad(t, ((0, 0), (0, pad), (0, 0), (0, 0)))
        return t.reshape(B, nb, span, dilation, H, E)

    def with_prev(t):
        prev = jnp.pad(t, ((0, 0), (1, 0), (0, 0), (0, 0), (0, 0), (0, 0)))[:, :-1]
        return jnp.concatenate([prev, t], axis=2)

    qb = blocks(q)
    kk = with_prev(blocks(k))
    vv = with_prev(blocks(v))
    s = jnp.einsum('bnidhe,bnjdhe->bndhij', qb, kk) * E ** -0.5
    i = jnp.arange(span)[:, None]
    j = jnp.arange(2 * span)[None, :]
    rel = span + i - j
    valid = ((rel >= 0) & (rel <= span))[None] & ((jnp.arange(nb)[:, None, None] * span + j - span) >= 0)
    s = jnp.where(valid[None, :, None, None], s, -jnp.inf)
    m = s.max(-1, keepdims=True)
    p = jnp.exp(s - m)
    den = p.sum(-1)
    den_t = jnp.transpose(den, (0, 1, 4, 2, 3))
    o = jnp.einsum('bndhij,bnjdhe->bnidhe', p, vv) / den_t[..., None]
    lse = jnp.transpose(m[..., 0], (0, 1, 4, 2, 3)) + jnp.log(den_t)
    o = o.reshape(B, nb * window, H, E)[:, :S]
    lse = lse.reshape(B, nb * window, H)[:, :S]
    return o, lse


def gla_chunked(q, k, v, lg):
    B, H, S, dk = q.shape
    dv = v.shape[-1]
    L = GLA_CHUNK
    q = q * dk ** -0.5
    causal = jnp.tril(jnp.ones((L, L), bool))

    def step(state, inp):
        qc, kc, vc, gc = inp
        b = jnp.cumsum(gc, axis=2)
        diff = jnp.where(causal[:, :, None], b[:, :, :, None, :] - b[:, :, None, :, :], -jnp.inf)
        att = jnp.einsum('bhtsd,bhsd->bhts', qc[:, :, :, None, :] * jnp.exp(diff), kc)
        o = jnp.einsum('bhts,bhsv->bhtv', att, vc) + jnp.einsum('bhtd,bhdv->bhtv', qc * jnp.exp(b), state)
        b_last = b[:, :, -1:]
        state = jnp.exp(b_last[:, :, 0])[..., None] * state + jnp.einsum('bhsd,bhsv->bhdv', kc * jnp.exp(b_last - b), vc)
        return state, o

    s0 = jnp.zeros((B, H, dk, dv), q.dtype)
    _, o = lax.scan(step, s0, tuple(to_chunks(t, L) for t in (q, k, v, lg)))
    return from_chunks(o)


def token_mixer(x, w_in, ml_i_bias, ml_f_bias, ml_norm_w, ssm_conv_w, ssm_conv_b, ssm_dt_bias,
                ssm_a_log, ssm_d, ssm_norm_w, gla_gate_w2, gla_gate_b, gla_norm_w, w_out):
    B, S, _ = x.shape
    f32 = jnp.float32
    h = jnp.einsum('bsd,de->bse', x, w_in)
    parts = jnp.split(h, np.cumsum(IN_SIZES)[:-1].tolist(), axis=-1)
    (mq, mk, mv, mo, mi, mf, sz, sxbc, sdt, aq, ak, av, gq, gk, gv, gr, ga) = [p.astype(f32) for p in parts]

    def heads_first(t, n):
        return t.reshape(B, S, n, -1).transpose(0, 2, 1, 3)

    li = (mi + ml_i_bias.astype(f32)).transpose(0, 2, 1)
    lf = jax.nn.log_sigmoid(mf + ml_f_bias.astype(f32)).transpose(0, 2, 1)
    h_ml = mlstm_chunked(heads_first(mq, ML_HEADS), heads_first(mk, ML_HEADS), heads_first(mv, ML_HEADS), li, lf)
    y_ml = jax.nn.sigmoid(mo) * head_norm(h_ml.transpose(0, 2, 1, 3), ml_norm_w.astype(f32), True)

    xbc = jax.nn.silu(causal_conv(sxbc, ssm_conv_w.astype(f32), ssm_conv_b.astype(f32)))
    sx, sB, sC = jnp.split(xbc, [SSM_W, SSM_W + SSM_GROUPS * SSM_N], axis=-1)
    dt = jax.nn.softplus(sdt + ssm_dt_bias.astype(f32))
    A = -jnp.exp(ssm_a_log.astype(f32))
    y = ssd(sx.reshape(B, S, SSM_HEADS, SSM_P), dt, A, sB.reshape(B, S, SSM_GROUPS, SSM_N),
            sC.reshape(B, S, SSM_GROUPS, SSM_N), ssm_d.astype(f32))
    y = y.reshape(B, S, SSM_W) * jax.nn.silu(sz)
    y_ssm = y * lax.rsqrt(jnp.mean(y * y, -1, keepdims=True) + LN_EPS) * ssm_norm_w.astype(f32)

    pos = jnp.arange(S)
    q = rope(aq.reshape(B, S, AT_HEADS, AT_DH), pos)
    k = rope(ak.reshape(B, S, AT_HEADS, AT_DH), pos)
    v = av.reshape(B, S, AT_HEADS, AT_DH)
    outs, lses = zip(*[dilated_branch(q, k, v, w, d) for (w, d) in DILATED])
    wts = jax.nn.softmax(jnp.stack(lses), axis=0)
    y_at = jnp.einsum('gbsh,gbshe->bshe', wts, jnp.stack(outs)).reshape(B, S, AT_W)

    lg = jax.nn.log_sigmoid(ga @ gla_gate_w2.astype(f32) + gla_gate_b.astype(f32)) / GLA_TAU
    o = gla_chunked(heads_first(gq, GLA_HEADS), heads_first(gk, GLA_HEADS), heads_first(gv, GLA_HEADS),
                    heads_first(lg, GLA_HEADS))
    y_gla = head_norm(o.transpose(0, 2, 1, 3), gla_norm_w.astype(f32), False) * jax.nn.silu(gr)

    y = jnp.concatenate([y_ml, y_ssm, y_at, y_gla], axis=-1).astype(x.dtype)
    return jnp.einsum('bse,ed->bsd', y, w_out)


def moe(x, router_w, router_bias, w_gate, w_up, w_down, sh_gate, sh_up, sh_down):
    B, S, D = x.shape
    xf = x.reshape(-1, D)
    T = xf.shape[0]
    E = N_EXPERTS
    scores = jax.nn.sigmoid((xf @ router_w).astype(jnp.float32))
    sel = scores + router_bias.astype(jnp.float32)
    grp_score = lax.top_k(sel.reshape(T, N_GROUPS, E // N_GROUPS), 2)[0].sum(-1)
    _, top_g = lax.top_k(grp_score, TOPK_GROUPS)
    gmask = (top_g[..., None] == jnp.arange(N_GROUPS)).any(1)
    sel = jnp.where(jnp.repeat(gmask, E // N_GROUPS, axis=1), sel, -jnp.inf)
    _, top_e = lax.top_k(sel, TOP_K)
    g = jnp.take_along_axis(scores, top_e, axis=1)
    g = g / g.sum(-1, keepdims=True) * ROUTED_SCALE
    M = T * TOP_K
    e_flat = top_e.reshape(-1)
    tok_flat = jnp.repeat(jnp.arange(T, dtype=jnp.int32), TOP_K)
    order = jnp.argsort(e_flat)
    e_sorted = e_flat[order]
    tok_sorted = tok_flat[order]
    g_sorted = g.reshape(-1)[order]
    counts = jnp.zeros((E,), jnp.int32).at[e_flat].add(1)
    start = jnp.cumsum(counts) - counts
    padded = (counts + MOE_BLOCK - 1) // MOE_BLOCK * MOE_BLOCK
    pend = jnp.cumsum(padded)
    pstart = pend - padded
    dest = pstart[e_sorted] + jnp.arange(M, dtype=jnp.int32) - start[e_sorted]
    P = M + E * MOE_BLOCK
    nblk = P // MOE_BLOCK
    row_tok = jnp.full((P,), T, jnp.int32).at[dest].set(tok_sorted)
    row_gate = jnp.zeros((P,), g.dtype).at[dest].set(g_sorted)
    blk_exp = jnp.minimum(jnp.searchsorted(pend, jnp.arange(nblk, dtype=jnp.int32) * MOE_BLOCK, side='right'), E - 1)
    x_pad = jnp.concatenate([xf, jnp.zeros((1, D), xf.dtype)], axis=0)

    def body(acc, inp):
        tok, gate, e = inp
        xb = x_pad[tok]
        hb = jax.nn.silu(xb @ w_gate[e]) * (xb @ w_up[e])
        yb = (hb @ w_down[e]) * gate[:, None].astype(xb.dtype)
        return acc.at[tok].add(yb), None

    acc, _ = lax.scan(body, jnp.zeros((T + 1, D), x.dtype),
                      (row_tok.reshape(nblk, MOE_BLOCK), row_gate.reshape(nblk, MOE_BLOCK), blk_exp))
    shared = (jax.nn.silu(xf @ sh_gate) * (xf @ sh_up)) @ sh_down
    return (acc[:T] + shared).reshape(B, S, D)


def setup_inputs(seed: int = 0) -> dict:
    key = jax.random.key(seed)
    ks = iter(jax.random.split(key, 32))
    f32 = jnp.float32
    L = DEPTH
    D = D_MODEL

    def nrm(shape, scale):
        return jax.random.normal(next(ks), shape, f32) * scale

    x = nrm((BATCH, SEQ, D), 1.0)
    w_in = nrm((L, D, D_IN), D ** -0.5)
    ml_i_bias = nrm((L, ML_HEADS), 0.1)
    ml_f_bias = jnp.linspace(3.0, 6.0, ML_HEADS, dtype=f32)[None] + nrm((L, ML_HEADS), 0.1)
    ml_norm_w = 1.0 + nrm((L, ML_W), 0.02)
    ssm_conv_w = nrm((L, SSM_CONV, SSM_CONV_DIM), SSM_CONV ** -0.5)
    ssm_conv_b = nrm((L, SSM_CONV_DIM), 0.02)
    dt0 = jnp.exp(jax.random.uniform(next(ks), (L, SSM_HEADS), f32, math.log(1e-3), math.log(1e-1)))
    ssm_dt_bias = dt0 + jnp.log(-jnp.expm1(-dt0))
    ssm_a_log = jnp.log(jax.random.uniform(next(ks), (L, SSM_HEADS), f32, 1.0, 16.0))
    ssm_d = 1.0 + nrm((L, SSM_HEADS), 0.1)
    ssm_norm_w = 1.0 + nrm((L, SSM_W), 0.02)
    gla_gate_w2 = nrm((L, GLA_GATE_RANK, GLA_HEADS * GLA_DK), GLA_GATE_RANK ** -0.5)
    gla_gate_b = nrm((L, GLA_HEADS * GLA_DK), 0.1)
    gla_norm_w = 1.0 + nrm((L, GLA_W), 0.02)
    w_out = nrm((L, D_MIX, D), D_MIX ** -0.5 * BETA)
    ln1_g = 1.0 + nrm((L, D), 0.02)
    ln1_b = nrm((L, D), 0.02)
    router_w = nrm((L, D, N_EXPERTS), D ** -0.5)
    router_bias = nrm((L, N_EXPERTS), 0.01)
    exp_w_gate = nrm((L, N_EXPERTS, D, D_EXPERT), D ** -0.5)
    exp_w_up = nrm((L, N_EXPERTS, D, D_EXPERT), D ** -0.5)
    exp_w_down = nrm((L, N_EXPERTS, D_EXPERT, D), D_EXPERT ** -0.5 * BETA)
    sh_w_gate = nrm((L, D, D_SHARED), D ** -0.5)
    sh_w_up = nrm((L, D, D_SHARED), D ** -0.5)
    sh_w_down = nrm((L, D_SHARED, D), D_SHARED ** -0.5 * BETA)
    ln2_g = 1.0 + nrm((L, D), 0.02)
    ln2_b = nrm((L, D), 0.02)
    return {'x': x, 'w_in': w_in, 'ml_i_bias': ml_i_bias, 'ml_f_bias': ml_f_bias, 'ml_norm_w': ml_norm_w,
            'ssm_conv_w': ssm_conv_w, 'ssm_conv_b': ssm_conv_b, 'ssm_dt_bias': ssm_dt_bias,
            'ssm_a_log': ssm_a_log, 'ssm_d': ssm_d, 'ssm_norm_w': ssm_norm_w,
            'gla_gate_w2': gla_gate_w2, 'gla_gate_b': gla_gate_b, 'gla_norm_w': gla_norm_w,
            'w_out': w_out, 'ln1_g': ln1_g, 'ln1_b': ln1_b, 'router_w': router_w, 'router_bias': router_bias,
            'exp_w_gate': exp_w_gate, 'exp_w_up': exp_w_up, 'exp_w_down': exp_w_down,
            'sh_w_gate': sh_w_gate, 'sh_w_up': sh_w_up, 'sh_w_down': sh_w_down,
            'ln2_g': ln2_g, 'ln2_b': ln2_b}


def reference(x, w_in, ml_i_bias, ml_f_bias, ml_norm_w, ssm_conv_w, ssm_conv_b, ssm_dt_bias, ssm_a_log,
              ssm_d, ssm_norm_w, gla_gate_w2, gla_gate_b, gla_norm_w, w_out, ln1_g, ln1_b, router_w,
              router_bias, exp_w_gate, exp_w_up, exp_w_down, sh_w_gate, sh_w_up, sh_w_down, ln2_g, ln2_b):
    for l in range(DEPTH):
        y = token_mixer(x, w_in[l], ml_i_bias[l], ml_f_bias[l], ml_norm_w[l], ssm_conv_w[l], ssm_conv_b[l],
                        ssm_dt_bias[l], ssm_a_log[l], ssm_d[l], ssm_norm_w[l], gla_gate_w2[l], gla_gate_b[l],
                        gla_norm_w[l], w_out[l])
        x = layer_norm(ALPHA * x + y, ln1_g[l], ln1_b[l])
        y = moe(x, router_w[l], router_bias[l], exp_w_gate[l], exp_w_up[l], exp_w_down[l],
                sh_w_gate[l], sh_w_up[l], sh_w_down[l])
        x = layer_norm(ALPHA * x + y, ln2_g[l], ln2_b[l])
    return x
```

```python
import functools

import jax
import jax.numpy as jnp
from jax import lax
from jax.experimental import pallas as pl
from jax.experimental.pallas import tpu as pltpu

F32 = jnp.float32
BF16 = jnp.bfloat16
HI = lax.Precision.HIGHEST
NT = (((1,), (1,)), ((), ()))
TN = (((0,), (0,)), ((), ()))

DEPTH = 4
HEADS = 4
ML_DH = 64
SSM_P = 64
SSM_N = 64
SSM_GROUPS = 2
SSM_CONV = 4
AT_DH = 64
DILATIONS = (1, 4, 16)
AT_SPAN = 128
ROPE_THETA = 10000.0
GLA_DK = 32
GLA_DV = 64
GLA_RANK = 16
GLA_TAU = 16.0
N_EXPERTS = 64
TOP_K = 8
N_GROUPS = 8
TOPK_GROUPS = 4
ROUTED_SCALE = 2.5
ALPHA = (2 * DEPTH) ** 0.25
LN_EPS = 1e-5
CHUNK = 64

G_MI, G_MF, G_DT, G_GA = 0, 4, 8, 12

VMEM_LIMIT = 48 * 1024 * 1024


def _log_sigmoid(x):
    return jnp.minimum(x, 0.0) - jnp.log(1.0 + jnp.exp(-jnp.abs(x)))


def _softplus(x):
    return jnp.maximum(x, 0.0) + jnp.log(1.0 + jnp.exp(-jnp.abs(x)))


def _silu(x):
    return x * jax.nn.sigmoid(x)


def _layer_norm(z, g, b):
    mu = jnp.mean(z, axis=-1, keepdims=True)
    zc = z - mu
    var = jnp.mean(zc * zc, axis=-1, keepdims=True)
    return zc * lax.rsqrt(var + LN_EPS) * g + b


def _params(n_axes):
    return pltpu.CompilerParams(dimension_semantics=("arbitrary",) * n_axes, vmem_limit_bytes=VMEM_LIMIT)


def _in_proj_kernel(x_ref, wm_ref, wg_ref, cos_ref, sin_ref, ml_ref, ssm_ref, at_ref, gla_ref, g_ref):
    xb = x_ref[...].astype(BF16)

    def mm(lo):
        return jnp.dot(xb, wm_ref[:, lo:lo + 256], preferred_element_type=F32)

    for c in range(4):
        a = mm(256 * c)
        if c == 1:
            a = a * ML_DH ** -0.5
        ml_ref[:, 256 * c:256 * (c + 1)] = a.astype(BF16)
    for c in range(3):
        ssm_ref[:, 256 * c:256 * (c + 1)] = mm(1024 + 256 * c).astype(BF16)

    cos = cos_ref[...]
    sin = sin_ref[...]
    lane = lax.broadcasted_iota(jnp.int32, cos.shape, 1)
    first_half = (lane % AT_DH) < AT_DH // 2

    def rope(a):
        rot = jnp.where(first_half, pltpu.roll(a, 128 - AT_DH // 2, 1), pltpu.roll(a, AT_DH // 2, 1))
        return a * cos + rot * sin

    for c in range(3):
        a = mm(1792 + 256 * c)
        if c < 2:
            scale = AT_DH ** -0.5 if c == 0 else 1.0
            for hh in range(2):
                at_ref[:, 256 * c + 128 * hh:256 * c + 128 * (hh + 1)] = (
                    rope(a[:, 128 * hh:128 * (hh + 1)]) * scale).astype(BF16)
        else:
            at_ref[:, 512:768] = a.astype(BF16)
    for c in range(3):
        gla_ref[:, 256 * c:256 * (c + 1)] = mm(2560 + 256 * c).astype(BF16)
    g_ref[...] = jnp.dot(xb, wg_ref[...], preferred_element_type=F32)


def _in_proj(x, wm, wg, cos, sin, S, tm=512):
    T, D = x.shape
    nS = S // tm
    row = lambda i: (i, 0)
    full = lambda i: (0, 0)
    return pl.pallas_call(
        _in_proj_kernel,
        grid=(T // tm,),
        in_specs=[pl.BlockSpec((tm, D), row), pl.BlockSpec(wm.shape, full), pl.BlockSpec(wg.shape, full),
                  pl.BlockSpec((tm, 128), lambda i: (i % nS, 0)), pl.BlockSpec((tm, 128), lambda i: (i % nS, 0))],
        out_specs=[pl.BlockSpec((tm, 1024), row), pl.BlockSpec((tm, 768), row), pl.BlockSpec((tm, 768), row),
                   pl.BlockSpec((tm, 768), row), pl.BlockSpec((tm, 128), row)],
        out_shape=[jax.ShapeDtypeStruct((T, 1024), BF16), jax.ShapeDtypeStruct((T, 768), BF16),
                   jax.ShapeDtypeStruct((T, 768), BF16), jax.ShapeDtypeStruct((T, 768), BF16),
                   jax.ShapeDtypeStruct((T, 128), F32)],
        compiler_params=_params(1),
        name="in_proj",
    )(x, wm, wg, cos, sin)


def _tri_consts(L):
    ri = lax.broadcasted_iota(jnp.int32, (L, L), 0)
    ci = lax.broadcasted_iota(jnp.int32, (L, L), 1)
    causal = ri >= ci
    return causal, causal.astype(F32), (ri <= ci).astype(F32)


def _mlstm_kernel(ml_ref, g_ref, brow_ref, nw_ref, y_ref, c_ref, m_ref, *, CH):
    @pl.when(pl.program_id(1) == 0)
    def _init():
        c_ref[...] = jnp.zeros_like(c_ref)
        m_ref[...] = jnp.full_like(m_ref, -jnp.inf)

    L = CHUNK
    causal, tril, triu = _tri_consts(L)
    lane = lax.broadcasted_iota(jnp.int32, (128, 128), 1)
    is_f = (lane >= G_MF) & (lane < G_MF + HEADS)
    ones_v = jnp.ones((L, ML_DH), BF16)
    nw = nw_ref[...]

    def pair(pi, carry):
        r0 = pl.multiple_of(pi * 128, 128)
        g2 = g_ref[pl.ds(r0, 128), :] + brow_ref[...]
        vals = jnp.where(is_f, _log_sigmoid(g2), g2)
        vals_t = vals.T
        for half in range(2):
            rr = pl.multiple_of(r0 + half * L, L)
            vc = vals[half * L:(half + 1) * L, :]
            vtc = vals_t[0:8, half * L:(half + 1) * L]
            b_all = jnp.dot(tril, vc, precision=HI, preferred_element_type=F32)
            bt_all = jnp.dot(vtc, triu, precision=HI, preferred_element_type=F32)
            for h in range(HEADS):
                q = ml_ref[pl.ds(rr, L), h * 64:(h + 1) * 64]
                k = ml_ref[pl.ds(rr, L), 256 + h * 64:256 + (h + 1) * 64]
                v = ml_ref[pl.ds(rr, L), 512 + h * 64:512 + (h + 1) * 64]
                og = ml_ref[pl.ds(rr, L), 768 + h * 64:768 + (h + 1) * 64].astype(F32)
                b_col = b_all[:, G_MF + h:G_MF + h + 1]
                li_col = vc[:, G_MI + h:G_MI + h + 1]
                b_row = bt_all[G_MF + h:G_MF + h + 1, :]
                li_row = vtc[G_MI + h:G_MI + h + 1, :]
                m_prev = m_ref[h][0:1, 0:1]
                cst = c_ref[h]

                log_d = jnp.where(causal, b_col - b_row + li_row, -jnp.inf)
                log_inter = b_col + m_prev
                m_t = jnp.maximum(log_inter, jnp.max(log_d, axis=1, keepdims=True))
                w_inter = jnp.exp(log_inter - m_t)
                s = lax.dot_general(q, k, NT, preferred_element_type=F32) * jnp.exp(log_d - m_t)
                qc = jnp.dot(q, cst.astype(BF16), preferred_element_type=F32)
                num = jnp.dot(s.astype(BF16), v, preferred_element_type=F32) + w_inter * qc[:, 0:64]
                den = jnp.sum(s, axis=1, keepdims=True) + w_inter * qc[:, 64:65]
                hh = num / jnp.maximum(jnp.abs(den), jnp.exp(-m_t))

                b_last = b_col[L - 1:L, :]
                log_s = b_last - b_col + li_col
                m_new = jnp.maximum(b_last + m_prev, jnp.max(log_s, axis=0, keepdims=True))
                w_c = jnp.exp(b_last + m_prev - m_new)
                kw = (k.astype(F32) * jnp.exp(log_s - m_new)).astype(BF16)
                c_ref[h, :, 0:64] = w_c * cst[:, 0:64] + lax.dot_general(kw, v, TN, preferred_element_type=F32)
                c_ref[h, :, 64:128] = w_c * cst[:, 64:128] + lax.dot_general(kw, ones_v, TN,
                                                                              preferred_element_type=F32)
                m_ref[h] = jnp.broadcast_to(m_new, (8, 128))

                hc = hh - jnp.mean(hh, axis=1, keepdims=True)
                hn = hc * lax.rsqrt(jnp.mean(hc * hc, axis=1, keepdims=True) + LN_EPS) * nw[:, h * 64:(h + 1) * 64]
                y_ref[pl.ds(rr, L), h * 64:(h + 1) * 64] = (jax.nn.sigmoid(og) * hn).astype(BF16)
        return carry

    lax.fori_loop(0, CH // 128, pair, 0)


def _mlstm(ml, gates, brow, nw, B, S, CH=512):
    T = ml.shape[0]
    nS = S // CH
    row = lambda b, j: (b * nS + j, 0)
    full = lambda b, j: (0, 0)
    return pl.pallas_call(
        functools.partial(_mlstm_kernel, CH=CH),
        grid=(B, nS),
        in_specs=[pl.BlockSpec((CH, 1024), row), pl.BlockSpec((CH, 128), row),
                  pl.BlockSpec((1, 128), full), pl.BlockSpec((1, 256), full)],
        out_specs=pl.BlockSpec((CH, 256), row),
        out_shape=jax.ShapeDtypeStruct((T, 256), BF16),
        scratch_shapes=[pltpu.VMEM((HEADS, ML_DH, 128), F32), pltpu.VMEM((HEADS, 8, 128), F32)],
        compiler_params=_params(2),
        name="mlstm",
    )(ml, gates, brow, nw)


def _ssd_kernel(ssm_ref, g_ref, brow_ref, alog_ref, cw_ref, cb_ref, d_ref, nw_ref, y_ref,
                xbuf_ref, xact_ref, st_ref, *, CH):
    @pl.when(pl.program_id(1) == 0)
    def _init():
        xbuf_ref[0:8, :] = jnp.zeros((8, 512), F32)
        st_ref[...] = jnp.zeros_like(st_ref)

    xbuf_ref[8:CH + 8, :] = ssm_ref[:, 256:768].astype(F32)
    conv = cb_ref[...] + cw_ref[0:1, :] * xbuf_ref[5:5 + CH, :]
    for j in range(1, SSM_CONV):
        conv = conv + cw_ref[j:j + 1, :] * xbuf_ref[5 + j:5 + j + CH, :]
    xact_ref[...] = _silu(conv)
    xbuf_ref[0:8, :] = xbuf_ref[CH:CH + 8, :]

    L = CHUNK
    causal, tril, triu = _tri_consts(L)
    lane = lax.broadcasted_iota(jnp.int32, (1, 128), 1)
    a_row = jnp.where((lane >= G_DT) & (lane < G_DT + HEADS), -jnp.exp(alog_ref[...]), 0.0)
    dskip = d_ref[...]
    nw = nw_ref[...]

    def pair(pi, carry):
        r0 = pl.multiple_of(pi * 128, 128)
        dt2 = _softplus(g_ref[pl.ds(r0, 128), :] + brow_ref[...])
        a2 = dt2 * a_row
        a2_t = a2.T
        for half in range(2):
            rr = pl.multiple_of(r0 + half * L, L)
            acs_all = jnp.dot(tril, a2[half * L:(half + 1) * L, :], precision=HI, preferred_element_type=F32)
            acs_t = jnp.dot(a2_t[G_DT:G_DT + 8, half * L:(half + 1) * L], triu, precision=HI,
                            preferred_element_type=F32)
            cb = []
            bmat = []
            cmat = []
            for g in range(SSM_GROUPS):
                bm = xact_ref[pl.ds(rr, L), 256 + g * 64:256 + (g + 1) * 64]
                cm = xact_ref[pl.ds(rr, L), 384 + g * 64:384 + (g + 1) * 64].astype(BF16)
                bmat.append(bm)
                cmat.append(cm)
                cb.append(lax.dot_general(cm, bm.astype(BF16), NT, preferred_element_type=F32))
            gated = []
            ssq = jnp.zeros((L, 1), F32)
            for h in range(HEADS):
                g = h // (HEADS // SSM_GROUPS)
                acs_col = acs_all[:, G_DT + h:G_DT + h + 1]
                acs_row = acs_t[h:h + 1, :]
                dt_col = dt2[half * L:(half + 1) * L, G_DT + h:G_DT + h + 1]
                xh = xact_ref[pl.ds(rr, L), h * 64:(h + 1) * 64]
                xdt = (xh * dt_col).astype(BF16)
                st = st_ref[h]
                mmat = cb[g] * jnp.exp(jnp.where(causal, acs_col - acs_row, -jnp.inf))
                y = jnp.dot(mmat.astype(BF16), xdt, preferred_element_type=F32)
                y = y + jnp.dot(cmat[g], st.astype(BF16), preferred_element_type=F32) * jnp.exp(acs_col)
                y = y + xh * dskip[:, h * 64:(h + 1) * 64]
                acs_last = acs_col[L - 1:L, :]
                bdec = (bmat[g] * jnp.exp(acs_last - acs_col)).astype(BF16)
                st_ref[h] = jnp.exp(acs_last) * st + lax.dot_general(bdec, xdt, TN, preferred_element_type=F32)
                z = ssm_ref[pl.ds(rr, L), h * 64:(h + 1) * 64].astype(F32)
                yg = y * _silu(z)
                ssq = ssq + jnp.sum(yg * yg, axis=1, keepdims=True)
                gated.append(yg)
            scale = lax.rsqrt(ssq / (HEADS * SSM_P) + LN_EPS)
            for h in range(HEADS):
                y_ref[pl.ds(rr, L), h * 64:(h + 1) * 64] = (gated[h] * scale * nw[:, h * 64:(h + 1) * 64]).astype(BF16)
        return carry

    lax.fori_loop(0, CH // 128, pair, 0)


def _ssd(ssm, gates, brow, alog, cw, cb, dskip, nw, B, S, CH=512):
    T = ssm.shape[0]
    nS = S // CH
    row = lambda b, j: (b * nS + j, 0)
    full = lambda b, j: (0, 0)
    return pl.pallas_call(
        functools.partial(_ssd_kernel, CH=CH),
        grid=(B, nS),
        in_specs=[pl.BlockSpec((CH, 768), row), pl.BlockSpec((CH, 128), row), pl.BlockSpec((1, 128), full),
                  pl.BlockSpec((1, 128), full), pl.BlockSpec((SSM_CONV, 512), full), pl.BlockSpec((1, 512), full),
                  pl.BlockSpec((1, 256), full), pl.BlockSpec((1, 256), full)],
        out_specs=pl.BlockSpec((CH, 256), row),
        out_shape=jax.ShapeDtypeStruct((T, 256), BF16),
        scratch_shapes=[pltpu.VMEM((CH + 8, 512), F32), pltpu.VMEM((CH, 512), F32),
                        pltpu.VMEM((HEADS, SSM_N, SSM_P), F32)],
        compiler_params=_params(2),
        name="ssd",
    )(ssm, gates, brow, alog, cw, cb, dskip, nw)


def _attn_kernel(at_ref, o_ref, lse_ref, *, N):
    W = AT_SPAN
    ri = lax.broadcasted_iota(jnp.int32, (W, W), 0)
    ci = lax.broadcasted_iota(jnp.int32, (W, W), 1)
    cur_ok = ri >= ci
    prev_ok = ci >= ri

    def blk(n, carry):
        r0 = pl.multiple_of(n * W, W)
        rp = pl.multiple_of(jnp.maximum(n - 1, 0) * W, W)
        pmask = prev_ok & (n > 0)
        for h in range(HEADS):
            q = at_ref[pl.ds(r0, W), h * 64:(h + 1) * 64]
            kc = at_ref[pl.ds(r0, W), 256 + h * 64:256 + (h + 1) * 64]
            kp = at_ref[pl.ds(rp, W), 256 + h * 64:256 + (h + 1) * 64]
            vc = at_ref[pl.ds(r0, W), 512 + h * 64:512 + (h + 1) * 64]
            vp = at_ref[pl.ds(rp, W), 512 + h * 64:512 + (h + 1) * 64]
            sc = jnp.where(cur_ok, lax.dot_general(q, kc, NT, preferred_element_type=F32), -jnp.inf)
            sp = jnp.where(pmask, lax.dot_general(q, kp, NT, preferred_element_type=F32), -jnp.inf)
            m = jnp.maximum(jnp.max(sc, axis=1, keepdims=True), jnp.max(sp, axis=1, keepdims=True))
            pc = jnp.exp(sc - m)
            pp = jnp.exp(sp - m)
            den = jnp.sum(pc, axis=1, keepdims=True) + jnp.sum(pp, axis=1, keepdims=True)
            acc = jnp.dot(pc.astype(BF16), vc, preferred_element_type=F32)
            acc = acc + jnp.dot(pp.astype(BF16), vp, preferred_element_type=F32)
            o_ref[pl.ds(r0, W), h * 64:(h + 1) * 64] = (acc / den).astype(BF16)
            lse_ref[pl.ds(r0, W), h * 64:(h + 1) * 64] = jnp.broadcast_to(m + jnp.log(den), (W, 64))
        return carry

    lax.fori_loop(0, N // W, blk, 0)


def _attn(at, d, B, S):
    N = S // d
    atv = at.reshape(B, N, d * 768)
    o, lse = pl.pallas_call(
        functools.partial(_attn_kernel, N=N),
        grid=(B, d),
        in_specs=[pl.BlockSpec((None, N, 768), lambda b, r: (b, 0, r))],
        out_specs=[pl.BlockSpec((None, N, 256), lambda b, r: (b, 0, r)),
                   pl.BlockSpec((None, N, 256), lambda b, r: (b, 0, r))],
        out_shape=[jax.ShapeDtypeStruct((B, N, d * 256), BF16), jax.ShapeDtypeStruct((B, N, d * 256), F32)],
        compiler_params=_params(2),
        name=f"attn_d{d}",
    )(atv)
    return o.reshape(B * S, 256), lse.reshape(B * S, 256)


def _gla_kernel(gla_ref, g_ref, w2_ref, b2_ref, nw_ref, y_ref, st_ref, *, CH):
    @pl.when(pl.program_id(1) == 0)
    def _init():
        st_ref[...] = jnp.zeros_like(st_ref)

    L = CHUNK
    SB = 16
    ri = lax.broadcasted_iota(jnp.int32, (L, L), 0)
    ci = lax.broadcasted_iota(jnp.int32, (L, L), 1)
    tril_blk = ((ri >= ci) & (ri // SB == ci // SB)).astype(F32)
    tl = lax.broadcasted_iota(jnp.int32, (L, 128), 0) % SB
    er = lax.broadcasted_iota(jnp.int32, (128, 256), 0) // GLA_DK
    ec = lax.broadcasted_iota(jnp.int32, (128, 256), 1) // GLA_DV
    head_expand = (er == ec).astype(BF16)
    sr = lax.broadcasted_iota(jnp.int32, (256, 128), 0) // GLA_DV
    sc = lax.broadcasted_iota(jnp.int32, (256, 128), 1) // GLA_DK
    st_mask = (sr == sc).astype(F32)
    nw = nw_ref[...]

    def chunk(ci_, carry):
        rr = pl.multiple_of(ci_ * L, L)
        lg = _log_sigmoid(jnp.dot(g_ref[pl.ds(rr, L), :], w2_ref[...], preferred_element_type=F32)
                          + b2_ref[...]) / GLA_TAU
        c = jnp.dot(tril_blk, lg, precision=HI, preferred_element_type=F32)
        q = gla_ref[pl.ds(rr, L), 0:128].astype(F32) * GLA_DK ** -0.5
        k = gla_ref[pl.ds(rr, L), 128:256].astype(F32)
        vb = gla_ref[pl.ds(rr, L), 256:512]
        v = vb.astype(F32)
        c4 = c.reshape(L // SB, SB, 128)
        k4 = k.reshape(L // SB, SB, 128)
        v4 = v.reshape(L // SB, SB, 256)

        def bcast(x4, j, width):
            return jnp.broadcast_to(x4[:, j:j + 1, :], (L // SB, SB, width)).reshape(L, width)

        o = jnp.zeros((L, 256), F32)
        for j in range(SB):
            dec = jnp.exp(jnp.minimum(c - bcast(c4, j, 128), 0.0))
            p = jnp.where(tl >= j, q * dec * bcast(k4, j, 128), 0.0)
            a = jnp.dot(p.astype(BF16), head_expand, preferred_element_type=F32)
            o = o + a * bcast(v4, j, 256)

        outs = []
        for i in range(L // SB):
            cblk = c[i * SB:(i + 1) * SB, :]
            clast = cblk[SB - 1:SB, :]
            st = st_ref[...]
            qe = (q[i * SB:(i + 1) * SB, :] * jnp.exp(cblk)).astype(BF16)
            outs.append(o[i * SB:(i + 1) * SB, :] + lax.dot_general(qe, st.astype(BF16), NT,
                                                                    preferred_element_type=F32))
            ke = (k[i * SB:(i + 1) * SB, :] * jnp.exp(clast - cblk)).astype(BF16)
            upd = lax.dot_general(vb[i * SB:(i + 1) * SB, :], ke, TN, preferred_element_type=F32)
            st_ref[...] = st * jnp.exp(clast) + st_mask * upd
        for i in range(L // SB):
            rg = gla_ref[pl.ds(pl.multiple_of(rr + i * SB, SB), SB), 512:768].astype(F32)
            for h in range(HEADS):
                oh = outs[i][:, h * 64:(h + 1) * 64]
                rms = lax.rsqrt(jnp.mean(oh * oh, axis=1, keepdims=True) + LN_EPS)
                y_ref[pl.ds(pl.multiple_of(rr + i * SB, SB), SB), h * 64:(h + 1) * 64] = (
                    oh * rms * nw[:, h * 64:(h + 1) * 64] * _silu(rg[:, h * 64:(h + 1) * 64])).astype(BF16)
        return carry

    lax.fori_loop(0, CH // L, chunk, 0)


def _gla(gla, gates, w2, b2, nw, B, S, CH=512):
    T = gla.shape[0]
    nS = S // CH
    row = lambda b, j: (b * nS + j, 0)
    full = lambda b, j: (0, 0)
    return pl.pallas_call(
        functools.partial(_gla_kernel, CH=CH),
        grid=(B, nS),
        in_specs=[pl.BlockSpec((CH, 768), row), pl.BlockSpec((CH, 128), row), pl.BlockSpec((128, 128), full),
                  pl.BlockSpec((1, 128), full), pl.BlockSpec((1, 256), full)],
        out_specs=pl.BlockSpec((CH, 256), row),
        out_shape=jax.ShapeDtypeStruct((T, 256), BF16),
        scratch_shapes=[pltpu.VMEM((HEADS * GLA_DV, HEADS * GLA_DK), F32)],
        compiler_params=_params(2),
        name="gla",
    )(gla, gates, w2, b2, nw)


def _out_proj_kernel(yml_ref, yssm_ref, o1_ref, o2_ref, o3_ref, l1_ref, l2_ref, l3_ref, ygla_ref, x_ref,
                     wo_ref, g_ref, b_ref, x1_ref):
    l1, l2, l3 = l1_ref[...], l2_ref[...], l3_ref[...]
    mx = jnp.maximum(jnp.maximum(l1, l2), l3)
    e1, e2, e3 = jnp.exp(l1 - mx), jnp.exp(l2 - mx), jnp.exp(l3 - mx)
    yat = (e1 * o1_ref[...].astype(F32) + e2 * o2_ref[...].astype(F32) + e3 * o3_ref[...].astype(F32)) / (e1 + e2 + e3)
    acc = jnp.dot(yml_ref[...], wo_ref[0:256, :], preferred_element_type=F32)
    acc = acc + jnp.dot(yssm_ref[...], wo_ref[256:512, :], preferred_element_type=F32)
    acc = acc + jnp.dot(yat.astype(BF16), wo_ref[512:768, :], preferred_element_type=F32)
    acc = acc + jnp.dot(ygla_ref[...], wo_ref[768:1024, :], preferred_element_type=F32)
    x1_ref[...] = _layer_norm(ALPHA * x_ref[...] + acc, g_ref[...], b_ref[...])


def _out_proj(yml, yssm, o1, o2, o3, l1, l2, l3, ygla, x, wo, g, b, tm=512):
    T, D = x.shape
    row = lambda i: (i, 0)
    full = lambda i: (0, 0)
    small = pl.BlockSpec((tm, 256), row)
    return pl.pallas_call(
        _out_proj_kernel,
        grid=(T // tm,),
        in_specs=[small] * 9 + [pl.BlockSpec((tm, D), row), pl.BlockSpec(wo.shape, full),
                                pl.BlockSpec((1, D), full), pl.BlockSpec((1, D), full)],
        out_specs=pl.BlockSpec((tm, D), row),
        out_shape=jax.ShapeDtypeStruct((T, D), F32),
        compiler_params=_params(1),
        name="out_proj_ln",
    )(yml, yssm, o1, o2, o3, l1, l2, l3, ygla, x, wo, g, b)


def _router_kernel(x_ref, wr_ref, rb_ref, gd_ref):
    tm = x_ref.shape[0]
    gsz = N_EXPERTS // N_GROUPS
    logits = lax.dot_general(wr_ref[...], x_ref[...], NT, precision=HI, preferred_element_type=F32)
    scores = jax.nn.sigmoid(logits)
    sel = scores + rb_ref[...]
    eidx = lax.broadcasted_iota(jnp.int32, (gsz, tm), 0)
    big = jnp.int32(1 << 20)
    neg = -jnp.inf

    sel_g = [sel[g * gsz:(g + 1) * gsz, :] for g in range(N_GROUPS)]
    idx_g = [eidx + g * gsz for g in range(N_GROUPS)]
    gscore = []
    for g in range(N_GROUPS):
        v = sel_g[g]
        m1 = jnp.max(v, axis=0, keepdims=True)
        i1 = jnp.min(jnp.where(v == m1, idx_g[g], big), axis=0, keepdims=True)
        m2 = jnp.max(jnp.where(idx_g[g] == i1, neg, v), axis=0, keepdims=True)
        gscore.append(m1 + m2)
    gkeep = [jnp.zeros((1, tm), jnp.bool_) for _ in range(N_GROUPS)]
    for _ in range(TOPK_GROUPS):
        m = functools.reduce(jnp.maximum, gscore)
        gi = functools.reduce(jnp.minimum, [jnp.where(gscore[g] == m, g, big) for g in range(N_GROUPS)])
        for g in range(N_GROUPS):
            hit = gi == g
            gkeep[g] = gkeep[g] | hit
            gscore[g] = jnp.where(hit, neg, gscore[g])
    cand = [jnp.where(gkeep[g], sel_g[g], neg) for g in range(N_GROUPS)]
    chosen = [jnp.zeros((gsz, tm), jnp.bool_) for _ in range(N_GROUPS)]
    for _ in range(TOP_K):
        m = functools.reduce(jnp.maximum, [jnp.max(c, axis=0, keepdims=True) for c in cand])
        ei = functools.reduce(jnp.minimum, [jnp.min(jnp.where(cand[g] == m, idx_g[g], big), axis=0, keepdims=True)
                                            for g in range(N_GROUPS)])
        for g in range(N_GROUPS):
            hit = idx_g[g] == ei
            chosen[g] = chosen[g] | hit
            cand[g] = jnp.where(hit, neg, cand[g])
    picked = [jnp.where(chosen[g], scores[g * gsz:(g + 1) * gsz, :], 0.0) for g in range(N_GROUPS)]
    tot = functools.reduce(jnp.add, [jnp.sum(p, axis=0, keepdims=True) for p in picked])
    gates = [p / tot * ROUTED_SCALE for p in picked]
    row0 = lax.broadcasted_iota(jnp.int32, (N_EXPERTS, tm), 0) == 0
    shared = jnp.where(row0, 1.0, 0.0).astype(F32)
    gd_ref[...] = jnp.concatenate(gates + [shared], axis=0).T


def _router(x1, wr_t, rb, tm=512):
    T, D = x1.shape
    return pl.pallas_call(
        _router_kernel,
        grid=(T // tm,),
        in_specs=[pl.BlockSpec((tm, D), lambda i: (i, 0)), pl.BlockSpec(wr_t.shape, lambda i: (0, 0)),
                  pl.BlockSpec(rb.shape, lambda i: (0, 0))],
        out_specs=pl.BlockSpec((tm, 2 * N_EXPERTS), lambda i: (i, 0)),
        out_shape=jax.ShapeDtypeStruct((T, 2 * N_EXPERTS), F32),
        compiler_params=_params(1),
        name="router",
    )(x1, wr_t, rb)


def _moe_kernel(x1_ref, gd_ref, wg_ref, wu_ref, wd_ref, g_ref, b_ref, x2_ref, acc_ref):
    e = pl.program_id(1)

    @pl.when(e == 0)
    def _init():
        acc_ref[...] = jnp.zeros_like(acc_ref)

    xb = x1_ref[...].astype(BF16)
    a = jnp.dot(xb, wg_ref[...], preferred_element_type=F32)
    u = jnp.dot(xb, wu_ref[...], preferred_element_type=F32)
    pick = (lax.broadcasted_iota(jnp.int32, (2 * N_EXPERTS, 128), 0) == e).astype(BF16)
    gd = gd_ref[...]
    g_hi = gd.astype(BF16)
    g_lo = (gd - g_hi.astype(F32)).astype(BF16)
    gc = jnp.dot(g_hi, pick, preferred_element_type=F32) + jnp.dot(g_lo, pick, preferred_element_type=F32)
    hid = (_silu(a) * u * jnp.concatenate([gc] * (a.shape[1] // 128), axis=1)).astype(BF16)
    acc_ref[...] += jnp.dot(hid, wd_ref[...], preferred_element_type=F32)

    @pl.when(e == pl.num_programs(1) - 1)
    def _fin():
        x2_ref[...] = _layer_norm(ALPHA * x1_ref[...] + acc_ref[...], g_ref[...], b_ref[...])


def _moe(x1, gd, wg, wu, wd, g, b, tm=1024):
    T, D = x1.shape
    E1, _, F = wg.shape
    row = lambda i, e: (i, 0)
    full = lambda i, e: (0, 0)
    return pl.pallas_call(
        _moe_kernel,
        grid=(T // tm, E1),
        in_specs=[pl.BlockSpec((tm, D), row), pl.BlockSpec((tm, 2 * N_EXPERTS), row),
                  pl.BlockSpec((None, D, F), lambda i, e: (e, 0, 0)), pl.BlockSpec((None, D, F), lambda i, e: (e, 0, 0)),
                  pl.BlockSpec((None, F, D), lambda i, e: (e, 0, 0)),
                  pl.BlockSpec((1, D), full), pl.BlockSpec((1, D), full)],
        out_specs=pl.BlockSpec((tm, D), row),
        out_shape=jax.ShapeDtypeStruct((T, D), F32),
        scratch_shapes=[pltpu.VMEM((tm, D), F32)],
        compiler_params=_params(2),
        name="moe_ffn_ln",
    )(x1, gd, wg, wu, wd, g, b)


def _rope_tables(S):
    half = AT_DH // 2
    lane = jnp.arange(128)
    inv = ROPE_THETA ** (-(lane % half).astype(F32) / half)
    ang = jnp.arange(S, dtype=F32)[:, None] * inv[None, :]
    sign = jnp.where((lane % AT_DH) < half, -1.0, 1.0).astype(F32)
    return jnp.cos(ang), jnp.sin(ang) * sign[None, :]


def _pad_cols(a, width):
    return jnp.pad(a, ((0, 0), (0, width - a.shape[1])))


def kernel(x, w_in, ml_i_bias, ml_f_bias, ml_norm_w, ssm_conv_w, ssm_conv_b, ssm_dt_bias, ssm_a_log, ssm_d, ssm_norm_w, gla_gate_w2, gla_gate_b, gla_norm_w, w_out, ln1_g, ln1_b, router_w, router_bias, exp_w_gate, exp_w_up, exp_w_down, sh_w_gate, sh_w_up, sh_w_down, ln2_g, ln2_b):
    B, S, D = x.shape
    T = B * S
    depth = w_in.shape[0]
    assert D == 1024 and S % (AT_SPAN * max(DILATIONS)) == 0 and T % 1024 == 0
    cos, sin = _rope_tables(S)
    o_mi, o_sz, o_sdt, o_aq, o_gq, o_ga = 1024, 1032, 1800, 1804, 2572, 3340
    xf = x.reshape(T, D)
    for l in range(depth):
        w = w_in[l]
        wm = jnp.concatenate([w[:, 0:1024], w[:, o_sz:o_sdt], w[:, o_aq:o_gq], w[:, o_gq:o_ga]], axis=1).astype(BF16)
        wgt = _pad_cols(jnp.concatenate([w[:, o_mi:o_sz], w[:, o_sdt:o_aq], w[:, o_ga:o_ga + GLA_RANK]], axis=1),
                        128).astype(BF16)
        brow = _pad_cols(jnp.concatenate([ml_i_bias[l], ml_f_bias[l], ssm_dt_bias[l]])[None, :], 128).astype(F32)
        alog = jnp.pad(ssm_a_log[l].astype(F32), (G_DT, 128 - G_DT - HEADS))[None, :]
        w2 = jnp.pad(gla_gate_w2[l].astype(F32), ((G_GA, 128 - G_GA - GLA_RANK), (0, 0)))

        ml, ssm, at, gla, gates = _in_proj(xf, wm, wgt, cos, sin, S)
        y_ml = _mlstm(ml, gates, brow, ml_norm_w[l][None, :].astype(F32), B, S)
        y_ssm = _ssd(ssm, gates, brow, alog, ssm_conv_w[l].astype(F32), ssm_conv_b[l][None, :].astype(F32),
                     jnp.repeat(ssm_d[l].astype(F32), SSM_P)[None, :], ssm_norm_w[l][None, :].astype(F32), B, S)
        branches = [_attn(at, d, B, S) for d in DILATIONS]
        y_gla = _gla(gla, gates, w2, gla_gate_b[l][None, :].astype(F32), gla_norm_w[l][None, :].astype(F32), B, S)
        x1 = _out_proj(y_ml, y_ssm, branches[0][0], branches[1][0], branches[2][0],
                       branches[0][1], branches[1][1], branches[2][1], y_gla, xf,
                       w_out[l].astype(BF16), ln1_g[l][None, :].astype(F32), ln1_b[l][None, :].astype(F32))
        gd = _router(x1, router_w[l].T.astype(F32), router_bias[l][:, None].astype(F32))
        wg = jnp.concatenate([exp_w_gate[l], sh_w_gate[l][None]], axis=0).astype(BF16)
        wu = jnp.concatenate([exp_w_up[l], sh_w_up[l][None]], axis=0).astype(BF16)
        wd = jnp.concatenate([exp_w_down[l], sh_w_down[l][None]], axis=0).astype(BF16)
        xf = _moe(x1, gd, wg, wu, wd, ln2_g[l][None, :].astype(F32), ln2_b[l][None, :].astype(F32))
    return xf.reshape(B, S, D)
```

```python
import functools

import jax
import jax.numpy as jnp
from jax import lax
from jax.experimental import pallas as pl
from jax.experimental.pallas import tpu as pltpu

F32 = jnp.float32
BF16 = jnp.bfloat16
HI = lax.Precision.HIGHEST
NT = (((1,), (1,)), ((), ()))
TN = (((0,), (0,)), ((), ()))

DEPTH = 4
HEADS = 4
ML_DH = 64
SSM_P = 64
SSM_N = 64
SSM_GROUPS = 2
SSM_CONV = 4
AT_DH = 64
DILATIONS = (1, 4, 16)
AT_SPAN = 128
ROPE_THETA = 10000.0
GLA_DK = 32
GLA_DV = 64
GLA_RANK = 16
GLA_TAU = 16.0
N_EXPERTS = 64
TOP_K = 8
N_GROUPS = 8
TOPK_GROUPS = 4
ROUTED_SCALE = 2.5
ALPHA = (2 * DEPTH) ** 0.25
LN_EPS = 1e-5
CHUNK = 64
TOK_TILE = 256
ROW_PAD = 16
FFN_BLK = 256
LOC_ROWS = -(-(TOK_TILE * TOP_K + N_EXPERTS * (ROW_PAD - 1)) // 512) * 512

G_MI, G_MF, G_DT, G_GA = 0, 4, 8, 12

VMEM_LIMIT = 48 * 1024 * 1024


def _log_sigmoid(x):
    return jnp.minimum(x, 0.0) - jnp.log(1.0 + jnp.exp(-jnp.abs(x)))


def _softplus(x):
    return jnp.maximum(x, 0.0) + jnp.log(1.0 + jnp.exp(-jnp.abs(x)))


def _silu(x):
    return x * jax.nn.sigmoid(x)


def _layer_norm(z, g, b):
    mu = jnp.mean(z, axis=-1, keepdims=True)
    zc = z - mu
    var = jnp.mean(zc * zc, axis=-1, keepdims=True)
    return zc * lax.rsqrt(var + LN_EPS) * g + b


def _params(n_axes):
    return pltpu.CompilerParams(dimension_semantics=("arbitrary",) * n_axes, vmem_limit_bytes=VMEM_LIMIT)


def _in_proj_kernel(x_ref, wm_ref, wg_ref, cos_ref, sin_ref, ml_ref, ssm_ref, at_ref, gla_ref, g_ref):
    xb = x_ref[...].astype(BF16)

    def mm(lo):
        return jnp.dot(xb, wm_ref[:, lo:lo + 256], preferred_element_type=F32)

    for c in range(4):
        a = mm(256 * c)
        if c == 1:
            a = a * ML_DH ** -0.5
        ml_ref[:, 256 * c:256 * (c + 1)] = a.astype(BF16)
    for c in range(3):
        ssm_ref[:, 256 * c:256 * (c + 1)] = mm(1024 + 256 * c).astype(BF16)

    cos = cos_ref[...]
    sin = sin_ref[...]
    lane = lax.broadcasted_iota(jnp.int32, cos.shape, 1)
    first_half = (lane % AT_DH) < AT_DH // 2

    def rope(a):
        rot = jnp.where(first_half, pltpu.roll(a, 128 - AT_DH // 2, 1), pltpu.roll(a, AT_DH // 2, 1))
        return a * cos + rot * sin

    for c in range(3):
        a = mm(1792 + 256 * c)
        if c < 2:
            scale = AT_DH ** -0.5 if c == 0 else 1.0
            for hh in range(2):
                at_ref[:, 256 * c + 128 * hh:256 * c + 128 * (hh + 1)] = (
                    rope(a[:, 128 * hh:128 * (hh + 1)]) * scale).astype(BF16)
        else:
            at_ref[:, 512:768] = a.astype(BF16)
    for c in range(3):
        gla_ref[:, 256 * c:256 * (c + 1)] = mm(2560 + 256 * c).astype(BF16)
    g_ref[...] = jnp.dot(xb, wg_ref[...], preferred_element_type=F32)


def _in_proj(x, wm, wg, cos, sin, S, tm=512):
    T, D = x.shape
    nS = S // tm
    row = lambda i: (i, 0)
    full = lambda i: (0, 0)
    return pl.pallas_call(
        _in_proj_kernel,
        grid=(T // tm,),
        in_specs=[pl.BlockSpec((tm, D), row), pl.BlockSpec(wm.shape, full), pl.BlockSpec(wg.shape, full),
                  pl.BlockSpec((tm, 128), lambda i: (i % nS, 0)), pl.BlockSpec((tm, 128), lambda i: (i % nS, 0))],
        out_specs=[pl.BlockSpec((tm, 1024), row), pl.BlockSpec((tm, 768), row), pl.BlockSpec((tm, 768), row),
                   pl.BlockSpec((tm, 768), row), pl.BlockSpec((tm, 128), row)],
        out_shape=[jax.ShapeDtypeStruct((T, 1024), BF16), jax.ShapeDtypeStruct((T, 768), BF16),
                   jax.ShapeDtypeStruct((T, 768), BF16), jax.ShapeDtypeStruct((T, 768), BF16),
                   jax.ShapeDtypeStruct((T, 128), F32)],
        compiler_params=_params(1),
        name="in_proj",
    )(x, wm, wg, cos, sin)


def _tri_consts(L):
    ri = lax.broadcasted_iota(jnp.int32, (L, L), 0)
    ci = lax.broadcasted_iota(jnp.int32, (L, L), 1)
    causal = ri >= ci
    return causal, causal.astype(F32), (ri <= ci).astype(F32)


def _mlstm_kernel(ml_ref, g_ref, brow_ref, nw_ref, y_ref, c_ref, m_ref, *, CH):
    @pl.when(pl.program_id(1) == 0)
    def _init():
        c_ref[...] = jnp.zeros_like(c_ref)
        m_ref[...] = jnp.full_like(m_ref, -jnp.inf)

    L = CHUNK
    causal, tril, triu = _tri_consts(L)
    lane = lax.broadcasted_iota(jnp.int32, (128, 128), 1)
    is_f = (lane >= G_MF) & (lane < G_MF + HEADS)
    ones_v = jnp.ones((L, ML_DH), BF16)
    nw = nw_ref[...]

    def pair(pi, carry):
        r0 = pl.multiple_of(pi * 128, 128)
        g2 = g_ref[pl.ds(r0, 128), :] + brow_ref[...]
        vals = jnp.where(is_f, _log_sigmoid(g2), g2)
        vals_t = vals.T
        for half in range(2):
            rr = pl.multiple_of(r0 + half * L, L)
            vc = vals[half * L:(half + 1) * L, :]
            vtc = vals_t[0:8, half * L:(half + 1) * L]
            b_all = jnp.dot(tril, vc, precision=HI, preferred_element_type=F32)
            bt_all = jnp.dot(vtc, triu, precision=HI, preferred_element_type=F32)
            for h in range(HEADS):
                q = ml_ref[pl.ds(rr, L), h * 64:(h + 1) * 64]
                k = ml_ref[pl.ds(rr, L), 256 + h * 64:256 + (h + 1) * 64]
                v = ml_ref[pl.ds(rr, L), 512 + h * 64:512 + (h + 1) * 64]
                og = ml_ref[pl.ds(rr, L), 768 + h * 64:768 + (h + 1) * 64].astype(F32)
                b_col = b_all[:, G_MF + h:G_MF + h + 1]
                li_col = vc[:, G_MI + h:G_MI + h + 1]
                b_row = bt_all[G_MF + h:G_MF + h + 1, :]
                li_row = vtc[G_MI + h:G_MI + h + 1, :]
                m_prev = m_ref[h][0:1, 0:1]
                cst = c_ref[h]

                log_d = jnp.where(causal, b_col - b_row + li_row, -jnp.inf)
                log_inter = b_col + m_prev
                m_t = jnp.maximum(log_inter, jnp.max(log_d, axis=1, keepdims=True))
                w_inter = jnp.exp(log_inter - m_t)
                s = lax.dot_general(q, k, NT, preferred_element_type=F32) * jnp.exp(log_d - m_t)
                qc = jnp.dot(q, cst.astype(BF16), preferred_element_type=F32)
                num = jnp.dot(s.astype(BF16), v, preferred_element_type=F32) + w_inter * qc[:, 0:64]
                den = jnp.sum(s, axis=1, keepdims=True) + w_inter * qc[:, 64:65]
                hh = num / jnp.maximum(jnp.abs(den), jnp.exp(-m_t))

                b_last = b_col[L - 1:L, :]
                log_s = b_last - b_col + li_col
                m_new = jnp.maximum(b_last + m_prev, jnp.max(log_s, axis=0, keepdims=True))
                w_c = jnp.exp(b_last + m_prev - m_new)
                kw = (k.astype(F32) * jnp.exp(log_s - m_new)).astype(BF16)
                c_ref[h, :, 0:64] = w_c * cst[:, 0:64] + lax.dot_general(kw, v, TN, preferred_element_type=F32)
                c_ref[h, :, 64:128] = w_c * cst[:, 64:128] + lax.dot_general(kw, ones_v, TN,
                                                                              preferred_element_type=F32)
                m_ref[h] = jnp.broadcast_to(m_new, (8, 128))

                hc = hh - jnp.mean(hh, axis=1, keepdims=True)
                hn = hc * lax.rsqrt(jnp.mean(hc * hc, axis=1, keepdims=True) + LN_EPS) * nw[:, h * 64:(h + 1) * 64]
                y_ref[pl.ds(rr, L), h * 64:(h + 1) * 64] = (jax.nn.sigmoid(og) * hn).astype(BF16)
        return carry

    lax.fori_loop(0, CH // 128, pair, 0)


def _mlstm(ml, gates, brow, nw, B, S, CH=512):
    T = ml.shape[0]
    nS = S // CH
    row = lambda b, j: (b * nS + j, 0)
    full = lambda b, j: (0, 0)
    return pl.pallas_call(
        functools.partial(_mlstm_kernel, CH=CH),
        grid=(B, nS),
        in_specs=[pl.BlockSpec((CH, 1024), row), pl.BlockSpec((CH, 128), row),
                  pl.BlockSpec((1, 128), full), pl.BlockSpec((1, 256), full)],
        out_specs=pl.BlockSpec((CH, 256), row),
        out_shape=jax.ShapeDtypeStruct((T, 256), BF16),
        scratch_shapes=[pltpu.VMEM((HEADS, ML_DH, 128), F32), pltpu.VMEM((HEADS, 8, 128), F32)],
        compiler_params=_params(2),
        name="mlstm",
    )(ml, gates, brow, nw)


def _ssd_kernel(ssm_ref, g_ref, brow_ref, alog_ref, cw_ref, cb_ref, d_ref, nw_ref, y_ref,
                xbuf_ref, xact_ref, st_ref, *, CH):
    @pl.when(pl.program_id(1) == 0)
    def _init():
        xbuf_ref[0:8, :] = jnp.zeros((8, 512), F32)
        st_ref[...] = jnp.zeros_like(st_ref)

    xbuf_ref[8:CH + 8, :] = ssm_ref[:, 256:768].astype(F32)
    conv = cb_ref[...] + cw_ref[0:1, :] * xbuf_ref[5:5 + CH, :]
    for j in range(1, SSM_CONV):
        conv = conv + cw_ref[j:j + 1, :] * xbuf_ref[5 + j:5 + j + CH, :]
    xact_ref[...] = _silu(conv)
    xbuf_ref[0:8, :] = xbuf_ref[CH:CH + 8, :]

    L = CHUNK
    causal, tril, triu = _tri_consts(L)
    lane = lax.broadcasted_iota(jnp.int32, (1, 128), 1)
    a_row = jnp.where((lane >= G_DT) & (lane < G_DT + HEADS), -jnp.exp(alog_ref[...]), 0.0)
    dskip = d_ref[...]
    nw = nw_ref[...]

    def pair(pi, carry):
        r0 = pl.multiple_of(pi * 128, 128)
        dt2 = _softplus(g_ref[pl.ds(r0, 128), :] + brow_ref[...])
        a2 = dt2 * a_row
        a2_t = a2.T
        for half in range(2):
            rr = pl.multiple_of(r0 + half * L, L)
            acs_all = jnp.dot(tril, a2[half * L:(half + 1) * L, :], precision=HI, preferred_element_type=F32)
            acs_t = jnp.dot(a2_t[G_DT:G_DT + 8, half * L:(half + 1) * L], triu, precision=HI,
                            preferred_element_type=F32)
            cb = []
            bmat = []
            cmat = []
            for g in range(SSM_GROUPS):
                bm = xact_ref[pl.ds(rr, L), 256 + g * 64:256 + (g + 1) * 64]
                cm = xact_ref[pl.ds(rr, L), 384 + g * 64:384 + (g + 1) * 64].astype(BF16)
                bmat.append(bm)
                cmat.append(cm)
                cb.append(lax.dot_general(cm, bm.astype(BF16), NT, preferred_element_type=F32))
            gated = []
            ssq = jnp.zeros((L, 1), F32)
            for h in range(HEADS):
                g = h // (HEADS // SSM_GROUPS)
                acs_col = acs_all[:, G_DT + h:G_DT + h + 1]
                acs_row = acs_t[h:h + 1, :]
                dt_col = dt2[half * L:(half + 1) * L, G_DT + h:G_DT + h + 1]
                xh = xact_ref[pl.ds(rr, L), h * 64:(h + 1) * 64]
                xdt = (xh * dt_col).astype(BF16)
                st = st_ref[h]
                mmat = cb[g] * jnp.exp(jnp.where(causal, acs_col - acs_row, -jnp.inf))
                y = jnp.dot(mmat.astype(BF16), xdt, preferred_element_type=F32)
                y = y + jnp.dot(cmat[g], st.astype(BF16), preferred_element_type=F32) * jnp.exp(acs_col)
                y = y + xh * dskip[:, h * 64:(h + 1) * 64]
                acs_last = acs_col[L - 1:L, :]
                bdec = (bmat[g] * jnp.exp(acs_last - acs_col)).astype(BF16)
                st_ref[h] = jnp.exp(acs_last) * st + lax.dot_general(bdec, xdt, TN, preferred_element_type=F32)
                z = ssm_ref[pl.ds(rr, L), h * 64:(h + 1) * 64].astype(F32)
                yg = y * _silu(z)
                ssq = ssq + jnp.sum(yg * yg, axis=1, keepdims=True)
                gated.append(yg)
            scale = lax.rsqrt(ssq / (HEADS * SSM_P) + LN_EPS)
            for h in range(HEADS):
                y_ref[pl.ds(rr, L), h * 64:(h + 1) * 64] = (gated[h] * scale * nw[:, h * 64:(h + 1) * 64]).astype(BF16)
        return carry

    lax.fori_loop(0, CH // 128, pair, 0)


def _ssd(ssm, gates, brow, alog, cw, cb, dskip, nw, B, S, CH=512):
    T = ssm.shape[0]
    nS = S // CH
    row = lambda b, j: (b * nS + j, 0)
    full = lambda b, j: (0, 0)
    return pl.pallas_call(
        functools.partial(_ssd_kernel, CH=CH),
        grid=(B, nS),
        in_specs=[pl.BlockSpec((CH, 768), row), pl.BlockSpec((CH, 128), row), pl.BlockSpec((1, 128), full),
                  pl.BlockSpec((1, 128), full), pl.BlockSpec((SSM_CONV, 512), full), pl.BlockSpec((1, 512), full),
                  pl.BlockSpec((1, 256), full), pl.BlockSpec((1, 256), full)],
        out_specs=pl.BlockSpec((CH, 256), row),
        out_shape=jax.ShapeDtypeStruct((T, 256), BF16),
        scratch_shapes=[pltpu.VMEM((CH + 8, 512), F32), pltpu.VMEM((CH, 512), F32),
                        pltpu.VMEM((HEADS, SSM_N, SSM_P), F32)],
        compiler_params=_params(2),
        name="ssd",
    )(ssm, gates, brow, alog, cw, cb, dskip, nw)


def _attn_kernel(at_ref, o_ref, lse_ref, *, N):
    W = AT_SPAN
    ri = lax.broadcasted_iota(jnp.int32, (W, W), 0)
    ci = lax.broadcasted_iota(jnp.int32, (W, W), 1)
    cur_ok = ri >= ci
    prev_ok = ci >= ri

    def blk(n, carry):
        r0 = pl.multiple_of(n * W, W)
        rp = pl.multiple_of(jnp.maximum(n - 1, 0) * W, W)
        pmask = prev_ok & (n > 0)
        for h in range(HEADS):
            q = at_ref[pl.ds(r0, W), h * 64:(h + 1) * 64]
            kc = at_ref[pl.ds(r0, W), 256 + h * 64:256 + (h + 1) * 64]
            kp = at_ref[pl.ds(rp, W), 256 + h * 64:256 + (h + 1) * 64]
            vc = at_ref[pl.ds(r0, W), 512 + h * 64:512 + (h + 1) * 64]
            vp = at_ref[pl.ds(rp, W), 512 + h * 64:512 + (h + 1) * 64]
            sc = jnp.where(cur_ok, lax.dot_general(q, kc, NT, preferred_element_type=F32), -jnp.inf)
            sp = jnp.where(pmask, lax.dot_general(q, kp, NT, preferred_element_type=F32), -jnp.inf)
            m = jnp.maximum(jnp.max(sc, axis=1, keepdims=True), jnp.max(sp, axis=1, keepdims=True))
            pc = jnp.exp(sc - m)
            pp = jnp.exp(sp - m)
            den = jnp.sum(pc, axis=1, keepdims=True) + jnp.sum(pp, axis=1, keepdims=True)
            acc = jnp.dot(pc.astype(BF16), vc, preferred_element_type=F32)
            acc = acc + jnp.dot(pp.astype(BF16), vp, preferred_element_type=F32)
            o_ref[pl.ds(r0, W), h * 64:(h + 1) * 64] = (acc / den).astype(BF16)
            lse_ref[pl.ds(r0, W), h * 64:(h + 1) * 64] = jnp.broadcast_to(m + jnp.log(den), (W, 64))
        return carry

    lax.fori_loop(0, N // W, blk, 0)


def _attn(at, d, B, S):
    N = S // d
    atv = at.reshape(B, N, d * 768)
    o, lse = pl.pallas_call(
        functools.partial(_attn_kernel, N=N),
        grid=(B, d),
        in_specs=[pl.BlockSpec((None, N, 768), lambda b, r: (b, 0, r))],
        out_specs=[pl.BlockSpec((None, N, 256), lambda b, r: (b, 0, r)),
                   pl.BlockSpec((None, N, 256), lambda b, r: (b, 0, r))],
        out_shape=[jax.ShapeDtypeStruct((B, N, d * 256), BF16), jax.ShapeDtypeStruct((B, N, d * 256), F32)],
        compiler_params=_params(2),
        name=f"attn_d{d}",
    )(atv)
    return o.reshape(B * S, 256), lse.reshape(B * S, 256)


def _gla_kernel(gla_ref, g_ref, w2_ref, b2_ref, nw_ref, y_ref, st_ref, *, CH):
    @pl.when(pl.program_id(1) == 0)
    def _init():
        st_ref[...] = jnp.zeros_like(st_ref)

    L = CHUNK
    SB = 16
    ri = lax.broadcasted_iota(jnp.int32, (L, L), 0)
    ci = lax.broadcasted_iota(jnp.int32, (L, L), 1)
    tril_blk = ((ri >= ci) & (ri // SB == ci // SB)).astype(F32)
    tl = lax.broadcasted_iota(jnp.int32, (L, 128), 0) % SB
    er = lax.broadcasted_iota(jnp.int32, (128, 256), 0) // GLA_DK
    ec = lax.broadcasted_iota(jnp.int32, (128, 256), 1) // GLA_DV
    head_expand = (er == ec).astype(BF16)
    sr = lax.broadcasted_iota(jnp.int32, (256, 128), 0) // GLA_DV
    sc = lax.broadcasted_iota(jnp.int32, (256, 128), 1) // GLA_DK
    st_mask = (sr == sc).astype(F32)
    nw = nw_ref[...]

    def chunk(ci_, carry):
        rr = pl.multiple_of(ci_ * L, L)
        lg = _log_sigmoid(jnp.dot(g_ref[pl.ds(rr, L), :], w2_ref[...], preferred_element_type=F32)
                          + b2_ref[...]) / GLA_TAU
        c = jnp.dot(tril_blk, lg, precision=HI, preferred_element_type=F32)
        q = gla_ref[pl.ds(rr, L), 0:128].astype(F32) * GLA_DK ** -0.5
        k = gla_ref[pl.ds(rr, L), 128:256].astype(F32)
        vb = gla_ref[pl.ds(rr, L), 256:512]
        v = vb.astype(F32)
        c4 = c.reshape(L // SB, SB, 128)
        k4 = k.reshape(L // SB, SB, 128)
        v4 = v.reshape(L // SB, SB, 256)

        def bcast(x4, j, width):
            return jnp.broadcast_to(x4[:, j:j + 1, :], (L // SB, SB, width)).reshape(L, width)

        o = jnp.zeros((L, 256), F32)
        for j in range(SB):
            dec = jnp.exp(jnp.minimum(c - bcast(c4, j, 128), 0.0))
            p = jnp.where(tl >= j, q * dec * bcast(k4, j, 128), 0.0)
            a = jnp.dot(p.astype(BF16), head_expand, preferred_element_type=F32)
            o = o + a * bcast(v4, j, 256)

        outs = []
        for i in range(L // SB):
            cblk = c[i * SB:(i + 1) * SB, :]
            clast = cblk[SB - 1:SB, :]
            st = st_ref[...]
            qe = (q[i * SB:(i + 1) * SB, :] * jnp.exp(cblk)).astype(BF16)
            outs.append(o[i * SB:(i + 1) * SB, :] + lax.dot_general(qe, st.astype(BF16), NT,
                                                                    preferred_element_type=F32))
            ke = (k[i * SB:(i + 1) * SB, :] * jnp.exp(clast - cblk)).astype(BF16)
            upd = lax.dot_general(vb[i * SB:(i + 1) * SB, :], ke, TN, preferred_element_type=F32)
            st_ref[...] = st * jnp.exp(clast) + st_mask * upd
        for i in range(L // SB):
            rg = gla_ref[pl.ds(pl.multiple_of(rr + i * SB, SB), SB), 512:768].astype(F32)
            for h in range(HEADS):
                oh = outs[i][:, h * 64:(h + 1) * 64]
                rms = lax.rsqrt(jnp.mean(oh * oh, axis=1, keepdims=True) + LN_EPS)
                y_ref[pl.ds(pl.multiple_of(rr + i * SB, SB), SB), h * 64:(h + 1) * 64] = (
                    oh * rms * nw[:, h * 64:(h + 1) * 64] * _silu(rg[:, h * 64:(h + 1) * 64])).astype(BF16)
        return carry

    lax.fori_loop(0, CH // L, chunk, 0)


def _gla(gla, gates, w2, b2, nw, B, S, CH=512):
    T = gla.shape[0]
    nS = S // CH
    row = lambda b, j: (b * nS + j, 0)
    full = lambda b, j: (0, 0)
    return pl.pallas_call(
        functools.partial(_gla_kernel, CH=CH),
        grid=(B, nS),
        in_specs=[pl.BlockSpec((CH, 768), row), pl.BlockSpec((CH, 128), row), pl.BlockSpec((128, 128), full),
                  pl.BlockSpec((1, 128), full), pl.BlockSpec((1, 256), full)],
        out_specs=pl.BlockSpec((CH, 256), row),
        out_shape=jax.ShapeDtypeStruct((T, 256), BF16),
        scratch_shapes=[pltpu.VMEM((HEADS * GLA_DV, HEADS * GLA_DK), F32)],
        compiler_params=_params(2),
        name="gla",
    )(gla, gates, w2, b2, nw)


def _out_proj_kernel(yml_ref, yssm_ref, o1_ref, o2_ref, o3_ref, l1_ref, l2_ref, l3_ref, ygla_ref, x_ref,
                     wo_ref, g_ref, b_ref, x1_ref):
    l1, l2, l3 = l1_ref[...], l2_ref[...], l3_ref[...]
    mx = jnp.maximum(jnp.maximum(l1, l2), l3)
    e1, e2, e3 = jnp.exp(l1 - mx), jnp.exp(l2 - mx), jnp.exp(l3 - mx)
    yat = (e1 * o1_ref[...].astype(F32) + e2 * o2_ref[...].astype(F32) + e3 * o3_ref[...].astype(F32)) / (e1 + e2 + e3)
    acc = jnp.dot(yml_ref[...], wo_ref[0:256, :], preferred_element_type=F32)
    acc = acc + jnp.dot(yssm_ref[...], wo_ref[256:512, :], preferred_element_type=F32)
    acc = acc + jnp.dot(yat.astype(BF16), wo_ref[512:768, :], preferred_element_type=F32)
    acc = acc + jnp.dot(ygla_ref[...], wo_ref[768:1024, :], preferred_element_type=F32)
    x1_ref[...] = _layer_norm(ALPHA * x_ref[...] + acc, g_ref[...], b_ref[...])


def _out_proj(yml, yssm, o1, o2, o3, l1, l2, l3, ygla, x, wo, g, b, tm=512):
    T, D = x.shape
    row = lambda i: (i, 0)
    full = lambda i: (0, 0)
    small = pl.BlockSpec((tm, 256), row)
    return pl.pallas_call(
        _out_proj_kernel,
        grid=(T // tm,),
        in_specs=[small] * 9 + [pl.BlockSpec((tm, D), row), pl.BlockSpec(wo.shape, full),
                                pl.BlockSpec((1, D), full), pl.BlockSpec((1, D), full)],
        out_specs=pl.BlockSpec((tm, D), row),
        out_shape=jax.ShapeDtypeStruct((T, D), F32),
        compiler_params=_params(1),
        name="out_proj_ln",
    )(yml, yssm, o1, o2, o3, l1, l2, l3, ygla, x, wo, g, b)


def _router_kernel(x_ref, wr_ref, rb_ref, row_ref, col_ref, cnt_ref, rt_ref):
    tm = x_ref.shape[0]
    gsz = N_EXPERTS // N_GROUPS
    logits = lax.dot_general(wr_ref[...], x_ref[...], NT, precision=HI, preferred_element_type=F32)
    scores = jax.nn.sigmoid(logits)
    sel = scores + rb_ref[...]
    eidx = lax.broadcasted_iota(jnp.int32, (gsz, tm), 0)
    big = jnp.int32(1 << 20)
    neg = -jnp.inf

    sel_g = [sel[g * gsz:(g + 1) * gsz, :] for g in range(N_GROUPS)]
    idx_g = [eidx + g * gsz for g in range(N_GROUPS)]
    gscore = []
    for g in range(N_GROUPS):
        v = sel_g[g]
        m1 = jnp.max(v, axis=0, keepdims=True)
        i1 = jnp.min(jnp.where(v == m1, idx_g[g], big), axis=0, keepdims=True)
        m2 = jnp.max(jnp.where(idx_g[g] == i1, neg, v), axis=0, keepdims=True)
        gscore.append(m1 + m2)
    gkeep = [jnp.zeros((1, tm), jnp.bool_) for _ in range(N_GROUPS)]
    for _ in range(TOPK_GROUPS):
        m = functools.reduce(jnp.maximum, gscore)
        gi = functools.reduce(jnp.minimum, [jnp.where(gscore[g] == m, g, big) for g in range(N_GROUPS)])
        for g in range(N_GROUPS):
            hit = gi == g
            gkeep[g] = gkeep[g] | hit
            gscore[g] = jnp.where(hit, neg, gscore[g])
    cand = [jnp.where(gkeep[g], sel_g[g], neg) for g in range(N_GROUPS)]
    chosen = [jnp.zeros((gsz, tm), jnp.bool_) for _ in range(N_GROUPS)]
    picks = []
    for _ in range(TOP_K):
        m = functools.reduce(jnp.maximum, [jnp.max(c, axis=0, keepdims=True) for c in cand])
        ei = functools.reduce(jnp.minimum, [jnp.min(jnp.where(cand[g] == m, idx_g[g], big), axis=0, keepdims=True)
                                            for g in range(N_GROUPS)])
        picks.append(ei)
        for g in range(N_GROUPS):
            hit = idx_g[g] == ei
            chosen[g] = chosen[g] | hit
            cand[g] = jnp.where(hit, neg, cand[g])
    picked = [jnp.where(chosen[g], scores[g * gsz:(g + 1) * gsz, :], 0.0) for g in range(N_GROUPS)]
    tot = functools.reduce(jnp.add, [jnp.sum(p, axis=0, keepdims=True) for p in picked])
    gates = [p / tot * ROUTED_SCALE for p in picked]

    chosen_b = jnp.concatenate([c.astype(F32) for c in chosen] + [jnp.zeros((N_EXPERTS, tm), F32)], axis=0).astype(BF16)
    before = (lax.broadcasted_iota(jnp.int32, (tm, tm), 0) < lax.broadcasted_iota(jnp.int32, (tm, tm), 1)).astype(BF16)
    rank = jnp.dot(chosen_b[0:N_EXPERTS, :], before, preferred_element_type=F32)
    cnt = jnp.dot(chosen_b[0:N_EXPERTS, :], jnp.ones((tm, 128), BF16), preferred_element_type=F32)
    padded = jnp.floor((cnt + (ROW_PAD - 1)) * (1.0 / ROW_PAD)) * ROW_PAD
    lower = (lax.broadcasted_iota(jnp.int32, (N_EXPERTS, N_EXPERTS), 0)
             > lax.broadcasted_iota(jnp.int32, (N_EXPERTS, N_EXPERTS), 1)).astype(F32)
    gstart = jnp.dot(lower, padded, precision=HI, preferred_element_type=F32)
    posmat = gstart[:, 0:1] + rank
    rt_ref[...] = jnp.zeros_like(rt_ref)
    for k in range(TOP_K):
        pos_k = jnp.zeros((1, tm), F32)
        gate_k = jnp.zeros((1, tm), F32)
        for g in range(N_GROUPS):
            hit = idx_g[g] == picks[k]
            pos_k = pos_k + jnp.sum(jnp.where(hit, posmat[g * gsz:(g + 1) * gsz, :], 0.0), axis=0, keepdims=True)
            gate_k = gate_k + jnp.sum(jnp.where(hit, gates[g], 0.0), axis=0, keepdims=True)
        rt_ref[k:k + 1, :] = pos_k
        rt_ref[TOP_K + k:TOP_K + k + 1, :] = gate_k
    row_ref[...] = rt_ref[0:2 * TOP_K, :]
    col_ref[...] = rt_ref[...].T
    cnt_ref[...] = lax.dot_general(jnp.ones((8, tm), BF16), chosen_b, NT, preferred_element_type=F32)


def _router(x1, wr_t, rb):
    T, D = x1.shape
    tm = TOK_TILE
    nst = T // tm
    return pl.pallas_call(
        _router_kernel,
        grid=(nst,),
        in_specs=[pl.BlockSpec((tm, D), lambda i: (i, 0)), pl.BlockSpec(wr_t.shape, lambda i: (0, 0)),
                  pl.BlockSpec(rb.shape, lambda i: (0, 0))],
        out_specs=[pl.BlockSpec((2 * TOP_K, tm), lambda i: (0, i)), pl.BlockSpec((tm, 128), lambda i: (i, 0)),
                   pl.BlockSpec((8, 128), lambda i: (i, 0))],
        out_shape=[jax.ShapeDtypeStruct((2 * TOP_K, T), F32), jax.ShapeDtypeStruct((T, 128), F32),
                   jax.ShapeDtypeStruct((nst * 8, 128), F32)],
        scratch_shapes=[pltpu.VMEM((128, tm), F32)],
        compiler_params=_params(1),
        name="router",
    )(x1, wr_t, rb)


M_LSTART, M_ROWS, M_GOFF, M_PREV = 0, N_EXPERTS, 2 * N_EXPERTS, 3 * N_EXPERTS


def _seg_copy(meta_ref, e, col, loc_ref, glob_ref, sem, to_global, placed=True):
    n = pl.multiple_of(meta_ref[0, 0, col + e] * ROW_PAD, ROW_PAD)
    ls = pl.multiple_of(meta_ref[0, 0, M_LSTART + e] * ROW_PAD, ROW_PAD) if placed else 0
    go = pl.multiple_of(meta_ref[0, 0, M_GOFF + e] * ROW_PAD, ROW_PAD) if placed else 0
    loc, glob = loc_ref.at[pl.ds(ls, n)], glob_ref.at[pl.ds(go, n)]
    return (pltpu.make_async_copy(loc, glob, sem) if to_global else pltpu.make_async_copy(glob, loc, sem)), n


def _dispatch_kernel(meta_ref, tail_ref, x_ref, row_ref, xg_ref, loc_ref, zero_ref, sems):
    s = pl.program_id(0)
    last = pl.num_programs(0) - 1
    slot = s % 2
    tm = x_ref.shape[0]

    @pl.when(s == 0)
    def _fill_tails():
        zero_ref[...] = jnp.zeros_like(zero_ref)

        def start(e, c):
            n = pl.multiple_of(tail_ref[1, e] * ROW_PAD, ROW_PAD)
            go = pl.multiple_of(tail_ref[0, e] * ROW_PAD, ROW_PAD)

            @pl.when(n > 0)
            def _():
                pltpu.make_async_copy(zero_ref.at[pl.ds(0, n)], xg_ref.at[pl.ds(go, n)], sems.at[2]).start()
            return c
        lax.fori_loop(0, N_EXPERTS, start, 0)

    xb = x_ref[...].astype(BF16)
    rows = lax.broadcasted_iota(jnp.int32, (256, tm), 0).astype(F32)
    for c in range(LOC_ROWS // 256):
        onehot = jnp.zeros((256, tm), F32)
        for k in range(TOP_K):
            onehot = onehot + jnp.where(row_ref[k:k + 1, :] == rows + (c * 256), 1.0, 0.0)
        loc_ref[slot, c * 256:(c + 1) * 256, :] = jnp.dot(onehot.astype(BF16), xb,
                                                          preferred_element_type=F32).astype(BF16)

    def start(e, c):
        cp, n = _seg_copy(meta_ref, e, M_ROWS, loc_ref.at[slot], xg_ref, sems.at[slot], True)

        @pl.when(n > 0)
        def _():
            cp.start()
        return c
    lax.fori_loop(0, N_EXPERTS, start, 0)

    def wait_for(col, buf):
        def wait(e, c):
            cp, n = _seg_copy(meta_ref, e, col, loc_ref.at[buf], xg_ref, sems.at[buf], True, placed=False)

            @pl.when(n > 0)
            def _():
                cp.wait()
            return c
        lax.fori_loop(0, N_EXPERTS, wait, 0)

    @pl.when(s > 0)
    def _wait_prev():
        wait_for(M_PREV, 1 - slot)

    @pl.when(s == last)
    def _wait_own():
        wait_for(M_ROWS, slot)

    @pl.when(s == 0)
    def _wait_tails():
        def wait(e, c):
            n = pl.multiple_of(tail_ref[1, e] * ROW_PAD, ROW_PAD)
            go = pl.multiple_of(tail_ref[0, e] * ROW_PAD, ROW_PAD)

            @pl.when(n > 0)
            def _():
                pltpu.make_async_copy(zero_ref.at[pl.ds(0, n)], xg_ref.at[pl.ds(go, n)], sems.at[2]).wait()
            return c
        lax.fori_loop(0, N_EXPERTS, wait, 0)


def _dispatch(meta, tail, x1, rowform, p_rows):
    T, D = x1.shape
    tm = TOK_TILE
    return pl.pallas_call(
        _dispatch_kernel,
        grid=(T // tm,),
        in_specs=[pl.BlockSpec((1, 1, 4 * N_EXPERTS), lambda i: (i, 0, 0), memory_space=pltpu.SMEM),
                  pl.BlockSpec(memory_space=pltpu.SMEM),
                  pl.BlockSpec((tm, D), lambda i: (i, 0)), pl.BlockSpec((2 * TOP_K, tm), lambda i: (0, i))],
        out_specs=pl.BlockSpec(memory_space=pl.ANY),
        out_shape=jax.ShapeDtypeStruct((p_rows, D), BF16),
        scratch_shapes=[pltpu.VMEM((2, LOC_ROWS, D), BF16), pltpu.VMEM((FFN_BLK, D), BF16),
                        pltpu.SemaphoreType.DMA((3,))],
        compiler_params=_params(1),
        name="moe_dispatch",
    )(meta, tail, x1, rowform)


def _ffn_kernel(bexp_ref, nused_ref, x_ref, wg_ref, wu_ref, wd_ref, y_ref):
    @pl.when(pl.program_id(0) < nused_ref[0])
    def _():
        xb = x_ref[...]
        a = jnp.dot(xb, wg_ref[...], preferred_element_type=F32)
        u = jnp.dot(xb, wu_ref[...], preferred_element_type=F32)
        y_ref[...] = jnp.dot((_silu(a) * u).astype(BF16), wd_ref[...], preferred_element_type=F32).astype(BF16)


def _ffn(blk_exp, nused, xg, wg, wu, wd):
    P, D = xg.shape
    F = wg.shape[2]
    blk = lambda i, be, nu: (jnp.minimum(i, nu[0] - 1), 0)
    wsel = lambda i, be, nu: (be[jnp.minimum(i, nu[0] - 1)], 0, 0)
    return pl.pallas_call(
        _ffn_kernel,
        grid_spec=pltpu.PrefetchScalarGridSpec(
            num_scalar_prefetch=2,
            grid=(P // FFN_BLK,),
            in_specs=[pl.BlockSpec((FFN_BLK, D), blk), pl.BlockSpec((None, D, F), wsel),
                      pl.BlockSpec((None, D, F), wsel), pl.BlockSpec((None, F, D), wsel)],
            out_specs=pl.BlockSpec((FFN_BLK, D), blk)),
        out_shape=jax.ShapeDtypeStruct((P, D), BF16),
        compiler_params=_params(1),
        name="moe_ffn",
    )(blk_exp, nused, xg, wg, wu, wd)


def _combine_kernel(meta_ref, x_ref, col_ref, yg_ref, sg_ref, su_ref, sd_ref, g_ref, b_ref, x2_ref, loc_ref, sem):
    tm = x_ref.shape[0]

    def start(e, c):
        cp, n = _seg_copy(meta_ref, e, M_ROWS, loc_ref, yg_ref, sem, False)

        @pl.when(n > 0)
        def _():
            cp.start()
        return c
    lax.fori_loop(0, N_EXPERTS, start, 0)

    used = meta_ref[0, 0, M_LSTART + N_EXPERTS - 1] + meta_ref[0, 0, M_ROWS + N_EXPERTS - 1]

    def clear(c, carry):
        loc_ref[pl.ds(pl.multiple_of(c * ROW_PAD, ROW_PAD), ROW_PAD), :] = jnp.zeros((ROW_PAD, loc_ref.shape[1]), BF16)
        return carry
    lax.fori_loop(used, LOC_ROWS // ROW_PAD, clear, 0)

    x1 = x_ref[...]
    xb = x1.astype(BF16)
    hid = _silu(jnp.dot(xb, sg_ref[...], preferred_element_type=F32)) * jnp.dot(xb, su_ref[...],
                                                                                preferred_element_type=F32)
    acc = jnp.dot(hid.astype(BF16), sd_ref[...], preferred_element_type=F32)

    def wait(e, c):
        cp, n = _seg_copy(meta_ref, e, M_ROWS, loc_ref, yg_ref, sem, False)

        @pl.when(n > 0)
        def _():
            cp.wait()
        return c
    lax.fori_loop(0, N_EXPERTS, wait, 0)

    CW = 512
    lanes = lax.broadcasted_iota(jnp.int32, (tm, CW), 1).astype(F32)
    for c in range(LOC_ROWS // CW):
        wmat = jnp.zeros((tm, CW), F32)
        for k in range(TOP_K):
            wmat = wmat + jnp.where(col_ref[:, k:k + 1] == lanes + (c * CW), col_ref[:, TOP_K + k:TOP_K + k + 1], 0.0)
        acc = acc + jnp.dot(wmat.astype(BF16), loc_ref[c * CW:(c + 1) * CW, :], preferred_element_type=F32)
    x2_ref[...] = _layer_norm(ALPHA * x1 + acc, g_ref[...], b_ref[...])


def _combine(meta, x1, colform, yg, sg, su, sd, g, b):
    T, D = x1.shape
    tm = TOK_TILE
    full = lambda i: (0, 0)
    return pl.pallas_call(
        _combine_kernel,
        grid=(T // tm,),
        in_specs=[pl.BlockSpec((1, 1, 4 * N_EXPERTS), lambda i: (i, 0, 0), memory_space=pltpu.SMEM),
                  pl.BlockSpec((tm, D), lambda i: (i, 0)), pl.BlockSpec((tm, 128), lambda i: (i, 0)),
                  pl.BlockSpec(memory_space=pl.ANY),
                  pl.BlockSpec(sg.shape, full), pl.BlockSpec(su.shape, full), pl.BlockSpec(sd.shape, full),
                  pl.BlockSpec((1, D), full), pl.BlockSpec((1, D), full)],
        out_specs=pl.BlockSpec((tm, D), lambda i: (i, 0)),
        out_shape=jax.ShapeDtypeStruct((T, D), F32),
        scratch_shapes=[pltpu.VMEM((LOC_ROWS, D), BF16), pltpu.SemaphoreType.DMA(())],
        compiler_params=_params(1),
        name="moe_combine_ln",
    )(meta, x1, colform, yg, sg, su, sd, g, b)


def _moe_plan(cnt_out, T):
    nst = T // TOK_TILE
    blk_u = FFN_BLK // ROW_PAD
    cnt = cnt_out.reshape(nst, 8, 128)[:, 0, :N_EXPERTS].astype(jnp.int32)
    rows = (cnt + ROW_PAD - 1) // ROW_PAD
    lstart = jnp.cumsum(rows, axis=1) - rows
    tot = rows.sum(axis=0)
    tot_pad = (tot + blk_u - 1) // blk_u * blk_u
    eend = jnp.cumsum(tot_pad)
    ebase = eend - tot_pad
    goff = ebase[None, :] + jnp.cumsum(rows, axis=0) - rows
    prev = jnp.concatenate([jnp.zeros((1, N_EXPERTS), jnp.int32), rows[:-1]], axis=0)
    meta = jnp.concatenate([lstart, rows, goff, prev], axis=1).reshape(nst, 1, 4 * N_EXPERTS)
    tail = jnp.stack([ebase + tot, tot_pad - tot])
    nblk = _moe_rows(T) // FFN_BLK
    nused = (eend[-1] // blk_u).reshape(1)
    blk_exp = jnp.minimum(jnp.searchsorted(eend, jnp.arange(nblk, dtype=jnp.int32) * blk_u, side='right'),
                          N_EXPERTS - 1).astype(jnp.int32)
    return meta, tail, blk_exp, nused


def _moe_rows(T):
    worst = T * TOP_K + (T // TOK_TILE) * N_EXPERTS * (ROW_PAD - 1) + N_EXPERTS * (FFN_BLK - ROW_PAD)
    return -(-worst // FFN_BLK) * FFN_BLK


def _rope_tables(S):
    half = AT_DH // 2
    lane = jnp.arange(128)
    inv = ROPE_THETA ** (-(lane % half).astype(F32) / half)
    ang = jnp.arange(S, dtype=F32)[:, None] * inv[None, :]
    sign = jnp.where((lane % AT_DH) < half, -1.0, 1.0).astype(F32)
    return jnp.cos(ang), jnp.sin(ang) * sign[None, :]


def _pad_cols(a, width):
    return jnp.pad(a, ((0, 0), (0, width - a.shape[1])))


def kernel(x, w_in, ml_i_bias, ml_f_bias, ml_norm_w, ssm_conv_w, ssm_conv_b, ssm_dt_bias, ssm_a_log, ssm_d, ssm_norm_w, gla_gate_w2, gla_gate_b, gla_norm_w, w_out, ln1_g, ln1_b, router_w, router_bias, exp_w_gate, exp_w_up, exp_w_down, sh_w_gate, sh_w_up, sh_w_down, ln2_g, ln2_b):
    B, S, D = x.shape
    T = B * S
    depth = w_in.shape[0]
    assert D == 1024 and S % (AT_SPAN * max(DILATIONS)) == 0 and T % 1024 == 0
    cos, sin = _rope_tables(S)
    o_mi, o_sz, o_sdt, o_aq, o_gq, o_ga = 1024, 1032, 1800, 1804, 2572, 3340
    xf = x.reshape(T, D)
    for l in range(depth):
        w = w_in[l]
        wm = jnp.concatenate([w[:, 0:1024], w[:, o_sz:o_sdt], w[:, o_aq:o_gq], w[:, o_gq:o_ga]], axis=1).astype(BF16)
        wgt = _pad_cols(jnp.concatenate([w[:, o_mi:o_sz], w[:, o_sdt:o_aq], w[:, o_ga:o_ga + GLA_RANK]], axis=1),
                        128).astype(BF16)
        brow = _pad_cols(jnp.concatenate([ml_i_bias[l], ml_f_bias[l], ssm_dt_bias[l]])[None, :], 128).astype(F32)
        alog = jnp.pad(ssm_a_log[l].astype(F32), (G_DT, 128 - G_DT - HEADS))[None, :]
        w2 = jnp.pad(gla_gate_w2[l].astype(F32), ((G_GA, 128 - G_GA - GLA_RANK), (0, 0)))

        ml, ssm, at, gla, gates = _in_proj(xf, wm, wgt, cos, sin, S)
        y_ml = _mlstm(ml, gates, brow, ml_norm_w[l][None, :].astype(F32), B, S)
        y_ssm = _ssd(ssm, gates, brow, alog, ssm_conv_w[l].astype(F32), ssm_conv_b[l][None, :].astype(F32),
                     jnp.repeat(ssm_d[l].astype(F32), SSM_P)[None, :], ssm_norm_w[l][None, :].astype(F32), B, S)
        branches = [_attn(at, d, B, S) for d in DILATIONS]
        y_gla = _gla(gla, gates, w2, gla_gate_b[l][None, :].astype(F32), gla_norm_w[l][None, :].astype(F32), B, S)
        x1 = _out_proj(y_ml, y_ssm, branches[0][0], branches[1][0], branches[2][0],
                       branches[0][1], branches[1][1], branches[2][1], y_gla, xf,
                       w_out[l].astype(BF16), ln1_g[l][None, :].astype(F32), ln1_b[l][None, :].astype(F32))
        rowform, colform, cnt = _router(x1, router_w[l].T.astype(F32), router_bias[l][:, None].astype(F32))
        meta, tail, blk_exp, nused = _moe_plan(cnt, T)
        xg = _dispatch(meta, tail, x1, rowform, _moe_rows(T))
        yg = _ffn(blk_exp, nused, xg, exp_w_gate[l].astype(BF16), exp_w_up[l].astype(BF16),
                  exp_w_down[l].astype(BF16))
        xf = _combine(meta, x1, colform, yg, sh_w_gate[l].astype(BF16), sh_w_up[l].astype(BF16),
                      sh_w_down[l].astype(BF16), ln2_g[l][None, :].astype(F32), ln2_b[l][None, :].astype(F32))
    return xf.reshape(B, S, D)
```

```python
import functools

import jax
import jax.numpy as jnp
from jax import lax
from jax.experimental import pallas as pl
from jax.experimental.pallas import tpu as pltpu

F32 = jnp.float32
BF16 = jnp.bfloat16
HI = lax.Precision.HIGHEST
NT = (((1,), (1,)), ((), ()))
TN = (((0,), (0,)), ((), ()))

DEPTH = 4
HEADS = 4
ML_DH = 64
SSM_P = 64
SSM_N = 64
SSM_GROUPS = 2
SSM_CONV = 4
AT_DH = 64
DILATIONS = (1, 4, 16)
AT_SPAN = 128
ROPE_THETA = 10000.0
GLA_DK = 32
GLA_DV = 64
GLA_RANK = 16
GLA_TAU = 16.0
N_EXPERTS = 64
TOP_K = 8
N_GROUPS = 8
TOPK_GROUPS = 4
ROUTED_SCALE = 2.5
ALPHA = (2 * DEPTH) ** 0.25
LN_EPS = 1e-5
CHUNK = 64
TOK_TILE = 256
ROW_PAD = 16
FFN_BLK = 512
LOC_ROWS = -(-(TOK_TILE * TOP_K + N_EXPERTS * (ROW_PAD - 1)) // 512) * 512

G_MI, G_MF, G_DT, G_GA = 0, 4, 8, 12

VMEM_LIMIT = 48 * 1024 * 1024


def _log_sigmoid(x):
    return jnp.minimum(x, 0.0) - jnp.log(1.0 + jnp.exp(-jnp.abs(x)))


def _softplus(x):
    return jnp.maximum(x, 0.0) + jnp.log(1.0 + jnp.exp(-jnp.abs(x)))


def _silu(x):
    return x * jax.nn.sigmoid(x)


def _layer_norm(z, g, b):
    mu = jnp.mean(z, axis=-1, keepdims=True)
    zc = z - mu
    var = jnp.mean(zc * zc, axis=-1, keepdims=True)
    return zc * lax.rsqrt(var + LN_EPS) * g + b


def _params(n_axes):
    return pltpu.CompilerParams(dimension_semantics=("arbitrary",) * n_axes, vmem_limit_bytes=VMEM_LIMIT)


def _in_proj_kernel(x_ref, wm_ref, wg_ref, cos_ref, sin_ref, ml_ref, ssm_ref, at_ref, gla_ref, g_ref):
    xb = x_ref[...].astype(BF16)

    def mm(lo):
        return jnp.dot(xb, wm_ref[:, lo:lo + 256], preferred_element_type=F32)

    for c in range(4):
        a = mm(256 * c)
        if c == 1:
            a = a * ML_DH ** -0.5
        ml_ref[:, 256 * c:256 * (c + 1)] = a.astype(BF16)
    for c in range(3):
        ssm_ref[:, 256 * c:256 * (c + 1)] = mm(1024 + 256 * c).astype(BF16)

    cos = cos_ref[...]
    sin = sin_ref[...]
    lane = lax.broadcasted_iota(jnp.int32, cos.shape, 1)
    first_half = (lane % AT_DH) < AT_DH // 2

    def rope(a):
        rot = jnp.where(first_half, pltpu.roll(a, 128 - AT_DH // 2, 1), pltpu.roll(a, AT_DH // 2, 1))
        return a * cos + rot * sin

    for c in range(3):
        a = mm(1792 + 256 * c)
        if c < 2:
            scale = AT_DH ** -0.5 if c == 0 else 1.0
            for hh in range(2):
                at_ref[:, 256 * c + 128 * hh:256 * c + 128 * (hh + 1)] = (
                    rope(a[:, 128 * hh:128 * (hh + 1)]) * scale).astype(BF16)
        else:
            at_ref[:, 512:768] = a.astype(BF16)
    for c in range(3):
        gla_ref[:, 256 * c:256 * (c + 1)] = mm(2560 + 256 * c).astype(BF16)
    g_ref[...] = jnp.dot(xb, wg_ref[...], preferred_element_type=F32)


def _in_proj(x, wm, wg, cos, sin, S, tm=512):
    T, D = x.shape
    nS = S // tm
    row = lambda i: (i, 0)
    full = lambda i: (0, 0)
    return pl.pallas_call(
        _in_proj_kernel,
        grid=(T // tm,),
        in_specs=[pl.BlockSpec((tm, D), row), pl.BlockSpec(wm.shape, full), pl.BlockSpec(wg.shape, full),
                  pl.BlockSpec((tm, 128), lambda i: (i % nS, 0)), pl.BlockSpec((tm, 128), lambda i: (i % nS, 0))],
        out_specs=[pl.BlockSpec((tm, 1024), row), pl.BlockSpec((tm, 768), row), pl.BlockSpec((tm, 768), row),
                   pl.BlockSpec((tm, 768), row), pl.BlockSpec((tm, 128), row)],
        out_shape=[jax.ShapeDtypeStruct((T, 1024), BF16), jax.ShapeDtypeStruct((T, 768), BF16),
                   jax.ShapeDtypeStruct((T, 768), BF16), jax.ShapeDtypeStruct((T, 768), BF16),
                   jax.ShapeDtypeStruct((T, 128), F32)],
        compiler_params=_params(1),
        name="in_proj",
    )(x, wm, wg, cos, sin)


def _tri_consts(L):
    ri = lax.broadcasted_iota(jnp.int32, (L, L), 0)
    ci = lax.broadcasted_iota(jnp.int32, (L, L), 1)
    causal = ri >= ci
    return causal, causal.astype(F32), (ri <= ci).astype(F32)


def _mlstm_kernel(ml_ref, g_ref, brow_ref, nw_ref, y_ref, c_ref, m_ref, *, CH):
    @pl.when(pl.program_id(1) == 0)
    def _init():
        c_ref[...] = jnp.zeros_like(c_ref)
        m_ref[...] = jnp.full_like(m_ref, -jnp.inf)

    L = CHUNK
    causal, tril, triu = _tri_consts(L)
    lane = lax.broadcasted_iota(jnp.int32, (128, 128), 1)
    is_f = (lane >= G_MF) & (lane < G_MF + HEADS)
    ones_v = jnp.ones((L, ML_DH), BF16)
    nw = nw_ref[...]

    def pair(pi, carry):
        r0 = pl.multiple_of(pi * 128, 128)
        g2 = g_ref[pl.ds(r0, 128), :] + brow_ref[...]
        vals = jnp.where(is_f, _log_sigmoid(g2), g2)
        vals_t = vals.T
        for half in range(2):
            rr = pl.multiple_of(r0 + half * L, L)
            vc = vals[half * L:(half + 1) * L, :]
            vtc = vals_t[0:8, half * L:(half + 1) * L]
            b_all = jnp.dot(tril, vc, precision=HI, preferred_element_type=F32)
            bt_all = jnp.dot(vtc, triu, precision=HI, preferred_element_type=F32)
            for h in range(HEADS):
                q = ml_ref[pl.ds(rr, L), h * 64:(h + 1) * 64]
                k = ml_ref[pl.ds(rr, L), 256 + h * 64:256 + (h + 1) * 64]
                v = ml_ref[pl.ds(rr, L), 512 + h * 64:512 + (h + 1) * 64]
                og = ml_ref[pl.ds(rr, L), 768 + h * 64:768 + (h + 1) * 64].astype(F32)
                b_col = b_all[:, G_MF + h:G_MF + h + 1]
                li_col = vc[:, G_MI + h:G_MI + h + 1]
                b_row = bt_all[G_MF + h:G_MF + h + 1, :]
                li_row = vtc[G_MI + h:G_MI + h + 1, :]
                m_prev = m_ref[h][0:1, 0:1]
                cst = c_ref[h]

                log_d = jnp.where(causal, b_col - b_row + li_row, -jnp.inf)
                log_inter = b_col + m_prev
                m_t = jnp.maximum(log_inter, jnp.max(log_d, axis=1, keepdims=True))
                w_inter = jnp.exp(log_inter - m_t)
                s = lax.dot_general(q, k, NT, preferred_element_type=F32) * jnp.exp(log_d - m_t)
                qc = jnp.dot(q, cst.astype(BF16), preferred_element_type=F32)
                num = jnp.dot(s.astype(BF16), v, preferred_element_type=F32) + w_inter * qc[:, 0:64]
                den = jnp.sum(s, axis=1, keepdims=True) + w_inter * qc[:, 64:65]
                hh = num / jnp.maximum(jnp.abs(den), jnp.exp(-m_t))

                b_last = b_col[L - 1:L, :]
                log_s = b_last - b_col + li_col
                m_new = jnp.maximum(b_last + m_prev, jnp.max(log_s, axis=0, keepdims=True))
                w_c = jnp.exp(b_last + m_prev - m_new)
                kw = (k.astype(F32) * jnp.exp(log_s - m_new)).astype(BF16)
                c_ref[h, :, 0:64] = w_c * cst[:, 0:64] + lax.dot_general(kw, v, TN, preferred_element_type=F32)
                c_ref[h, :, 64:128] = w_c * cst[:, 64:128] + lax.dot_general(kw, ones_v, TN,
                                                                              preferred_element_type=F32)
                m_ref[h] = jnp.broadcast_to(m_new, (8, 128))

                hc = hh - jnp.mean(hh, axis=1, keepdims=True)
                hn = hc * lax.rsqrt(jnp.mean(hc * hc, axis=1, keepdims=True) + LN_EPS) * nw[:, h * 64:(h + 1) * 64]
                y_ref[pl.ds(rr, L), h * 64:(h + 1) * 64] = (jax.nn.sigmoid(og) * hn).astype(BF16)
        return carry

    lax.fori_loop(0, CH // 128, pair, 0)


def _mlstm(ml, gates, brow, nw, B, S, CH=512):
    T = ml.shape[0]
    nS = S // CH
    row = lambda b, j: (b * nS + j, 0)
    full = lambda b, j: (0, 0)
    return pl.pallas_call(
        functools.partial(_mlstm_kernel, CH=CH),
        grid=(B, nS),
        in_specs=[pl.BlockSpec((CH, 1024), row), pl.BlockSpec((CH, 128), row),
                  pl.BlockSpec((1, 128), full), pl.BlockSpec((1, 256), full)],
        out_specs=pl.BlockSpec((CH, 256), row),
        out_shape=jax.ShapeDtypeStruct((T, 256), BF16),
        scratch_shapes=[pltpu.VMEM((HEADS, ML_DH, 128), F32), pltpu.VMEM((HEADS, 8, 128), F32)],
        compiler_params=_params(2),
        name="mlstm",
    )(ml, gates, brow, nw)


def _ssd_kernel(ssm_ref, g_ref, brow_ref, alog_ref, cw_ref, cb_ref, d_ref, nw_ref, y_ref,
                xbuf_ref, xact_ref, st_ref, *, CH):
    @pl.when(pl.program_id(1) == 0)
    def _init():
        xbuf_ref[0:8, :] = jnp.zeros((8, 512), F32)
        st_ref[...] = jnp.zeros_like(st_ref)

    xbuf_ref[8:CH + 8, :] = ssm_ref[:, 256:768].astype(F32)
    conv = cb_ref[...] + cw_ref[0:1, :] * xbuf_ref[5:5 + CH, :]
    for j in range(1, SSM_CONV):
        conv = conv + cw_ref[j:j + 1, :] * xbuf_ref[5 + j:5 + j + CH, :]
    xact_ref[...] = _silu(conv)
    xbuf_ref[0:8, :] = xbuf_ref[CH:CH + 8, :]

    L = CHUNK
    causal, tril, triu = _tri_consts(L)
    lane = lax.broadcasted_iota(jnp.int32, (1, 128), 1)
    a_row = jnp.where((lane >= G_DT) & (lane < G_DT + HEADS), -jnp.exp(alog_ref[...]), 0.0)
    dskip = d_ref[...]
    nw = nw_ref[...]

    def pair(pi, carry):
        r0 = pl.multiple_of(pi * 128, 128)
        dt2 = _softplus(g_ref[pl.ds(r0, 128), :] + brow_ref[...])
        a2 = dt2 * a_row
        a2_t = a2.T
        for half in range(2):
            rr = pl.multiple_of(r0 + half * L, L)
            acs_all = jnp.dot(tril, a2[half * L:(half + 1) * L, :], precision=HI, preferred_element_type=F32)
            acs_t = jnp.dot(a2_t[G_DT:G_DT + 8, half * L:(half + 1) * L], triu, precision=HI,
                            preferred_element_type=F32)
            cb = []
            bmat = []
            cmat = []
            for g in range(SSM_GROUPS):
                bm = xact_ref[pl.ds(rr, L), 256 + g * 64:256 + (g + 1) * 64]
                cm = xact_ref[pl.ds(rr, L), 384 + g * 64:384 + (g + 1) * 64].astype(BF16)
                bmat.append(bm)
                cmat.append(cm)
                cb.append(lax.dot_general(cm, bm.astype(BF16), NT, preferred_element_type=F32))
            gated = []
            ssq = jnp.zeros((L, 1), F32)
            for h in range(HEADS):
                g = h // (HEADS // SSM_GROUPS)
                acs_col = acs_all[:, G_DT + h:G_DT + h + 1]
                acs_row = acs_t[h:h + 1, :]
                dt_col = dt2[half * L:(half + 1) * L, G_DT + h:G_DT + h + 1]
                xh = xact_ref[pl.ds(rr, L), h * 64:(h + 1) * 64]
                xdt = (xh * dt_col).astype(BF16)
                st = st_ref[h]
                mmat = cb[g] * jnp.exp(jnp.where(causal, acs_col - acs_row, -jnp.inf))
                y = jnp.dot(mmat.astype(BF16), xdt, preferred_element_type=F32)
                y = y + jnp.dot(cmat[g], st.astype(BF16), preferred_element_type=F32) * jnp.exp(acs_col)
                y = y + xh * dskip[:, h * 64:(h + 1) * 64]
                acs_last = acs_col[L - 1:L, :]
                bdec = (bmat[g] * jnp.exp(acs_last - acs_col)).astype(BF16)
                st_ref[h] = jnp.exp(acs_last) * st + lax.dot_general(bdec, xdt, TN, preferred_element_type=F32)
                z = ssm_ref[pl.ds(rr, L), h * 64:(h + 1) * 64].astype(F32)
                yg = y * _silu(z)
                ssq = ssq + jnp.sum(yg * yg, axis=1, keepdims=True)
                gated.append(yg)
            scale = lax.rsqrt(ssq / (HEADS * SSM_P) + LN_EPS)
            for h in range(HEADS):
                y_ref[pl.ds(rr, L), h * 64:(h + 1) * 64] = (gated[h] * scale * nw[:, h * 64:(h + 1) * 64]).astype(BF16)
        return carry

    lax.fori_loop(0, CH // 128, pair, 0)


def _ssd(ssm, gates, brow, alog, cw, cb, dskip, nw, B, S, CH=512):
    T = ssm.shape[0]
    nS = S // CH
    row = lambda b, j: (b * nS + j, 0)
    full = lambda b, j: (0, 0)
    return pl.pallas_call(
        functools.partial(_ssd_kernel, CH=CH),
        grid=(B, nS),
        in_specs=[pl.BlockSpec((CH, 768), row), pl.BlockSpec((CH, 128), row), pl.BlockSpec((1, 128), full),
                  pl.BlockSpec((1, 128), full), pl.BlockSpec((SSM_CONV, 512), full), pl.BlockSpec((1, 512), full),
                  pl.BlockSpec((1, 256), full), pl.BlockSpec((1, 256), full)],
        out_specs=pl.BlockSpec((CH, 256), row),
        out_shape=jax.ShapeDtypeStruct((T, 256), BF16),
        scratch_shapes=[pltpu.VMEM((CH + 8, 512), F32), pltpu.VMEM((CH, 512), F32),
                        pltpu.VMEM((HEADS, SSM_N, SSM_P), F32)],
        compiler_params=_params(2),
        name="ssd",
    )(ssm, gates, brow, alog, cw, cb, dskip, nw)


def _attn_kernel(at_ref, o_ref, lse_ref, *, N):
    W = AT_SPAN
    ri = lax.broadcasted_iota(jnp.int32, (W, W), 0)
    ci = lax.broadcasted_iota(jnp.int32, (W, W), 1)
    cur_ok = ri >= ci
    prev_ok = ci >= ri

    def blk(n, carry):
        r0 = pl.multiple_of(n * W, W)
        rp = pl.multiple_of(jnp.maximum(n - 1, 0) * W, W)
        pmask = prev_ok & (n > 0)
        for h in range(HEADS):
            q = at_ref[pl.ds(r0, W), h * 64:(h + 1) * 64]
            kc = at_ref[pl.ds(r0, W), 256 + h * 64:256 + (h + 1) * 64]
            kp = at_ref[pl.ds(rp, W), 256 + h * 64:256 + (h + 1) * 64]
            vc = at_ref[pl.ds(r0, W), 512 + h * 64:512 + (h + 1) * 64]
            vp = at_ref[pl.ds(rp, W), 512 + h * 64:512 + (h + 1) * 64]
            sc = jnp.where(cur_ok, lax.dot_general(q, kc, NT, preferred_element_type=F32), -jnp.inf)
            sp = jnp.where(pmask, lax.dot_general(q, kp, NT, preferred_element_type=F32), -jnp.inf)
            m = jnp.maximum(jnp.max(sc, axis=1, keepdims=True), jnp.max(sp, axis=1, keepdims=True))
            pc = jnp.exp(sc - m)
            pp = jnp.exp(sp - m)
            den = jnp.sum(pc, axis=1, keepdims=True) + jnp.sum(pp, axis=1, keepdims=True)
            acc = jnp.dot(pc.astype(BF16), vc, preferred_element_type=F32)
            acc = acc + jnp.dot(pp.astype(BF16), vp, preferred_element_type=F32)
            o_ref[pl.ds(r0, W), h * 64:(h + 1) * 64] = (acc / den).astype(BF16)
            lse_ref[pl.ds(r0, W), h * 64:(h + 1) * 64] = jnp.broadcast_to(m + jnp.log(den), (W, 64))
        return carry

    lax.fori_loop(0, N // W, blk, 0)


def _attn(at, d, B, S):
    N = S // d
    atv = at.reshape(B, N, d * 768)
    o, lse = pl.pallas_call(
        functools.partial(_attn_kernel, N=N),
        grid=(B, d),
        in_specs=[pl.BlockSpec((None, N, 768), lambda b, r: (b, 0, r))],
        out_specs=[pl.BlockSpec((None, N, 256), lambda b, r: (b, 0, r)),
                   pl.BlockSpec((None, N, 256), lambda b, r: (b, 0, r))],
        out_shape=[jax.ShapeDtypeStruct((B, N, d * 256), BF16), jax.ShapeDtypeStruct((B, N, d * 256), F32)],
        compiler_params=_params(2),
        name=f"attn_d{d}",
    )(atv)
    return o.reshape(B * S, 256), lse.reshape(B * S, 256)


def _gla_kernel(gla_ref, g_ref, w2_ref, b2_ref, nw_ref, y_ref, st_ref, *, CH):
    @pl.when(pl.program_id(1) == 0)
    def _init():
        st_ref[...] = jnp.zeros_like(st_ref)

    L = CHUNK
    SB = 16
    ri = lax.broadcasted_iota(jnp.int32, (L, L), 0)
    ci = lax.broadcasted_iota(jnp.int32, (L, L), 1)
    tril_blk = ((ri >= ci) & (ri // SB == ci // SB)).astype(F32)
    tl = lax.broadcasted_iota(jnp.int32, (L, 128), 0) % SB
    er = lax.broadcasted_iota(jnp.int32, (128, 256), 0) // GLA_DK
    ec = lax.broadcasted_iota(jnp.int32, (128, 256), 1) // GLA_DV
    head_expand = (er == ec).astype(BF16)
    sr = lax.broadcasted_iota(jnp.int32, (256, 128), 0) // GLA_DV
    sc = lax.broadcasted_iota(jnp.int32, (256, 128), 1) // GLA_DK
    st_mask = (sr == sc).astype(F32)
    nw = nw_ref[...]

    def chunk(ci_, carry):
        rr = pl.multiple_of(ci_ * L, L)
        lg = _log_sigmoid(jnp.dot(g_ref[pl.ds(rr, L), :], w2_ref[...], preferred_element_type=F32)
                          + b2_ref[...]) / GLA_TAU
        c = jnp.dot(tril_blk, lg, precision=HI, preferred_element_type=F32)
        q = gla_ref[pl.ds(rr, L), 0:128].astype(F32) * GLA_DK ** -0.5
        k = gla_ref[pl.ds(rr, L), 128:256].astype(F32)
        vb = gla_ref[pl.ds(rr, L), 256:512]
        v = vb.astype(F32)
        c4 = c.reshape(L // SB, SB, 128)
        k4 = k.reshape(L // SB, SB, 128)
        v4 = v.reshape(L // SB, SB, 256)

        def bcast(x4, j, width):
            return jnp.broadcast_to(x4[:, j:j + 1, :], (L // SB, SB, width)).reshape(L, width)

        o = jnp.zeros((L, 256), F32)
        for j in range(SB):
            dec = jnp.exp(jnp.minimum(c - bcast(c4, j, 128), 0.0))
            p = jnp.where(tl >= j, q * dec * bcast(k4, j, 128), 0.0)
            a = jnp.dot(p.astype(BF16), head_expand, preferred_element_type=F32)
            o = o + a * bcast(v4, j, 256)

        outs = []
        for i in range(L // SB):
            cblk = c[i * SB:(i + 1) * SB, :]
            clast = cblk[SB - 1:SB, :]
            st = st_ref[...]
            qe = (q[i * SB:(i + 1) * SB, :] * jnp.exp(cblk)).astype(BF16)
            outs.append(o[i * SB:(i + 1) * SB, :] + lax.dot_general(qe, st.astype(BF16), NT,
                                                                    preferred_element_type=F32))
            ke = (k[i * SB:(i + 1) * SB, :] * jnp.exp(clast - cblk)).astype(BF16)
            upd = lax.dot_general(vb[i * SB:(i + 1) * SB, :], ke, TN, preferred_element_type=F32)
            st_ref[...] = st * jnp.exp(clast) + st_mask * upd
        for i in range(L // SB):
            rg = gla_ref[pl.ds(pl.multiple_of(rr + i * SB, SB), SB), 512:768].astype(F32)
            for h in range(HEADS):
                oh = outs[i][:, h * 64:(h + 1) * 64]
                rms = lax.rsqrt(jnp.mean(oh * oh, axis=1, keepdims=True) + LN_EPS)
                y_ref[pl.ds(pl.multiple_of(rr + i * SB, SB), SB), h * 64:(h + 1) * 64] = (
                    oh * rms * nw[:, h * 64:(h + 1) * 64] * _silu(rg[:, h * 64:(h + 1) * 64])).astype(BF16)
        return carry

    lax.fori_loop(0, CH // L, chunk, 0)


def _gla(gla, gates, w2, b2, nw, B, S, CH=512):
    T = gla.shape[0]
    nS = S // CH
    row = lambda b, j: (b * nS + j, 0)
    full = lambda b, j: (0, 0)
    return pl.pallas_call(
        functools.partial(_gla_kernel, CH=CH),
        grid=(B, nS),
        in_specs=[pl.BlockSpec((CH, 768), row), pl.BlockSpec((CH, 128), row), pl.BlockSpec((128, 128), full),
                  pl.BlockSpec((1, 128), full), pl.BlockSpec((1, 256), full)],
        out_specs=pl.BlockSpec((CH, 256), row),
        out_shape=jax.ShapeDtypeStruct((T, 256), BF16),
        scratch_shapes=[pltpu.VMEM((HEADS * GLA_DV, HEADS * GLA_DK), F32)],
        compiler_params=_params(2),
        name="gla",
    )(gla, gates, w2, b2, nw)


def _out_proj_kernel(yml_ref, yssm_ref, o1_ref, o2_ref, o3_ref, l1_ref, l2_ref, l3_ref, ygla_ref, x_ref,
                     wo_ref, g_ref, b_ref, x1_ref):
    l1, l2, l3 = l1_ref[...], l2_ref[...], l3_ref[...]
    mx = jnp.maximum(jnp.maximum(l1, l2), l3)
    e1, e2, e3 = jnp.exp(l1 - mx), jnp.exp(l2 - mx), jnp.exp(l3 - mx)
    yat = (e1 * o1_ref[...].astype(F32) + e2 * o2_ref[...].astype(F32) + e3 * o3_ref[...].astype(F32)) / (e1 + e2 + e3)
    acc = jnp.dot(yml_ref[...], wo_ref[0:256, :], preferred_element_type=F32)
    acc = acc + jnp.dot(yssm_ref[...], wo_ref[256:512, :], preferred_element_type=F32)
    acc = acc + jnp.dot(yat.astype(BF16), wo_ref[512:768, :], preferred_element_type=F32)
    acc = acc + jnp.dot(ygla_ref[...], wo_ref[768:1024, :], preferred_element_type=F32)
    x1_ref[...] = _layer_norm(ALPHA * x_ref[...] + acc, g_ref[...], b_ref[...])


def _out_proj(yml, yssm, o1, o2, o3, l1, l2, l3, ygla, x, wo, g, b, tm=512):
    T, D = x.shape
    row = lambda i: (i, 0)
    full = lambda i: (0, 0)
    small = pl.BlockSpec((tm, 256), row)
    return pl.pallas_call(
        _out_proj_kernel,
        grid=(T // tm,),
        in_specs=[small] * 9 + [pl.BlockSpec((tm, D), row), pl.BlockSpec(wo.shape, full),
                                pl.BlockSpec((1, D), full), pl.BlockSpec((1, D), full)],
        out_specs=pl.BlockSpec((tm, D), row),
        out_shape=jax.ShapeDtypeStruct((T, D), F32),
        compiler_params=_params(1),
        name="out_proj_ln",
    )(yml, yssm, o1, o2, o3, l1, l2, l3, ygla, x, wo, g, b)


def _router_kernel(x_ref, wr_ref, rb_ref, row_ref, col_ref, cnt_ref, rt_ref):
    tm = x_ref.shape[0]
    gsz = N_EXPERTS // N_GROUPS
    logits = lax.dot_general(wr_ref[...], x_ref[...], NT, precision=HI, preferred_element_type=F32)
    scores = jax.nn.sigmoid(logits)
    sel = scores + rb_ref[...]
    eidx = lax.broadcasted_iota(jnp.int32, (gsz, tm), 0)
    big = jnp.int32(1 << 20)
    neg = -jnp.inf

    sel_g = [sel[g * gsz:(g + 1) * gsz, :] for g in range(N_GROUPS)]
    idx_g = [eidx + g * gsz for g in range(N_GROUPS)]
    gscore = []
    for g in range(N_GROUPS):
        v = sel_g[g]
        m1 = jnp.max(v, axis=0, keepdims=True)
        i1 = jnp.min(jnp.where(v == m1, idx_g[g], big), axis=0, keepdims=True)
        m2 = jnp.max(jnp.where(idx_g[g] == i1, neg, v), axis=0, keepdims=True)
        gscore.append(m1 + m2)
    gkeep = [jnp.zeros((1, tm), jnp.bool_) for _ in range(N_GROUPS)]
    for _ in range(TOPK_GROUPS):
        m = functools.reduce(jnp.maximum, gscore)
        gi = functools.reduce(jnp.minimum, [jnp.where(gscore[g] == m, g, big) for g in range(N_GROUPS)])
        for g in range(N_GROUPS):
            hit = gi == g
            gkeep[g] = gkeep[g] | hit
            gscore[g] = jnp.where(hit, neg, gscore[g])
    cand = [jnp.where(gkeep[g], sel_g[g], neg) for g in range(N_GROUPS)]
    chosen = [jnp.zeros((gsz, tm), jnp.bool_) for _ in range(N_GROUPS)]
    picks = []
    for _ in range(TOP_K):
        m = functools.reduce(jnp.maximum, [jnp.max(c, axis=0, keepdims=True) for c in cand])
        ei = functools.reduce(jnp.minimum, [jnp.min(jnp.where(cand[g] == m, idx_g[g], big), axis=0, keepdims=True)
                                            for g in range(N_GROUPS)])
        picks.append(ei)
        for g in range(N_GROUPS):
            hit = idx_g[g] == ei
            chosen[g] = chosen[g] | hit
            cand[g] = jnp.where(hit, neg, cand[g])
    picked = [jnp.where(chosen[g], scores[g * gsz:(g + 1) * gsz, :], 0.0) for g in range(N_GROUPS)]
    tot = functools.reduce(jnp.add, [jnp.sum(p, axis=0, keepdims=True) for p in picked])
    gates = [p / tot * ROUTED_SCALE for p in picked]

    chosen_b = jnp.concatenate([c.astype(F32) for c in chosen] + [jnp.zeros((N_EXPERTS, tm), F32)], axis=0).astype(BF16)
    before = (lax.broadcasted_iota(jnp.int32, (tm, tm), 0) < lax.broadcasted_iota(jnp.int32, (tm, tm), 1)).astype(BF16)
    rank = jnp.dot(chosen_b[0:N_EXPERTS, :], before, preferred_element_type=F32)
    cnt = jnp.dot(chosen_b[0:N_EXPERTS, :], jnp.ones((tm, 128), BF16), preferred_element_type=F32)
    padded = jnp.floor((cnt + (ROW_PAD - 1)) * (1.0 / ROW_PAD)) * ROW_PAD
    lower = (lax.broadcasted_iota(jnp.int32, (N_EXPERTS, N_EXPERTS), 0)
             > lax.broadcasted_iota(jnp.int32, (N_EXPERTS, N_EXPERTS), 1)).astype(F32)
    gstart = jnp.dot(lower, padded, precision=HI, preferred_element_type=F32)
    posmat = gstart[:, 0:1] + rank
    rt_ref[...] = jnp.zeros_like(rt_ref)
    for k in range(TOP_K):
        pos_k = jnp.zeros((1, tm), F32)
        gate_k = jnp.zeros((1, tm), F32)
        for g in range(N_GROUPS):
            hit = idx_g[g] == picks[k]
            pos_k = pos_k + jnp.sum(jnp.where(hit, posmat[g * gsz:(g + 1) * gsz, :], 0.0), axis=0, keepdims=True)
            gate_k = gate_k + jnp.sum(jnp.where(hit, gates[g], 0.0), axis=0, keepdims=True)
        rt_ref[k:k + 1, :] = pos_k
        rt_ref[TOP_K + k:TOP_K + k + 1, :] = gate_k
    row_ref[...] = rt_ref[0:2 * TOP_K, :]
    col_ref[...] = rt_ref[...].T
    cnt_ref[...] = lax.dot_general(jnp.ones((8, tm), BF16), chosen_b, NT, preferred_element_type=F32)


def _router(x1, wr_t, rb):
    T, D = x1.shape
    tm = TOK_TILE
    nst = T // tm
    return pl.pallas_call(
        _router_kernel,
        grid=(nst,),
        in_specs=[pl.BlockSpec((tm, D), lambda i: (i, 0)), pl.BlockSpec(wr_t.shape, lambda i: (0, 0)),
                  pl.BlockSpec(rb.shape, lambda i: (0, 0))],
        out_specs=[pl.BlockSpec((2 * TOP_K, tm), lambda i: (0, i)), pl.BlockSpec((tm, 128), lambda i: (i, 0)),
                   pl.BlockSpec((8, 128), lambda i: (i, 0))],
        out_shape=[jax.ShapeDtypeStruct((2 * TOP_K, T), F32), jax.ShapeDtypeStruct((T, 128), F32),
                   jax.ShapeDtypeStruct((nst * 8, 128), F32)],
        scratch_shapes=[pltpu.VMEM((128, tm), F32)],
        compiler_params=_params(1),
        name="router",
    )(x1, wr_t, rb)


M_LSTART, M_ROWS, M_GOFF, M_PREV = 0, N_EXPERTS, 2 * N_EXPERTS, 3 * N_EXPERTS


def _ceil_div(x, n):
    assert n & (n - 1) == 0
    return lax.shift_right_logical(x + (n - 1), n.bit_length() - 1)


def _seg_copy(meta_ref, e, col, loc_ref, glob_ref, sem, to_global, placed=True):
    n = pl.multiple_of(meta_ref[0, 0, col + e] * ROW_PAD, ROW_PAD)
    ls = pl.multiple_of(meta_ref[0, 0, M_LSTART + e] * ROW_PAD, ROW_PAD) if placed else 0
    go = pl.multiple_of(meta_ref[0, 0, M_GOFF + e] * ROW_PAD, ROW_PAD) if placed else 0
    loc, glob = loc_ref.at[pl.ds(ls, n)], glob_ref.at[pl.ds(go, n)]
    return (pltpu.make_async_copy(loc, glob, sem) if to_global else pltpu.make_async_copy(glob, loc, sem)), n


def _dispatch_kernel(meta_ref, tail_ref, x_ref, row_ref, xg_ref, loc_ref, zero_ref, sems):
    s = pl.program_id(0)
    last = pl.num_programs(0) - 1
    slot = s % 2
    tm = x_ref.shape[0]

    @pl.when(s == 0)
    def _fill_tails():
        zero_ref[...] = jnp.zeros_like(zero_ref)

        def start(e, c):
            n = pl.multiple_of(tail_ref[1, e] * ROW_PAD, ROW_PAD)
            go = pl.multiple_of(tail_ref[0, e] * ROW_PAD, ROW_PAD)

            @pl.when(n > 0)
            def _():
                pltpu.make_async_copy(zero_ref.at[pl.ds(0, n)], xg_ref.at[pl.ds(go, n)], sems.at[2]).start()
            return c
        lax.fori_loop(0, N_EXPERTS, start, 0)

    RC = 256
    xb = x_ref[...].astype(BF16)
    pos = row_ref[0:TOP_K, :]
    hi = jnp.floor(pos * (1.0 / RC))
    lo = pos - hi * RC
    rows = lax.broadcasted_iota(jnp.int32, (RC, tm), 0).astype(F32)
    lo_hit = [jnp.where(lo[k:k + 1, :] == rows, 1.0, 0.0).astype(BF16) for k in range(TOP_K)]
    used = meta_ref[0, 0, M_LSTART + N_EXPERTS - 1] + meta_ref[0, 0, M_ROWS + N_EXPERTS - 1]

    def chunk(c, carry):
        cf = c.astype(F32)
        onehot = lo_hit[0] * jnp.where(hi[0:1, :] == cf, 1.0, 0.0).astype(BF16)
        for k in range(1, TOP_K):
            onehot = onehot + lo_hit[k] * jnp.where(hi[k:k + 1, :] == cf, 1.0, 0.0).astype(BF16)
        loc_ref[slot, pl.ds(pl.multiple_of(c * RC, RC), RC), :] = jnp.dot(
            onehot, xb, preferred_element_type=F32).astype(BF16)
        return carry
    lax.fori_loop(0, _ceil_div(used, RC // ROW_PAD), chunk, 0)

    def start(e, c):
        cp, n = _seg_copy(meta_ref, e, M_ROWS, loc_ref.at[slot], xg_ref, sems.at[slot], True)

        @pl.when(n > 0)
        def _():
            cp.start()
        return c
    lax.fori_loop(0, N_EXPERTS, start, 0)

    def wait_for(col, buf):
        def wait(e, c):
            cp, n = _seg_copy(meta_ref, e, col, loc_ref.at[buf], xg_ref, sems.at[buf], True, placed=False)

            @pl.when(n > 0)
            def _():
                cp.wait()
            return c
        lax.fori_loop(0, N_EXPERTS, wait, 0)

    @pl.when(s > 0)
    def _wait_prev():
        wait_for(M_PREV, 1 - slot)

    @pl.when(s == last)
    def _wait_own():
        wait_for(M_ROWS, slot)

    @pl.when(s == 0)
    def _wait_tails():
        def wait(e, c):
            n = pl.multiple_of(tail_ref[1, e] * ROW_PAD, ROW_PAD)
            go = pl.multiple_of(tail_ref[0, e] * ROW_PAD, ROW_PAD)

            @pl.when(n > 0)
            def _():
                pltpu.make_async_copy(zero_ref.at[pl.ds(0, n)], xg_ref.at[pl.ds(go, n)], sems.at[2]).wait()
            return c
        lax.fori_loop(0, N_EXPERTS, wait, 0)


def _dispatch(meta, tail, x1, rowform, p_rows):
    T, D = x1.shape
    tm = TOK_TILE
    return pl.pallas_call(
        _dispatch_kernel,
        grid=(T // tm,),
        in_specs=[pl.BlockSpec((1, 1, 4 * N_EXPERTS), lambda i: (i, 0, 0), memory_space=pltpu.SMEM),
                  pl.BlockSpec(memory_space=pltpu.SMEM),
                  pl.BlockSpec((tm, D), lambda i: (i, 0)), pl.BlockSpec((2 * TOP_K, tm), lambda i: (0, i))],
        out_specs=pl.BlockSpec(memory_space=pl.ANY),
        out_shape=jax.ShapeDtypeStruct((p_rows, D), BF16),
        scratch_shapes=[pltpu.VMEM((2, LOC_ROWS, D), BF16), pltpu.VMEM((FFN_BLK, D), BF16),
                        pltpu.SemaphoreType.DMA((3,))],
        compiler_params=_params(1),
        name="moe_dispatch",
    )(meta, tail, x1, rowform)


def _ffn_kernel(bexp_ref, nused_ref, x_ref, wg_ref, wu_ref, wd_ref, y_ref, wgb_ref, wub_ref, wdb_ref):
    i = pl.program_id(0)

    @pl.when(i < nused_ref[0])
    def _():
        @pl.when((i == 0) | (bexp_ref[i] != bexp_ref[jnp.maximum(i - 1, 0)]))
        def _new_expert():
            wgb_ref[...] = wg_ref[...].astype(BF16)
            wub_ref[...] = wu_ref[...].astype(BF16)
            wdb_ref[...] = wd_ref[...].astype(BF16)

        xb = x_ref[...]
        a = jnp.dot(xb, wgb_ref[...], preferred_element_type=F32)
        u = jnp.dot(xb, wub_ref[...], preferred_element_type=F32)
        y_ref[...] = jnp.dot((_silu(a) * u).astype(BF16), wdb_ref[...], preferred_element_type=F32).astype(BF16)


def _ffn(blk_exp, nused, xg, wg, wu, wd, layer):
    P, D = xg.shape
    F = wg.shape[3]
    blk = lambda i, be, nu: (jnp.maximum(jnp.minimum(i, nu[0] - 1), 0), 0)
    wsel = lambda i, be, nu: (layer, be[jnp.maximum(jnp.minimum(i, nu[0] - 1), 0)], 0, 0)
    return pl.pallas_call(
        _ffn_kernel,
        grid_spec=pltpu.PrefetchScalarGridSpec(
            num_scalar_prefetch=2,
            grid=(P // FFN_BLK,),
            in_specs=[pl.BlockSpec((FFN_BLK, D), blk), pl.BlockSpec((None, None, D, F), wsel),
                      pl.BlockSpec((None, None, D, F), wsel), pl.BlockSpec((None, None, F, D), wsel)],
            out_specs=pl.BlockSpec((FFN_BLK, D), blk),
            scratch_shapes=[pltpu.VMEM((D, F), BF16), pltpu.VMEM((D, F), BF16), pltpu.VMEM((F, D), BF16)]),
        out_shape=jax.ShapeDtypeStruct((P, D), BF16),
        compiler_params=_params(1),
        name="moe_ffn",
    )(blk_exp, nused, xg, wg, wu, wd)


def _combine_kernel(meta_ref, next_ref, x_ref, col_ref, yg_ref, sg_ref, su_ref, sd_ref, g_ref, b_ref, x2_ref,
                    loc_ref, sems):
    s = pl.program_id(0)
    last = pl.num_programs(0) - 1
    slot = s % 2
    tm = x_ref.shape[0]
    CW = 512
    cw_u = CW // ROW_PAD

    def used_rows(m_ref):
        return m_ref[0, 0, M_LSTART + N_EXPERTS - 1] + m_ref[0, 0, M_ROWS + N_EXPERTS - 1]

    def fetch(m_ref, buf):
        def start(e, c):
            cp, n = _seg_copy(m_ref, e, M_ROWS, loc_ref.at[buf], yg_ref, sems.at[buf], False)

            @pl.when(n > 0)
            def _():
                cp.start()
            return c
        lax.fori_loop(0, N_EXPERTS, start, 0)
        used = used_rows(m_ref)

        def clear(c, carry):
            loc_ref[buf, pl.ds(pl.multiple_of(c * ROW_PAD, ROW_PAD), ROW_PAD), :] = jnp.zeros(
                (ROW_PAD, loc_ref.shape[2]), BF16)
            return carry
        lax.fori_loop(used, _ceil_div(used, cw_u) * cw_u, clear, 0)

    @pl.when(s == 0)
    def _first():
        fetch(meta_ref, 0)

    @pl.when(s < last)
    def _prefetch():
        fetch(next_ref, 1 - slot)

    x1 = x_ref[...]
    xb = x1.astype(BF16)
    hid = _silu(jnp.dot(xb, sg_ref[...], preferred_element_type=F32)) * jnp.dot(xb, su_ref[...],
                                                                                preferred_element_type=F32)
    acc0 = jnp.dot(hid.astype(BF16), sd_ref[...], preferred_element_type=F32)

    pos = col_ref[:, 0:TOP_K]
    gate = col_ref[:, TOP_K:2 * TOP_K]
    hi = jnp.floor(pos * (1.0 / CW))
    lo = pos - hi * CW
    lanes = lax.broadcasted_iota(jnp.int32, (tm, CW), 1).astype(F32)
    lo_hit = [jnp.where(lo[:, k:k + 1] == lanes, 1.0, 0.0).astype(BF16) for k in range(TOP_K)]

    def wait(e, c):
        cp, n = _seg_copy(meta_ref, e, M_ROWS, loc_ref.at[slot], yg_ref, sems.at[slot], False)

        @pl.when(n > 0)
        def _():
            cp.wait()
        return c
    lax.fori_loop(0, N_EXPERTS, wait, 0)

    def chunk(c, acc):
        cf = c.astype(F32)
        wmat = lo_hit[0] * jnp.where(hi[:, 0:1] == cf, gate[:, 0:1], 0.0).astype(BF16)
        for k in range(1, TOP_K):
            wmat = wmat + lo_hit[k] * jnp.where(hi[:, k:k + 1] == cf, gate[:, k:k + 1], 0.0).astype(BF16)
        rows = loc_ref[slot, pl.ds(pl.multiple_of(c * CW, CW), CW), :]
        return acc + jnp.dot(wmat, rows, preferred_element_type=F32)
    acc = lax.fori_loop(0, _ceil_div(used_rows(meta_ref), cw_u), chunk, acc0)
    x2_ref[...] = _layer_norm(ALPHA * x1 + acc, g_ref[...], b_ref[...])


def _combine(meta, x1, colform, yg, sg, su, sd, g, b):
    T, D = x1.shape
    tm = TOK_TILE
    nst = T // tm
    full = lambda i: (0, 0)
    mspec = lambda f: pl.BlockSpec((1, 1, 4 * N_EXPERTS), f, memory_space=pltpu.SMEM)
    return pl.pallas_call(
        _combine_kernel,
        grid=(nst,),
        in_specs=[mspec(lambda i: (i, 0, 0)), mspec(lambda i: (jnp.minimum(i + 1, nst - 1), 0, 0)),
                  pl.BlockSpec((tm, D), lambda i: (i, 0)), pl.BlockSpec((tm, 128), lambda i: (i, 0)),
                  pl.BlockSpec(memory_space=pl.ANY),
                  pl.BlockSpec(sg.shape, full), pl.BlockSpec(su.shape, full), pl.BlockSpec(sd.shape, full),
                  pl.BlockSpec((1, D), full), pl.BlockSpec((1, D), full)],
        out_specs=pl.BlockSpec((tm, D), lambda i: (i, 0)),
        out_shape=jax.ShapeDtypeStruct((T, D), F32),
        scratch_shapes=[pltpu.VMEM((2, LOC_ROWS, D), BF16), pltpu.SemaphoreType.DMA((2,))],
        compiler_params=_params(1),
        name="moe_combine_ln",
    )(meta, meta, x1, colform, yg, sg, su, sd, g, b)


def _moe_plan(cnt_out, T):
    nst = T // TOK_TILE
    blk_u = FFN_BLK // ROW_PAD
    cnt = cnt_out.reshape(nst, 8, 128)[:, 0, :N_EXPERTS].astype(jnp.int32)
    rows = (cnt + ROW_PAD - 1) // ROW_PAD
    lstart = jnp.cumsum(rows, axis=1) - rows
    tot = rows.sum(axis=0)
    tot_pad = (tot + blk_u - 1) // blk_u * blk_u
    eend = jnp.cumsum(tot_pad)
    ebase = eend - tot_pad
    goff = ebase[None, :] + jnp.cumsum(rows, axis=0) - rows
    prev = jnp.concatenate([jnp.zeros((1, N_EXPERTS), jnp.int32), rows[:-1]], axis=0)
    meta = jnp.concatenate([lstart, rows, goff, prev], axis=1).reshape(nst, 1, 4 * N_EXPERTS)
    tail = jnp.stack([ebase + tot, tot_pad - tot])
    nblk = _moe_rows(T) // FFN_BLK
    nused = (eend[-1] // blk_u).reshape(1)
    first_row = jnp.arange(nblk, dtype=jnp.int32) * blk_u
    blk_exp = jnp.minimum(jnp.sum(first_row[:, None] >= eend[None, :], axis=1), N_EXPERTS - 1).astype(jnp.int32)
    return meta, tail, blk_exp, nused


def _moe_rows(T):
    worst = T * TOP_K + (T // TOK_TILE) * N_EXPERTS * (ROW_PAD - 1) + N_EXPERTS * (FFN_BLK - ROW_PAD)
    return -(-worst // FFN_BLK) * FFN_BLK


def _rope_tables(S):
    half = AT_DH // 2
    lane = jnp.arange(128)
    inv = ROPE_THETA ** (-(lane % half).astype(F32) / half)
    ang = jnp.arange(S, dtype=F32)[:, None] * inv[None, :]
    sign = jnp.where((lane % AT_DH) < half, -1.0, 1.0).astype(F32)
    return jnp.cos(ang), jnp.sin(ang) * sign[None, :]


def _pad_cols(a, width):
    return jnp.pad(a, ((0, 0), (0, width - a.shape[1])))


def kernel(x, w_in, ml_i_bias, ml_f_bias, ml_norm_w, ssm_conv_w, ssm_conv_b, ssm_dt_bias, ssm_a_log, ssm_d, ssm_norm_w, gla_gate_w2, gla_gate_b, gla_norm_w, w_out, ln1_g, ln1_b, router_w, router_bias, exp_w_gate, exp_w_up, exp_w_down, sh_w_gate, sh_w_up, sh_w_down, ln2_g, ln2_b):
    B, S, D = x.shape
    T = B * S
    depth = w_in.shape[0]
    assert D == 1024 and S % (AT_SPAN * max(DILATIONS)) == 0 and T % 1024 == 0
    cos, sin = _rope_tables(S)
    o_mi, o_sz, o_sdt, o_aq, o_gq, o_ga = 1024, 1032, 1800, 1804, 2572, 3340
    xf = x.reshape(T, D)
    for l in range(depth):
        w = w_in[l]
        wm = jnp.concatenate([w[:, 0:1024], w[:, o_sz:o_sdt], w[:, o_aq:o_gq], w[:, o_gq:o_ga]], axis=1).astype(BF16)
        wgt = _pad_cols(jnp.concatenate([w[:, o_mi:o_sz], w[:, o_sdt:o_aq], w[:, o_ga:o_ga + GLA_RANK]], axis=1),
                        128).astype(BF16)
        brow = _pad_cols(jnp.concatenate([ml_i_bias[l], ml_f_bias[l], ssm_dt_bias[l]])[None, :], 128).astype(F32)
        alog = jnp.pad(ssm_a_log[l].astype(F32), (G_DT, 128 - G_DT - HEADS))[None, :]
        w2 = jnp.pad(gla_gate_w2[l].astype(F32), ((G_GA, 128 - G_GA - GLA_RANK), (0, 0)))

        ml, ssm, at, gla, gates = _in_proj(xf, wm, wgt, cos, sin, S)
        y_ml = _mlstm(ml, gates, brow, ml_norm_w[l][None, :].astype(F32), B, S)
        y_ssm = _ssd(ssm, gates, brow, alog, ssm_conv_w[l].astype(F32), ssm_conv_b[l][None, :].astype(F32),
                     jnp.repeat(ssm_d[l].astype(F32), SSM_P)[None, :], ssm_norm_w[l][None, :].astype(F32), B, S)
        branches = [_attn(at, d, B, S) for d in DILATIONS]
        y_gla = _gla(gla, gates, w2, gla_gate_b[l][None, :].astype(F32), gla_norm_w[l][None, :].astype(F32), B, S)
        x1 = _out_proj(y_ml, y_ssm, branches[0][0], branches[1][0], branches[2][0],
                       branches[0][1], branches[1][1], branches[2][1], y_gla, xf,
                       w_out[l].astype(BF16), ln1_g[l][None, :].astype(F32), ln1_b[l][None, :].astype(F32))
        rowform, colform, cnt = _router(x1, router_w[l].T.astype(F32), router_bias[l][:, None].astype(F32))
        meta, tail, blk_exp, nused = _moe_plan(cnt, T)
        xg = _dispatch(meta, tail, x1, rowform, _moe_rows(T))
        yg = _ffn(blk_exp, nused, xg, exp_w_gate, exp_w_up, exp_w_down, l)
        xf = _combine(meta, x1, colform, yg, sh_w_gate[l].astype(BF16), sh_w_up[l].astype(BF16),
                      sh_w_down[l].astype(BF16), ln2_g[l][None, :].astype(F32), ln2_b[l][None, :].astype(F32))
    return xf.reshape(B, S, D)
```

```python
import functools

import jax
import jax.numpy as jnp
from jax import lax
from jax.experimental import pallas as pl
from jax.experimental.pallas import tpu as pltpu

F32 = jnp.float32
BF16 = jnp.bfloat16
HI = lax.Precision.HIGHEST
NT = (((1,), (1,)), ((), ()))
TN = (((0,), (0,)), ((), ()))

DEPTH = 4
HEADS = 4
ML_DH = 64
SSM_P = 64
SSM_N = 64
SSM_GROUPS = 2
SSM_CONV = 4
AT_DH = 64
DILATIONS = (1, 4, 16)
AT_SPAN = 128
ROPE_THETA = 10000.0
GLA_DK = 32
GLA_DV = 64
GLA_RANK = 16
GLA_TAU = 16.0
N_EXPERTS = 64
TOP_K = 8
N_GROUPS = 8
TOPK_GROUPS = 4
ROUTED_SCALE = 2.5
ALPHA = (2 * DEPTH) ** 0.25
LN_EPS = 1e-5
CHUNK = 64
TOK_TILE = 256
ROW_PAD = 16
FFN_BLK = 512
LOC_ROWS = -(-(TOK_TILE * TOP_K + N_EXPERTS * (ROW_PAD - 1)) // 512) * 512

G_MI, G_MF, G_DT, G_GA = 0, 4, 8, 12

VMEM_LIMIT = 48 * 1024 * 1024


def _log_sigmoid(x):
    return jnp.minimum(x, 0.0) - jnp.log(1.0 + jnp.exp(-jnp.abs(x)))


def _softplus(x):
    return jnp.maximum(x, 0.0) + jnp.log(1.0 + jnp.exp(-jnp.abs(x)))


def _silu(x):
    return x * jax.nn.sigmoid(x)


def _layer_norm(z, g, b):
    mu = jnp.mean(z, axis=-1, keepdims=True)
    zc = z - mu
    var = jnp.mean(zc * zc, axis=-1, keepdims=True)
    return zc * lax.rsqrt(var + LN_EPS) * g + b


def _params(n_axes):
    return pltpu.CompilerParams(dimension_semantics=("arbitrary",) * n_axes, vmem_limit_bytes=VMEM_LIMIT)


def _in_proj_kernel(x_ref, wm_ref, wg_ref, cos_ref, sin_ref, ml_ref, ssm_ref, at_ref, gla_ref, g_ref):
    xb = x_ref[...].astype(BF16)

    def mm(lo):
        return jnp.dot(xb, wm_ref[:, lo:lo + 256], preferred_element_type=F32)

    for c in range(4):
        a = mm(256 * c)
        if c == 1:
            a = a * ML_DH ** -0.5
        ml_ref[:, 256 * c:256 * (c + 1)] = a.astype(BF16)
    for c in range(3):
        ssm_ref[:, 256 * c:256 * (c + 1)] = mm(1024 + 256 * c).astype(BF16)

    cos = cos_ref[...]
    sin = sin_ref[...]
    lane = lax.broadcasted_iota(jnp.int32, cos.shape, 1)
    first_half = (lane % AT_DH) < AT_DH // 2

    def rope(a):
        rot = jnp.where(first_half, pltpu.roll(a, 128 - AT_DH // 2, 1), pltpu.roll(a, AT_DH // 2, 1))
        return a * cos + rot * sin

    for c in range(3):
        a = mm(1792 + 256 * c)
        if c < 2:
            scale = AT_DH ** -0.5 if c == 0 else 1.0
            for hh in range(2):
                at_ref[:, 256 * c + 128 * hh:256 * c + 128 * (hh + 1)] = (
                    rope(a[:, 128 * hh:128 * (hh + 1)]) * scale).astype(BF16)
        else:
            at_ref[:, 512:768] = a.astype(BF16)
    for c in range(3):
        gla_ref[:, 256 * c:256 * (c + 1)] = mm(2560 + 256 * c).astype(BF16)
    g_ref[...] = jnp.dot(xb, wg_ref[...], preferred_element_type=F32)


def _in_proj(x, wm, wg, cos, sin, S, tm=512):
    T, D = x.shape
    nS = S // tm
    row = lambda i: (i, 0)
    full = lambda i: (0, 0)
    return pl.pallas_call(
        _in_proj_kernel,
        grid=(T // tm,),
        in_specs=[pl.BlockSpec((tm, D), row), pl.BlockSpec(wm.shape, full), pl.BlockSpec(wg.shape, full),
                  pl.BlockSpec((tm, 128), lambda i: (i % nS, 0)), pl.BlockSpec((tm, 128), lambda i: (i % nS, 0))],
        out_specs=[pl.BlockSpec((tm, 1024), row), pl.BlockSpec((tm, 768), row), pl.BlockSpec((tm, 768), row),
                   pl.BlockSpec((tm, 768), row), pl.BlockSpec((tm, 128), row)],
        out_shape=[jax.ShapeDtypeStruct((T, 1024), BF16), jax.ShapeDtypeStruct((T, 768), BF16),
                   jax.ShapeDtypeStruct((T, 768), BF16), jax.ShapeDtypeStruct((T, 768), BF16),
                   jax.ShapeDtypeStruct((T, 128), F32)],
        compiler_params=_params(1),
        name="in_proj",
    )(x, wm, wg, cos, sin)


def _tri_consts(L):
    ri = lax.broadcasted_iota(jnp.int32, (L, L), 0)
    ci = lax.broadcasted_iota(jnp.int32, (L, L), 1)
    causal = ri >= ci
    return causal, causal.astype(F32), (ri <= ci).astype(F32)


def _mlstm_kernel(ml_ref, g_ref, brow_ref, nw_ref, y_ref, c_ref, m_ref, *, CH):
    @pl.when(pl.program_id(1) == 0)
    def _init():
        c_ref[...] = jnp.zeros_like(c_ref)
        m_ref[...] = jnp.full_like(m_ref, -jnp.inf)

    L = CHUNK
    causal, tril, triu = _tri_consts(L)
    lane = lax.broadcasted_iota(jnp.int32, (128, 128), 1)
    is_f = (lane >= G_MF) & (lane < G_MF + HEADS)
    ones_v = jnp.ones((L, ML_DH), BF16)
    nw = nw_ref[...]

    def pair(pi, carry):
        r0 = pl.multiple_of(pi * 128, 128)
        g2 = g_ref[pl.ds(r0, 128), :] + brow_ref[...]
        vals = jnp.where(is_f, _log_sigmoid(g2), g2)
        vals_t = vals.T
        for half in range(2):
            rr = pl.multiple_of(r0 + half * L, L)
            vc = vals[half * L:(half + 1) * L, :]
            vtc = vals_t[0:8, half * L:(half + 1) * L]
            b_all = jnp.dot(tril, vc, precision=HI, preferred_element_type=F32)
            bt_all = jnp.dot(vtc, triu, precision=HI, preferred_element_type=F32)
            for h in range(HEADS):
                q = ml_ref[pl.ds(rr, L), h * 64:(h + 1) * 64]
                k = ml_ref[pl.ds(rr, L), 256 + h * 64:256 + (h + 1) * 64]
                v = ml_ref[pl.ds(rr, L), 512 + h * 64:512 + (h + 1) * 64]
                og = ml_ref[pl.ds(rr, L), 768 + h * 64:768 + (h + 1) * 64].astype(F32)
                b_col = b_all[:, G_MF + h:G_MF + h + 1]
                li_col = vc[:, G_MI + h:G_MI + h + 1]
                b_row = bt_all[G_MF + h:G_MF + h + 1, :]
                li_row = vtc[G_MI + h:G_MI + h + 1, :]
                m_prev = m_ref[h][0:1, 0:1]
                cst = c_ref[h]

                log_d = jnp.where(causal, b_col - b_row + li_row, -jnp.inf)
                log_inter = b_col + m_prev
                m_t = jnp.maximum(log_inter, jnp.max(log_d, axis=1, keepdims=True))
                w_inter = jnp.exp(log_inter - m_t)
                s = lax.dot_general(q, k, NT, preferred_element_type=F32) * jnp.exp(log_d - m_t)
                qc = jnp.dot(q, cst.astype(BF16), preferred_element_type=F32)
                num = jnp.dot(s.astype(BF16), v, preferred_element_type=F32) + w_inter * qc[:, 0:64]
                den = jnp.sum(s, axis=1, keepdims=True) + w_inter * qc[:, 64:65]
                hh = num / jnp.maximum(jnp.abs(den), jnp.exp(-m_t))

                b_last = b_col[L - 1:L, :]
                log_s = b_last - b_col + li_col
                m_new = jnp.maximum(b_last + m_prev, jnp.max(log_s, axis=0, keepdims=True))
                w_c = jnp.exp(b_last + m_prev - m_new)
                kw = (k.astype(F32) * jnp.exp(log_s - m_new)).astype(BF16)
                c_ref[h, :, 0:64] = w_c * cst[:, 0:64] + lax.dot_general(kw, v, TN, preferred_element_type=F32)
                c_ref[h, :, 64:128] = w_c * cst[:, 64:128] + lax.dot_general(kw, ones_v, TN,
                                                                              preferred_element_type=F32)
                m_ref[h] = jnp.broadcast_to(m_new, (8, 128))

                hc = hh - jnp.mean(hh, axis=1, keepdims=True)
                hn = hc * lax.rsqrt(jnp.mean(hc * hc, axis=1, keepdims=True) + LN_EPS) * nw[:, h * 64:(h + 1) * 64]
                y_ref[pl.ds(rr, L), h * 64:(h + 1) * 64] = (jax.nn.sigmoid(og) * hn).astype(BF16)
        return carry

    lax.fori_loop(0, CH // 128, pair, 0)


def _mlstm(ml, gates, brow, nw, B, S, CH=512):
    T = ml.shape[0]
    nS = S // CH
    row = lambda b, j: (b * nS + j, 0)
    full = lambda b, j: (0, 0)
    return pl.pallas_call(
        functools.partial(_mlstm_kernel, CH=CH),
        grid=(B, nS),
        in_specs=[pl.BlockSpec((CH, 1024), row), pl.BlockSpec((CH, 128), row),
                  pl.BlockSpec((1, 128), full), pl.BlockSpec((1, 256), full)],
        out_specs=pl.BlockSpec((CH, 256), row),
        out_shape=jax.ShapeDtypeStruct((T, 256), BF16),
        scratch_shapes=[pltpu.VMEM((HEADS, ML_DH, 128), F32), pltpu.VMEM((HEADS, 8, 128), F32)],
        compiler_params=_params(2),
        name="mlstm",
    )(ml, gates, brow, nw)


def _ssd_kernel(ssm_ref, g_ref, brow_ref, alog_ref, cw_ref, cb_ref, d_ref, nw_ref, y_ref,
                xbuf_ref, xact_ref, st_ref, *, CH):
    @pl.when(pl.program_id(1) == 0)
    def _init():
        xbuf_ref[0:8, :] = jnp.zeros((8, 512), F32)
        st_ref[...] = jnp.zeros_like(st_ref)

    xbuf_ref[8:CH + 8, :] = ssm_ref[:, 256:768].astype(F32)
    conv = cb_ref[...] + cw_ref[0:1, :] * xbuf_ref[5:5 + CH, :]
    for j in range(1, SSM_CONV):
        conv = conv + cw_ref[j:j + 1, :] * xbuf_ref[5 + j:5 + j + CH, :]
    xact_ref[...] = _silu(conv)
    xbuf_ref[0:8, :] = xbuf_ref[CH:CH + 8, :]

    L = CHUNK
    causal, tril, triu = _tri_consts(L)
    lane = lax.broadcasted_iota(jnp.int32, (1, 128), 1)
    a_row = jnp.where((lane >= G_DT) & (lane < G_DT + HEADS), -jnp.exp(alog_ref[...]), 0.0)
    dskip = d_ref[...]
    nw = nw_ref[...]

    def pair(pi, carry):
        r0 = pl.multiple_of(pi * 128, 128)
        dt2 = _softplus(g_ref[pl.ds(r0, 128), :] + brow_ref[...])
        a2 = dt2 * a_row
        a2_t = a2.T
        for half in range(2):
            rr = pl.multiple_of(r0 + half * L, L)
            acs_all = jnp.dot(tril, a2[half * L:(half + 1) * L, :], precision=HI, preferred_element_type=F32)
            acs_t = jnp.dot(a2_t[G_DT:G_DT + 8, half * L:(half + 1) * L], triu, precision=HI,
                            preferred_element_type=F32)
            cb = []
            bmat = []
            cmat = []
            for g in range(SSM_GROUPS):
                bm = xact_ref[pl.ds(rr, L), 256 + g * 64:256 + (g + 1) * 64]
                cm = xact_ref[pl.ds(rr, L), 384 + g * 64:384 + (g + 1) * 64].astype(BF16)
                bmat.append(bm)
                cmat.append(cm)
                cb.append(lax.dot_general(cm, bm.astype(BF16), NT, preferred_element_type=F32))
            gated = []
            ssq = jnp.zeros((L, 1), F32)
            for h in range(HEADS):
                g = h // (HEADS // SSM_GROUPS)
                acs_col = acs_all[:, G_DT + h:G_DT + h + 1]
                acs_row = acs_t[h:h + 1, :]
                dt_col = dt2[half * L:(half + 1) * L, G_DT + h:G_DT + h + 1]
                xh = xact_ref[pl.ds(rr, L), h * 64:(h + 1) * 64]
                xdt = (xh * dt_col).astype(BF16)
                st = st_ref[h]
                mmat = cb[g] * jnp.exp(jnp.where(causal, acs_col - acs_row, -jnp.inf))
                y = jnp.dot(mmat.astype(BF16), xdt, preferred_element_type=F32)
                y = y + jnp.dot(cmat[g], st.astype(BF16), preferred_element_type=F32) * jnp.exp(acs_col)
                y = y + xh * dskip[:, h * 64:(h + 1) * 64]
                acs_last = acs_col[L - 1:L, :]
                bdec = (bmat[g] * jnp.exp(acs_last - acs_col)).astype(BF16)
                st_ref[h] = jnp.exp(acs_last) * st + lax.dot_general(bdec, xdt, TN, preferred_element_type=F32)
                z = ssm_ref[pl.ds(rr, L), h * 64:(h + 1) * 64].astype(F32)
                yg = y * _silu(z)
                ssq = ssq + jnp.sum(yg * yg, axis=1, keepdims=True)
                gated.append(yg)
            scale = lax.rsqrt(ssq / (HEADS * SSM_P) + LN_EPS)
            for h in range(HEADS):
                y_ref[pl.ds(rr, L), h * 64:(h + 1) * 64] = (gated[h] * scale * nw[:, h * 64:(h + 1) * 64]).astype(BF16)
        return carry

    lax.fori_loop(0, CH // 128, pair, 0)


def _ssd(ssm, gates, brow, alog, cw, cb, dskip, nw, B, S, CH=512):
    T = ssm.shape[0]
    nS = S // CH
    row = lambda b, j: (b * nS + j, 0)
    full = lambda b, j: (0, 0)
    return pl.pallas_call(
        functools.partial(_ssd_kernel, CH=CH),
        grid=(B, nS),
        in_specs=[pl.BlockSpec((CH, 768), row), pl.BlockSpec((CH, 128), row), pl.BlockSpec((1, 128), full),
                  pl.BlockSpec((1, 128), full), pl.BlockSpec((SSM_CONV, 512), full), pl.BlockSpec((1, 512), full),
                  pl.BlockSpec((1, 256), full), pl.BlockSpec((1, 256), full)],
        out_specs=pl.BlockSpec((CH, 256), row),
        out_shape=jax.ShapeDtypeStruct((T, 256), BF16),
        scratch_shapes=[pltpu.VMEM((CH + 8, 512), F32), pltpu.VMEM((CH, 512), F32),
                        pltpu.VMEM((HEADS, SSM_N, SSM_P), F32)],
        compiler_params=_params(2),
        name="ssd",
    )(ssm, gates, brow, alog, cw, cb, dskip, nw)


def _attn_kernel(at_ref, o_ref, lse_ref, *, N):
    W = AT_SPAN
    ri = lax.broadcasted_iota(jnp.int32, (W, W), 0)
    ci = lax.broadcasted_iota(jnp.int32, (W, W), 1)
    cur_ok = ri >= ci
    prev_ok = ci >= ri
    low_lanes = ci < AT_DH
    ones = jnp.ones((W, W), BF16)

    NB = 2 if (N // W) % 2 == 0 else 1

    def pair_attention(q2, kc, kp, vc, vp, pmask):
        outs, lses = [], []
        for first in (True, False):
            qh = jnp.where(low_lanes if first else jnp.logical_not(low_lanes), q2, jnp.zeros_like(q2))
            sc = jnp.where(cur_ok, lax.dot_general(qh, kc, NT, preferred_element_type=F32), -jnp.inf)
            sp = jnp.where(pmask, lax.dot_general(qh, kp, NT, preferred_element_type=F32), -jnp.inf)
            m = jnp.max(jnp.maximum(sc, sp), axis=1, keepdims=True)
            pc = jnp.exp(sc - m).astype(BF16)
            pp = jnp.exp(sp - m).astype(BF16)
            acc = jnp.dot(pc, vc, preferred_element_type=F32) + jnp.dot(pp, vp, preferred_element_type=F32)
            den = jnp.dot(pc, ones, preferred_element_type=F32) + jnp.dot(pp, ones, preferred_element_type=F32)
            outs.append(acc / den)
            lses.append(m + jnp.log(den[:, 0:1]))
        return jnp.where(low_lanes, outs[0], outs[1]).astype(BF16), jnp.where(low_lanes, lses[0], lses[1])

    def blk(i, carry):
        work = []
        for u in range(NB):
            n = i * NB + u
            r0 = pl.multiple_of(n * W, W)
            rp = pl.multiple_of(jnp.maximum(n - 1, 0) * W, W)
            for p in range(HEADS // 2):
                lanes = lambda base: slice(base + 128 * p, base + 128 * (p + 1))
                work.append((r0, p, prev_ok & (n > 0),
                             at_ref[pl.ds(r0, W), lanes(0)], at_ref[pl.ds(r0, W), lanes(256)],
                             at_ref[pl.ds(rp, W), lanes(256)], at_ref[pl.ds(r0, W), lanes(512)],
                             at_ref[pl.ds(rp, W), lanes(512)]))
        done = [(r0, p) + pair_attention(q2, kc, kp, vc, vp, pmask) for r0, p, pmask, q2, kc, kp, vc, vp in work]
        for r0, p, o2, lse2 in done:
            o_ref[pl.ds(r0, W), 128 * p:128 * (p + 1)] = o2
            lse_ref[pl.ds(r0, W), 128 * p:128 * (p + 1)] = lse2
        return carry

    assert (N // W) % NB == 0
    lax.fori_loop(0, N // (W * NB), blk, 0)


def _attn(at, d, B, S):
    N = S // d
    atv = at.reshape(B, N, d * 768)
    o, lse = pl.pallas_call(
        functools.partial(_attn_kernel, N=N),
        grid=(B, d),
        in_specs=[pl.BlockSpec((None, N, 768), lambda b, r: (b, 0, r))],
        out_specs=[pl.BlockSpec((None, N, 256), lambda b, r: (b, 0, r)),
                   pl.BlockSpec((None, N, 256), lambda b, r: (b, 0, r))],
        out_shape=[jax.ShapeDtypeStruct((B, N, d * 256), BF16), jax.ShapeDtypeStruct((B, N, d * 256), F32)],
        compiler_params=_params(2),
        name=f"attn_d{d}",
    )(atv)
    return o.reshape(B * S, 256), lse.reshape(B * S, 256)


def _gla_kernel(gla_ref, g_ref, w2_ref, b2_ref, nw_ref, y_ref, st_ref, *, CH):
    @pl.when(pl.program_id(1) == 0)
    def _init():
        st_ref[...] = jnp.zeros_like(st_ref)

    L = CHUNK
    SB = 16
    ri = lax.broadcasted_iota(jnp.int32, (L, L), 0)
    ci = lax.broadcasted_iota(jnp.int32, (L, L), 1)
    tril_blk = ((ri >= ci) & (ri // SB == ci // SB)).astype(F32)
    tl = lax.broadcasted_iota(jnp.int32, (L, 128), 0) % SB
    er = lax.broadcasted_iota(jnp.int32, (128, 256), 0) // GLA_DK
    ec = lax.broadcasted_iota(jnp.int32, (128, 256), 1) // GLA_DV
    head_expand = (er == ec).astype(BF16)
    sr = lax.broadcasted_iota(jnp.int32, (256, 128), 0) // GLA_DV
    sc = lax.broadcasted_iota(jnp.int32, (256, 128), 1) // GLA_DK
    st_mask = (sr == sc).astype(F32)
    nw = nw_ref[...]

    def chunk(ci_, carry):
        rr = pl.multiple_of(ci_ * L, L)
        lg = _log_sigmoid(jnp.dot(g_ref[pl.ds(rr, L), :], w2_ref[...], preferred_element_type=F32)
                          + b2_ref[...]) / GLA_TAU
        c = jnp.dot(tril_blk, lg, precision=HI, preferred_element_type=F32)
        q = gla_ref[pl.ds(rr, L), 0:128].astype(F32) * GLA_DK ** -0.5
        k = gla_ref[pl.ds(rr, L), 128:256].astype(F32)
        vb = gla_ref[pl.ds(rr, L), 256:512]
        v = vb.astype(F32)
        c4 = c.reshape(L // SB, SB, 128)
        k4 = k.reshape(L // SB, SB, 128)
        v4 = v.reshape(L // SB, SB, 256)

        def bcast(x4, j, width):
            return jnp.broadcast_to(x4[:, j:j + 1, :], (L // SB, SB, width)).reshape(L, width)

        o = jnp.zeros((L, 256), F32)
        for j in range(SB):
            dec = jnp.exp(jnp.minimum(c - bcast(c4, j, 128), 0.0))
            p = jnp.where(tl >= j, q * dec * bcast(k4, j, 128), 0.0)
            a = jnp.dot(p.astype(BF16), head_expand, preferred_element_type=F32)
            o = o + a * bcast(v4, j, 256)

        outs = []
        for i in range(L // SB):
            cblk = c[i * SB:(i + 1) * SB, :]
            clast = cblk[SB - 1:SB, :]
            st = st_ref[...]
            qe = (q[i * SB:(i + 1) * SB, :] * jnp.exp(cblk)).astype(BF16)
            outs.append(o[i * SB:(i + 1) * SB, :] + lax.dot_general(qe, st.astype(BF16), NT,
                                                                    preferred_element_type=F32))
            ke = (k[i * SB:(i + 1) * SB, :] * jnp.exp(clast - cblk)).astype(BF16)
            upd = lax.dot_general(vb[i * SB:(i + 1) * SB, :], ke, TN, preferred_element_type=F32)
            st_ref[...] = st * jnp.exp(clast) + st_mask * upd
        for i in range(L // SB):
            rg = gla_ref[pl.ds(pl.multiple_of(rr + i * SB, SB), SB), 512:768].astype(F32)
            for h in range(HEADS):
                oh = outs[i][:, h * 64:(h + 1) * 64]
                rms = lax.rsqrt(jnp.mean(oh * oh, axis=1, keepdims=True) + LN_EPS)
                y_ref[pl.ds(pl.multiple_of(rr + i * SB, SB), SB), h * 64:(h + 1) * 64] = (
                    oh * rms * nw[:, h * 64:(h + 1) * 64] * _silu(rg[:, h * 64:(h + 1) * 64])).astype(BF16)
        return carry

    lax.fori_loop(0, CH // L, chunk, 0, unroll=2)


def _gla(gla, gates, w2, b2, nw, B, S, CH=512):
    T = gla.shape[0]
    nS = S // CH
    row = lambda b, j: (b * nS + j, 0)
    full = lambda b, j: (0, 0)
    return pl.pallas_call(
        functools.partial(_gla_kernel, CH=CH),
        grid=(B, nS),
        in_specs=[pl.BlockSpec((CH, 768), row), pl.BlockSpec((CH, 128), row), pl.BlockSpec((128, 128), full),
                  pl.BlockSpec((1, 128), full), pl.BlockSpec((1, 256), full)],
        out_specs=pl.BlockSpec((CH, 256), row),
        out_shape=jax.ShapeDtypeStruct((T, 256), BF16),
        scratch_shapes=[pltpu.VMEM((HEADS * GLA_DV, HEADS * GLA_DK), F32)],
        compiler_params=_params(2),
        name="gla",
    )(gla, gates, w2, b2, nw)


def _out_proj_kernel(yml_ref, yssm_ref, o1_ref, o2_ref, o3_ref, l1_ref, l2_ref, l3_ref, ygla_ref, x_ref,
                     wo_ref, g_ref, b_ref, x1_ref):
    l1, l2, l3 = l1_ref[...], l2_ref[...], l3_ref[...]
    mx = jnp.maximum(jnp.maximum(l1, l2), l3)
    e1, e2, e3 = jnp.exp(l1 - mx), jnp.exp(l2 - mx), jnp.exp(l3 - mx)
    yat = (e1 * o1_ref[...].astype(F32) + e2 * o2_ref[...].astype(F32) + e3 * o3_ref[...].astype(F32)) / (e1 + e2 + e3)
    acc = jnp.dot(yml_ref[...], wo_ref[0:256, :], preferred_element_type=F32)
    acc = acc + jnp.dot(yssm_ref[...], wo_ref[256:512, :], preferred_element_type=F32)
    acc = acc + jnp.dot(yat.astype(BF16), wo_ref[512:768, :], preferred_element_type=F32)
    acc = acc + jnp.dot(ygla_ref[...], wo_ref[768:1024, :], preferred_element_type=F32)
    x1_ref[...] = _layer_norm(ALPHA * x_ref[...] + acc, g_ref[...], b_ref[...])


def _out_proj(yml, yssm, o1, o2, o3, l1, l2, l3, ygla, x, wo, g, b, tm=512):
    T, D = x.shape
    row = lambda i: (i, 0)
    full = lambda i: (0, 0)
    small = pl.BlockSpec((tm, 256), row)
    return pl.pallas_call(
        _out_proj_kernel,
        grid=(T // tm,),
        in_specs=[small] * 9 + [pl.BlockSpec((tm, D), row), pl.BlockSpec(wo.shape, full),
                                pl.BlockSpec((1, D), full), pl.BlockSpec((1, D), full)],
        out_specs=pl.BlockSpec((tm, D), row),
        out_shape=jax.ShapeDtypeStruct((T, D), F32),
        compiler_params=_params(1),
        name="out_proj_ln",
    )(yml, yssm, o1, o2, o3, l1, l2, l3, ygla, x, wo, g, b)


def _router_kernel(x_ref, wr_ref, rb_ref, row_ref, col_ref, cnt_ref, rt_ref):
    tm = x_ref.shape[0]
    gsz = N_EXPERTS // N_GROUPS
    logits = lax.dot_general(wr_ref[...], x_ref[...], NT, precision=HI, preferred_element_type=F32)
    scores = jax.nn.sigmoid(logits)
    sel = scores + rb_ref[...]
    eidx = lax.broadcasted_iota(jnp.int32, (gsz, tm), 0)
    big = jnp.int32(1 << 20)
    neg = -jnp.inf

    sel_g = [sel[g * gsz:(g + 1) * gsz, :] for g in range(N_GROUPS)]
    idx_g = [eidx + g * gsz for g in range(N_GROUPS)]
    gscore = []
    for g in range(N_GROUPS):
        v = sel_g[g]
        m1 = jnp.max(v, axis=0, keepdims=True)
        i1 = jnp.min(jnp.where(v == m1, idx_g[g], big), axis=0, keepdims=True)
        m2 = jnp.max(jnp.where(idx_g[g] == i1, neg, v), axis=0, keepdims=True)
        gscore.append(m1 + m2)
    gkeep = [jnp.zeros((1, tm), jnp.bool_) for _ in range(N_GROUPS)]
    for _ in range(TOPK_GROUPS):
        m = functools.reduce(jnp.maximum, gscore)
        gi = functools.reduce(jnp.minimum, [jnp.where(gscore[g] == m, g, big) for g in range(N_GROUPS)])
        for g in range(N_GROUPS):
            hit = gi == g
            gkeep[g] = gkeep[g] | hit
            gscore[g] = jnp.where(hit, neg, gscore[g])
    cand = [jnp.where(gkeep[g], sel_g[g], neg) for g in range(N_GROUPS)]
    chosen = [jnp.zeros((gsz, tm), jnp.bool_) for _ in range(N_GROUPS)]
    picks = []
    for _ in range(TOP_K):
        m = functools.reduce(jnp.maximum, [jnp.max(c, axis=0, keepdims=True) for c in cand])
        ei = functools.reduce(jnp.minimum, [jnp.min(jnp.where(cand[g] == m, idx_g[g], big), axis=0, keepdims=True)
                                            for g in range(N_GROUPS)])
        picks.append(ei)
        for g in range(N_GROUPS):
            hit = idx_g[g] == ei
            chosen[g] = chosen[g] | hit
            cand[g] = jnp.where(hit, neg, cand[g])
    picked = [jnp.where(chosen[g], scores[g * gsz:(g + 1) * gsz, :], 0.0) for g in range(N_GROUPS)]
    tot = functools.reduce(jnp.add, [jnp.sum(p, axis=0, keepdims=True) for p in picked])
    gates = [p / tot * ROUTED_SCALE for p in picked]

    chosen_b = jnp.concatenate([c.astype(F32) for c in chosen] + [jnp.zeros((N_EXPERTS, tm), F32)], axis=0).astype(BF16)
    before = (lax.broadcasted_iota(jnp.int32, (tm, tm), 0) < lax.broadcasted_iota(jnp.int32, (tm, tm), 1)).astype(BF16)
    rank = jnp.dot(chosen_b[0:N_EXPERTS, :], before, preferred_element_type=F32)
    cnt = jnp.dot(chosen_b[0:N_EXPERTS, :], jnp.ones((tm, 128), BF16), preferred_element_type=F32)
    padded = jnp.floor((cnt + (ROW_PAD - 1)) * (1.0 / ROW_PAD)) * ROW_PAD
    lower = (lax.broadcasted_iota(jnp.int32, (N_EXPERTS, N_EXPERTS), 0)
             > lax.broadcasted_iota(jnp.int32, (N_EXPERTS, N_EXPERTS), 1)).astype(F32)
    gstart = jnp.dot(lower, padded, precision=HI, preferred_element_type=F32)
    posmat = gstart[:, 0:1] + rank
    rt_ref[...] = jnp.zeros_like(rt_ref)
    for k in range(TOP_K):
        pos_k = jnp.zeros((1, tm), F32)
        gate_k = jnp.zeros((1, tm), F32)
        for g in range(N_GROUPS):
            hit = idx_g[g] == picks[k]
            pos_k = pos_k + jnp.sum(jnp.where(hit, posmat[g * gsz:(g + 1) * gsz, :], 0.0), axis=0, keepdims=True)
            gate_k = gate_k + jnp.sum(jnp.where(hit, gates[g], 0.0), axis=0, keepdims=True)
        rt_ref[k:k + 1, :] = pos_k
        rt_ref[TOP_K + k:TOP_K + k + 1, :] = gate_k
    row_ref[...] = rt_ref[0:2 * TOP_K, :]
    col_ref[...] = rt_ref[...].T
    cnt_ref[...] = lax.dot_general(jnp.ones((8, tm), BF16), chosen_b, NT, preferred_element_type=F32)


def _router(x1, wr_t, rb):
    T, D = x1.shape
    tm = TOK_TILE
    nst = T // tm
    return pl.pallas_call(
        _router_kernel,
        grid=(nst,),
        in_specs=[pl.BlockSpec((tm, D), lambda i: (i, 0)), pl.BlockSpec(wr_t.shape, lambda i: (0, 0)),
                  pl.BlockSpec(rb.shape, lambda i: (0, 0))],
        out_specs=[pl.BlockSpec((2 * TOP_K, tm), lambda i: (0, i)), pl.BlockSpec((tm, 128), lambda i: (i, 0)),
                   pl.BlockSpec((8, 128), lambda i: (i, 0))],
        out_shape=[jax.ShapeDtypeStruct((2 * TOP_K, T), F32), jax.ShapeDtypeStruct((T, 128), F32),
                   jax.ShapeDtypeStruct((nst * 8, 128), F32)],
        scratch_shapes=[pltpu.VMEM((128, tm), F32)],
        compiler_params=_params(1),
        name="router",
    )(x1, wr_t, rb)


M_LSTART, M_ROWS, M_GOFF = 0, N_EXPERTS, 2 * N_EXPERTS
M_USED, M_USED_PREV = 3 * N_EXPERTS, 3 * N_EXPERTS + 1
M_COLS = 4 * N_EXPERTS


def _ceil_div(x, n):
    assert n & (n - 1) == 0
    return lax.shift_right_logical(x + (n - 1), n.bit_length() - 1)


def _seg_copy(meta_ref, e, col, loc_ref, glob_ref, sem, to_global, placed=True):
    n = pl.multiple_of(meta_ref[0, 0, col + e] * ROW_PAD, ROW_PAD)
    ls = pl.multiple_of(meta_ref[0, 0, M_LSTART + e] * ROW_PAD, ROW_PAD) if placed else 0
    go = pl.multiple_of(meta_ref[0, 0, M_GOFF + e] * ROW_PAD, ROW_PAD) if placed else 0
    loc, glob = loc_ref.at[pl.ds(ls, n)], glob_ref.at[pl.ds(go, n)]
    return (pltpu.make_async_copy(loc, glob, sem) if to_global else pltpu.make_async_copy(glob, loc, sem)), n


def _dispatch_kernel(meta_ref, tail_ref, x_ref, row_ref, xg_ref, loc_ref, zero_ref, sems):
    s = pl.program_id(0)
    last = pl.num_programs(0) - 1
    slot = s % 2
    tm = x_ref.shape[0]

    @pl.when(s == 0)
    def _fill_tails():
        zero_ref[...] = jnp.zeros_like(zero_ref)

        def start(e, c):
            n = pl.multiple_of(tail_ref[1, e] * ROW_PAD, ROW_PAD)
            go = pl.multiple_of(tail_ref[0, e] * ROW_PAD, ROW_PAD)

            @pl.when(n > 0)
            def _():
                pltpu.make_async_copy(zero_ref.at[pl.ds(0, n)], xg_ref.at[pl.ds(go, n)], sems.at[2]).start()
            return c
        lax.fori_loop(0, N_EXPERTS, start, 0)

    RC = 256
    xb = x_ref[...].astype(BF16)
    pos = row_ref[0:TOP_K, :]
    hi = jnp.floor(pos * (1.0 / RC))
    lo = pos - hi * RC
    rows = lax.broadcasted_iota(jnp.int32, (RC, tm), 0).astype(F32)
    lo_hit = [jnp.where(lo[k:k + 1, :] == rows, 1.0, 0.0).astype(BF16) for k in range(TOP_K)]
    used = meta_ref[0, 0, M_USED]

    def chunk(c, carry):
        cf = c.astype(F32)
        onehot = lo_hit[0] * jnp.where(hi[0:1, :] == cf, 1.0, 0.0).astype(BF16)
        for k in range(1, TOP_K):
            onehot = onehot + lo_hit[k] * jnp.where(hi[k:k + 1, :] == cf, 1.0, 0.0).astype(BF16)
        loc_ref[slot, pl.ds(pl.multiple_of(c * RC, RC), RC), :] = jnp.dot(
            onehot, xb, preferred_element_type=F32).astype(BF16)
        return carry
    lax.fori_loop(0, _ceil_div(used, RC // ROW_PAD), chunk, 0)

    def start(e, c):
        cp, n = _seg_copy(meta_ref, e, M_ROWS, loc_ref.at[slot], xg_ref, sems.at[slot], True)

        @pl.when(n > 0)
        def _():
            cp.start()
        return c
    lax.fori_loop(0, N_EXPERTS, start, 0)

    def wait_all(total_col, buf):
        n = pl.multiple_of(meta_ref[0, 0, total_col] * ROW_PAD, ROW_PAD)

        @pl.when(n > 0)
        def _():
            pltpu.make_async_copy(loc_ref.at[buf, pl.ds(0, n)], xg_ref.at[pl.ds(0, n)], sems.at[buf]).wait()

    @pl.when(s > 0)
    def _wait_prev():
        wait_all(M_USED_PREV, 1 - slot)

    @pl.when(s == last)
    def _wait_own():
        wait_all(M_USED, slot)

    @pl.when(s == 0)
    def _wait_tails():
        def wait(e, c):
            n = pl.multiple_of(tail_ref[1, e] * ROW_PAD, ROW_PAD)
            go = pl.multiple_of(tail_ref[0, e] * ROW_PAD, ROW_PAD)

            @pl.when(n > 0)
            def _():
                pltpu.make_async_copy(zero_ref.at[pl.ds(0, n)], xg_ref.at[pl.ds(go, n)], sems.at[2]).wait()
            return c
        lax.fori_loop(0, N_EXPERTS, wait, 0)


def _dispatch(meta, tail, x1, rowform, p_rows):
    T, D = x1.shape
    tm = TOK_TILE
    return pl.pallas_call(
        _dispatch_kernel,
        grid=(T // tm,),
        in_specs=[pl.BlockSpec((1, 1, M_COLS), lambda i: (i, 0, 0), memory_space=pltpu.SMEM),
                  pl.BlockSpec(memory_space=pltpu.SMEM),
                  pl.BlockSpec((tm, D), lambda i: (i, 0)), pl.BlockSpec((2 * TOP_K, tm), lambda i: (0, i))],
        out_specs=pl.BlockSpec(memory_space=pl.ANY),
        out_shape=jax.ShapeDtypeStruct((p_rows, D), BF16),
        scratch_shapes=[pltpu.VMEM((2, LOC_ROWS, D), BF16), pltpu.VMEM((FFN_BLK, D), BF16),
                        pltpu.SemaphoreType.DMA((3,))],
        compiler_params=_params(1),
        name="moe_dispatch",
    )(meta, tail, x1, rowform)


def _ffn_kernel(bexp_ref, nused_ref, x_ref, wg_ref, wu_ref, wd_ref, y_ref, wgb_ref, wub_ref, wdb_ref):
    i = pl.program_id(0)

    @pl.when(i < nused_ref[0])
    def _():
        @pl.when((i == 0) | (bexp_ref[i] != bexp_ref[jnp.maximum(i - 1, 0)]))
        def _new_expert():
            wgb_ref[...] = wg_ref[...].astype(BF16)
            wub_ref[...] = wu_ref[...].astype(BF16)
            wdb_ref[...] = wd_ref[...].astype(BF16)

        xb = x_ref[...]
        a = jnp.dot(xb, wgb_ref[...], preferred_element_type=F32)
        u = jnp.dot(xb, wub_ref[...], preferred_element_type=F32)
        y_ref[...] = jnp.dot((_silu(a) * u).astype(BF16), wdb_ref[...], preferred_element_type=F32).astype(BF16)


def _ffn(blk_exp, nused, xg, wg, wu, wd, layer):
    P, D = xg.shape
    F = wg.shape[3]
    blk = lambda i, be, nu: (jnp.maximum(jnp.minimum(i, nu[0] - 1), 0), 0)
    wsel = lambda i, be, nu: (layer, be[jnp.maximum(jnp.minimum(i, nu[0] - 1), 0)], 0, 0)
    return pl.pallas_call(
        _ffn_kernel,
        grid_spec=pltpu.PrefetchScalarGridSpec(
            num_scalar_prefetch=2,
            grid=(P // FFN_BLK,),
            in_specs=[pl.BlockSpec((FFN_BLK, D), blk), pl.BlockSpec((None, None, D, F), wsel),
                      pl.BlockSpec((None, None, D, F), wsel), pl.BlockSpec((None, None, F, D), wsel)],
            out_specs=pl.BlockSpec((FFN_BLK, D), blk),
            scratch_shapes=[pltpu.VMEM((D, F), BF16), pltpu.VMEM((D, F), BF16), pltpu.VMEM((F, D), BF16)]),
        out_shape=jax.ShapeDtypeStruct((P, D), BF16),
        compiler_params=_params(1),
        name="moe_ffn",
    )(blk_exp, nused, xg, wg, wu, wd)


def _combine_kernel(meta_ref, next_ref, x_ref, col_ref, yg_ref, sg_ref, su_ref, sd_ref, g_ref, b_ref, x2_ref,
                    loc_ref, z_ref, sems):
    s = pl.program_id(0)
    last = pl.num_programs(0) - 1
    slot = s % 2
    tm = x_ref.shape[0]
    CW = 512
    cw_u = CW // ROW_PAD

    def used_rows(m_ref):
        return m_ref[0, 0, M_USED]

    def fetch(m_ref, buf):
        def start(e, c):
            cp, n = _seg_copy(m_ref, e, M_ROWS, loc_ref.at[buf], yg_ref, sems.at[buf], False)

            @pl.when(n > 0)
            def _():
                cp.start()
            return c
        lax.fori_loop(0, N_EXPERTS, start, 0)
        used = used_rows(m_ref)

        def clear(c, carry):
            loc_ref[buf, pl.ds(pl.multiple_of(c * ROW_PAD, ROW_PAD), ROW_PAD), :] = jnp.zeros(
                (ROW_PAD, loc_ref.shape[2]), BF16)
            return carry
        lax.fori_loop(used, _ceil_div(used, cw_u) * cw_u, clear, 0)

    @pl.when(s == 0)
    def _first():
        fetch(meta_ref, 0)

    @pl.when(s < last)
    def _prefetch():
        fetch(next_ref, 1 - slot)

    x1 = x_ref[...]
    xb = x1.astype(BF16)
    hid = _silu(jnp.dot(xb, sg_ref[...], preferred_element_type=F32)) * jnp.dot(xb, su_ref[...],
                                                                                preferred_element_type=F32)
    z_ref[...] = ALPHA * x1 + jnp.dot(hid.astype(BF16), sd_ref[...], preferred_element_type=F32)

    pos = col_ref[:, 0:TOP_K]
    gate = col_ref[:, TOP_K:2 * TOP_K]
    hi = jnp.floor(pos * (1.0 / CW))
    lo = pos - hi * CW
    lanes = lax.broadcasted_iota(jnp.int32, (tm, CW), 1).astype(F32)
    lo_hit = [jnp.where(lo[:, k:k + 1] == lanes, 1.0, 0.0).astype(BF16) for k in range(TOP_K)]
    n_own = pl.multiple_of(used_rows(meta_ref) * ROW_PAD, ROW_PAD)

    @pl.when(n_own > 0)
    def _wait_own():
        pltpu.make_async_copy(yg_ref.at[pl.ds(0, n_own)], loc_ref.at[slot, pl.ds(0, n_own)], sems.at[slot]).wait()

    def chunk(c, carry):
        cf = c.astype(F32)
        wmat = lo_hit[0] * jnp.where(hi[:, 0:1] == cf, gate[:, 0:1], 0.0).astype(BF16)
        for k in range(1, TOP_K):
            wmat = wmat + lo_hit[k] * jnp.where(hi[:, k:k + 1] == cf, gate[:, k:k + 1], 0.0).astype(BF16)
        rows = loc_ref[slot, pl.ds(pl.multiple_of(c * CW, CW), CW), :]
        z_ref[...] += jnp.dot(wmat, rows, preferred_element_type=F32)
        return carry
    lax.fori_loop(0, _ceil_div(used_rows(meta_ref), cw_u), chunk, 0)
    x2_ref[...] = _layer_norm(z_ref[...], g_ref[...], b_ref[...])


def _combine(meta, x1, colform, yg, sg, su, sd, g, b):
    T, D = x1.shape
    tm = TOK_TILE
    nst = T // tm
    full = lambda i: (0, 0)
    mspec = lambda f: pl.BlockSpec((1, 1, M_COLS), f, memory_space=pltpu.SMEM)
    return pl.pallas_call(
        _combine_kernel,
        grid=(nst,),
        in_specs=[mspec(lambda i: (i, 0, 0)), mspec(lambda i: (jnp.minimum(i + 1, nst - 1), 0, 0)),
                  pl.BlockSpec((tm, D), lambda i: (i, 0)), pl.BlockSpec((tm, 128), lambda i: (i, 0)),
                  pl.BlockSpec(memory_space=pl.ANY),
                  pl.BlockSpec(sg.shape, full), pl.BlockSpec(su.shape, full), pl.BlockSpec(sd.shape, full),
                  pl.BlockSpec((1, D), full), pl.BlockSpec((1, D), full)],
        out_specs=pl.BlockSpec((tm, D), lambda i: (i, 0)),
        out_shape=jax.ShapeDtypeStruct((T, D), F32),
        scratch_shapes=[pltpu.VMEM((2, LOC_ROWS, D), BF16), pltpu.VMEM((tm, D), F32),
                        pltpu.SemaphoreType.DMA((2,))],
        compiler_params=_params(1),
        name="moe_combine_ln",
    )(meta, meta, x1, colform, yg, sg, su, sd, g, b)


def _moe_plan(cnt_out, T):
    nst = T // TOK_TILE
    blk_u = FFN_BLK // ROW_PAD
    cnt = cnt_out.reshape(nst, 8, 128)[:, 0, :N_EXPERTS].astype(jnp.int32)
    rows = (cnt + ROW_PAD - 1) // ROW_PAD
    lstart = jnp.cumsum(rows, axis=1) - rows
    tot = rows.sum(axis=0)
    tot_pad = (tot + blk_u - 1) // blk_u * blk_u
    eend = jnp.cumsum(tot_pad)
    ebase = eend - tot_pad
    goff = ebase[None, :] + jnp.cumsum(rows, axis=0) - rows
    used = rows.sum(axis=1, keepdims=True)
    used_prev = jnp.concatenate([jnp.zeros((1, 1), jnp.int32), used[:-1]], axis=0)
    fill = jnp.zeros((nst, M_COLS - M_USED_PREV - 1), jnp.int32)
    meta = jnp.concatenate([lstart, rows, goff, used, used_prev, fill], axis=1).reshape(nst, 1, M_COLS)
    tail = jnp.stack([ebase + tot, tot_pad - tot])
    nblk = _moe_rows(T) // FFN_BLK
    nused = (eend[-1] // blk_u).reshape(1)
    first_row = jnp.arange(nblk, dtype=jnp.int32) * blk_u
    blk_exp = jnp.minimum(jnp.sum(first_row[:, None] >= eend[None, :], axis=1), N_EXPERTS - 1).astype(jnp.int32)
    return meta, tail, blk_exp, nused


def _moe_rows(T):
    worst = T * TOP_K + (T // TOK_TILE) * N_EXPERTS * (ROW_PAD - 1) + N_EXPERTS * (FFN_BLK - ROW_PAD)
    return -(-worst // FFN_BLK) * FFN_BLK


def _rope_tables(S):
    half = AT_DH // 2
    lane = jnp.arange(128)
    inv = ROPE_THETA ** (-(lane % half).astype(F32) / half)
    ang = jnp.arange(S, dtype=F32)[:, None] * inv[None, :]
    sign = jnp.where((lane % AT_DH) < half, -1.0, 1.0).astype(F32)
    return jnp.cos(ang), jnp.sin(ang) * sign[None, :]


def _pad_cols(a, width):
    return jnp.pad(a, ((0, 0), (0, width - a.shape[1])))


def kernel(x, w_in, ml_i_bias, ml_f_bias, ml_norm_w, ssm_conv_w, ssm_conv_b, ssm_dt_bias, ssm_a_log, ssm_d, ssm_norm_w, gla_gate_w2, gla_gate_b, gla_norm_w, w_out, ln1_g, ln1_b, router_w, router_bias, exp_w_gate, exp_w_up, exp_w_down, sh_w_gate, sh_w_up, sh_w_down, ln2_g, ln2_b):
    B, S, D = x.shape
    T = B * S
    depth = w_in.shape[0]
    assert D == 1024 and S % (AT_SPAN * max(DILATIONS)) == 0 and T % 1024 == 0
    cos, sin = _rope_tables(S)
    o_mi, o_sz, o_sdt, o_aq, o_gq, o_ga = 1024, 1032, 1800, 1804, 2572, 3340
    xf = x.reshape(T, D)
    for l in range(depth):
        w = w_in[l]
        wm = jnp.concatenate([w[:, 0:1024], w[:, o_sz:o_sdt], w[:, o_aq:o_gq], w[:, o_gq:o_ga]], axis=1).astype(BF16)
        wgt = _pad_cols(jnp.concatenate([w[:, o_mi:o_sz], w[:, o_sdt:o_aq], w[:, o_ga:o_ga + GLA_RANK]], axis=1),
                        128).astype(BF16)
        brow = _pad_cols(jnp.concatenate([ml_i_bias[l], ml_f_bias[l], ssm_dt_bias[l]])[None, :], 128).astype(F32)
        alog = jnp.pad(ssm_a_log[l].astype(F32), (G_DT, 128 - G_DT - HEADS))[None, :]
        w2 = jnp.pad(gla_gate_w2[l].astype(F32), ((G_GA, 128 - G_GA - GLA_RANK), (0, 0)))

        ml, ssm, at, gla, gates = _in_proj(xf, wm, wgt, cos, sin, S)
        y_ml = _mlstm(ml, gates, brow, ml_norm_w[l][None, :].astype(F32), B, S)
        y_ssm = _ssd(ssm, gates, brow, alog, ssm_conv_w[l].astype(F32), ssm_conv_b[l][None, :].astype(F32),
                     jnp.repeat(ssm_d[l].astype(F32), SSM_P)[None, :], ssm_norm_w[l][None, :].astype(F32), B, S)
        branches = [_attn(at, d, B, S) for d in DILATIONS]
        y_gla = _gla(gla, gates, w2, gla_gate_b[l][None, :].astype(F32), gla_norm_w[l][None, :].astype(F32), B, S)
        x1 = _out_proj(y_ml, y_ssm, branches[0][0], branches[1][0], branches[2][0],
                       branches[0][1], branches[1][1], branches[2][1], y_gla, xf,
                       w_out[l].astype(BF16), ln1_g[l][None, :].astype(F32), ln1_b[l][None, :].astype(F32))
        rowform, colform, cnt = _router(x1, router_w[l].T.astype(F32), router_bias[l][:, None].astype(F32))
        meta, tail, blk_exp, nused = _moe_plan(cnt, T)
        xg = _dispatch(meta, tail, x1, rowform, _moe_rows(T))
        yg = _ffn(blk_exp, nused, xg, exp_w_gate, exp_w_up, exp_w_down, l)
        xf = _combine(meta, x1, colform, yg, sh_w_gate[l].astype(BF16), sh_w_up[l].astype(BF16),
                      sh_w_down[l].astype(BF16), ln2_g[l][None, :].astype(F32), ln2_b[l][None, :].astype(F32))
    return xf.reshape(B, S, D)
```

```python
import functools

import jax
import jax.numpy as jnp
from jax import lax
from jax.experimental import pallas as pl
from jax.experimental.pallas import tpu as pltpu

F32 = jnp.float32
BF16 = jnp.bfloat16
HI = lax.Precision.HIGHEST
NT = (((1,), (1,)), ((), ()))
TN = (((0,), (0,)), ((), ()))

DEPTH = 4
HEADS = 4
ML_DH = 64
SSM_P = 64
SSM_N = 64
SSM_GROUPS = 2
SSM_CONV = 4
AT_DH = 64
DILATIONS = (1, 4, 16)
AT_SPAN = 128
ROPE_THETA = 10000.0
GLA_DK = 32
GLA_DV = 64
GLA_RANK = 16
GLA_TAU = 16.0
N_EXPERTS = 64
TOP_K = 8
N_GROUPS = 8
TOPK_GROUPS = 4
ROUTED_SCALE = 2.5
ALPHA = (2 * DEPTH) ** 0.25
LN_EPS = 1e-5
CHUNK = 64
TOK_TILE = 256
ROW_PAD = 16
FFN_BLK = 1024
LOC_ROWS = -(-(TOK_TILE * TOP_K + N_EXPERTS * (ROW_PAD - 1)) // 512) * 512

G_MI, G_MF, G_DT, G_GA = 0, 4, 8, 12

VMEM_LIMIT = 48 * 1024 * 1024


def _log_sigmoid(x):
    return jnp.minimum(x, 0.0) - jnp.log(1.0 + jnp.exp(-jnp.abs(x)))


def _softplus(x):
    return jnp.maximum(x, 0.0) + jnp.log(1.0 + jnp.exp(-jnp.abs(x)))


def _silu(x):
    return x * jax.nn.sigmoid(x)


def _layer_norm(z, g, b):
    mu = jnp.mean(z, axis=-1, keepdims=True)
    zc = z - mu
    var = jnp.mean(zc * zc, axis=-1, keepdims=True)
    return zc * lax.rsqrt(var + LN_EPS) * g + b


def _params(n_axes):
    return pltpu.CompilerParams(dimension_semantics=("arbitrary",) * n_axes, vmem_limit_bytes=VMEM_LIMIT)


def _in_proj_kernel(x_ref, wm_ref, wg_ref, cos_ref, sin_ref, ml_ref, ssm_ref, at1_ref, at4_ref, at16_ref, gla_ref,
                    g_ref, atf_ref):
    at_refs = (at1_ref, at4_ref, at16_ref)
    xb = x_ref[...].astype(BF16)

    def mm(lo):
        return jnp.dot(xb, wm_ref[:, lo:lo + 256], preferred_element_type=F32)

    for c in range(4):
        a = mm(256 * c)
        if c == 1:
            a = a * ML_DH ** -0.5
        ml_ref[:, 256 * c:256 * (c + 1)] = a.astype(BF16)
    for c in range(3):
        ssm_ref[:, 256 * c:256 * (c + 1)] = mm(1024 + 256 * c).astype(BF16)

    cos = cos_ref[...]
    sin = sin_ref[...]
    lane = lax.broadcasted_iota(jnp.int32, cos.shape, 1)
    first_half = (lane % AT_DH) < AT_DH // 2

    def rope(a):
        rot = jnp.where(first_half, pltpu.roll(a, 128 - AT_DH // 2, 1), pltpu.roll(a, AT_DH // 2, 1))
        return a * cos + rot * sin

    tm = x_ref.shape[0]
    for c in range(3):
        a = mm(1792 + 256 * c)
        for hh in range(2):
            tile = a[:, 128 * hh:128 * (hh + 1)]
            if c < 2:
                tile = rope(tile) * (AT_DH ** -0.5 if c == 0 else 1.0)
            atf_ref[2 * c + hh] = tile
    for d, ref in zip(DILATIONS, at_refs):
        for r in range(d):
            for j in range(6):
                rows = atf_ref[j] if d == 1 else atf_ref[j, pl.ds(r, tm // d, stride=d), :]
                ref[:, 768 * r + 128 * j:768 * r + 128 * (j + 1)] = rows.astype(BF16)
    for c in range(3):
        gla_ref[:, 256 * c:256 * (c + 1)] = mm(2560 + 256 * c).astype(BF16)
    g_ref[...] = jnp.dot(xb, wg_ref[...], preferred_element_type=F32)


def _in_proj(x, wm, wg, cos, sin, S, tm=512):
    T, D = x.shape
    nS = S // tm
    row = lambda i: (i, 0)
    full = lambda i: (0, 0)
    return pl.pallas_call(
        _in_proj_kernel,
        grid=(T // tm,),
        in_specs=[pl.BlockSpec((tm, D), row), pl.BlockSpec(wm.shape, full), pl.BlockSpec(wg.shape, full),
                  pl.BlockSpec((tm, 128), lambda i: (i % nS, 0)), pl.BlockSpec((tm, 128), lambda i: (i % nS, 0))],
        out_specs=[pl.BlockSpec((tm, 1024), row), pl.BlockSpec((tm, 768), row)]
        + [pl.BlockSpec((tm // d, 768 * d), row) for d in DILATIONS]
        + [pl.BlockSpec((tm, 768), row), pl.BlockSpec((tm, 128), row)],
        out_shape=[jax.ShapeDtypeStruct((T, 1024), BF16), jax.ShapeDtypeStruct((T, 768), BF16)]
        + [jax.ShapeDtypeStruct((T // d, 768 * d), BF16) for d in DILATIONS]
        + [jax.ShapeDtypeStruct((T, 768), BF16), jax.ShapeDtypeStruct((T, 128), F32)],
        scratch_shapes=[pltpu.VMEM((6, tm, 128), F32)],
        compiler_params=_params(1),
        name="in_proj",
    )(x, wm, wg, cos, sin)


def _tri_consts(L):
    ri = lax.broadcasted_iota(jnp.int32, (L, L), 0)
    ci = lax.broadcasted_iota(jnp.int32, (L, L), 1)
    causal = ri >= ci
    return causal, causal.astype(F32), (ri <= ci).astype(F32)


def _mlstm_kernel(ml_ref, g_ref, brow_ref, nw_ref, y_ref, c_ref, m_ref, *, CH):
    @pl.when(pl.program_id(1) == 0)
    def _init():
        c_ref[...] = jnp.zeros_like(c_ref)
        m_ref[...] = jnp.full_like(m_ref, -jnp.inf)

    L = 2 * CHUNK
    causal, tril, triu = _tri_consts(L)
    lane = lax.broadcasted_iota(jnp.int32, (L, 128), 1)
    is_f = (lane >= G_MF) & (lane < G_MF + HEADS)
    ones_v = jnp.ones((L, ML_DH), BF16)
    nw = nw_ref[...]

    def pair(pi, carry):
        rr = pl.multiple_of(pi * L, L)
        g2 = g_ref[pl.ds(rr, L), :] + brow_ref[...]
        vc = jnp.where(is_f, _log_sigmoid(g2), g2)
        vtc = vc.T[0:8, :]
        b_all = jnp.dot(tril, vc, precision=HI, preferred_element_type=F32)
        bt_all = jnp.dot(vtc, triu, precision=HI, preferred_element_type=F32)
        for h in range(HEADS):
            q = ml_ref[pl.ds(rr, L), h * 64:(h + 1) * 64]
            k = ml_ref[pl.ds(rr, L), 256 + h * 64:256 + (h + 1) * 64]
            v = ml_ref[pl.ds(rr, L), 512 + h * 64:512 + (h + 1) * 64]
            og = ml_ref[pl.ds(rr, L), 768 + h * 64:768 + (h + 1) * 64].astype(F32)
            b_col = b_all[:, G_MF + h:G_MF + h + 1]
            li_col = vc[:, G_MI + h:G_MI + h + 1]
            b_row = bt_all[G_MF + h:G_MF + h + 1, :]
            li_row = vtc[G_MI + h:G_MI + h + 1, :]
            m_prev = m_ref[h][0:1, 0:1]
            cst = c_ref[h]

            log_d = jnp.where(causal, b_col - b_row + li_row, -jnp.inf)
            log_inter = b_col + m_prev
            m_t = jnp.maximum(log_inter, jnp.max(log_d, axis=1, keepdims=True))
            w_inter = jnp.exp(log_inter - m_t)
            s = lax.dot_general(q, k, NT, preferred_element_type=F32) * jnp.exp(log_d - m_t)
            qc = jnp.dot(q, cst.astype(BF16), preferred_element_type=F32)
            num = jnp.dot(s.astype(BF16), v, preferred_element_type=F32) + w_inter * qc[:, 0:64]
            den = jnp.sum(s, axis=1, keepdims=True) + w_inter * qc[:, 64:65]
            hh = num / jnp.maximum(jnp.abs(den), jnp.exp(-m_t))

            b_last = b_col[L - 1:L, :]
            log_s = b_last - b_col + li_col
            m_new = jnp.maximum(b_last + m_prev, jnp.max(log_s, axis=0, keepdims=True))
            w_c = jnp.exp(b_last + m_prev - m_new)
            kw = (k.astype(F32) * jnp.exp(log_s - m_new)).astype(BF16)
            c_ref[h, :, 0:64] = w_c * cst[:, 0:64] + lax.dot_general(kw, v, TN, preferred_element_type=F32)
            c_ref[h, :, 64:128] = w_c * cst[:, 64:128] + lax.dot_general(kw, ones_v, TN,
                                                                          preferred_element_type=F32)
            m_ref[h] = jnp.broadcast_to(m_new, (8, 128))

            hc = hh - jnp.mean(hh, axis=1, keepdims=True)
            hn = hc * lax.rsqrt(jnp.mean(hc * hc, axis=1, keepdims=True) + LN_EPS) * nw[:, h * 64:(h + 1) * 64]
            y_ref[pl.ds(rr, L), h * 64:(h + 1) * 64] = (jax.nn.sigmoid(og) * hn).astype(BF16)
        return carry

    lax.fori_loop(0, CH // L, pair, 0)


def _mlstm(ml, gates, brow, nw, B, S, CH=512):
    T = ml.shape[0]
    nS = S // CH
    row = lambda b, j: (b * nS + j, 0)
    full = lambda b, j: (0, 0)
    return pl.pallas_call(
        functools.partial(_mlstm_kernel, CH=CH),
        grid=(B, nS),
        in_specs=[pl.BlockSpec((CH, 1024), row), pl.BlockSpec((CH, 128), row),
                  pl.BlockSpec((1, 128), full), pl.BlockSpec((1, 256), full)],
        out_specs=pl.BlockSpec((CH, 256), row),
        out_shape=jax.ShapeDtypeStruct((T, 256), BF16),
        scratch_shapes=[pltpu.VMEM((HEADS, ML_DH, 128), F32), pltpu.VMEM((HEADS, 8, 128), F32)],
        compiler_params=_params(2),
        name="mlstm",
    )(ml, gates, brow, nw)


def _ssd_kernel(ssm_ref, g_ref, brow_ref, alog_ref, cw_ref, cb_ref, d_ref, nw_ref, y_ref,
                xbuf_ref, xact_ref, st_ref, *, CH):
    @pl.when(pl.program_id(1) == 0)
    def _init():
        xbuf_ref[0:8, :] = jnp.zeros((8, 512), F32)
        st_ref[...] = jnp.zeros_like(st_ref)

    xbuf_ref[8:CH + 8, :] = ssm_ref[:, 256:768].astype(F32)
    conv = cb_ref[...] + cw_ref[0:1, :] * xbuf_ref[5:5 + CH, :]
    for j in range(1, SSM_CONV):
        conv = conv + cw_ref[j:j + 1, :] * xbuf_ref[5 + j:5 + j + CH, :]
    xact_ref[...] = _silu(conv)
    xbuf_ref[0:8, :] = xbuf_ref[CH:CH + 8, :]

    L = 2 * CHUNK
    causal, tril, triu = _tri_consts(L)
    lane = lax.broadcasted_iota(jnp.int32, (1, 128), 1)
    a_row = jnp.where((lane >= G_DT) & (lane < G_DT + HEADS), -jnp.exp(alog_ref[...]), 0.0)
    dskip = d_ref[...]
    nw = nw_ref[...]

    def pair(pi, carry):
        rr = pl.multiple_of(pi * L, L)
        dt2 = _softplus(g_ref[pl.ds(rr, L), :] + brow_ref[...])
        a2 = dt2 * a_row
        a2_t = a2.T
        acs_all = jnp.dot(tril, a2, precision=HI, preferred_element_type=F32)
        acs_t = jnp.dot(a2_t[G_DT:G_DT + 8, :], triu, precision=HI, preferred_element_type=F32)
        cb = []
        bmat = []
        cmat = []
        for g in range(SSM_GROUPS):
            bm = xact_ref[pl.ds(rr, L), 256 + g * 64:256 + (g + 1) * 64]
            cm = xact_ref[pl.ds(rr, L), 384 + g * 64:384 + (g + 1) * 64].astype(BF16)
            bmat.append(bm)
            cmat.append(cm)
            cb.append(lax.dot_general(cm, bm.astype(BF16), NT, preferred_element_type=F32))
        gated = []
        ssq = jnp.zeros((L, 1), F32)
        for h in range(HEADS):
            g = h // (HEADS // SSM_GROUPS)
            acs_col = acs_all[:, G_DT + h:G_DT + h + 1]
            acs_row = acs_t[h:h + 1, :]
            dt_col = dt2[:, G_DT + h:G_DT + h + 1]
            xh = xact_ref[pl.ds(rr, L), h * 64:(h + 1) * 64]
            xdt = (xh * dt_col).astype(BF16)
            st = st_ref[h]
            mmat = cb[g] * jnp.exp(jnp.where(causal, acs_col - acs_row, -jnp.inf))
            y = jnp.dot(mmat.astype(BF16), xdt, preferred_element_type=F32)
            y = y + jnp.dot(cmat[g], st.astype(BF16), preferred_element_type=F32) * jnp.exp(acs_col)
            y = y + xh * dskip[:, h * 64:(h + 1) * 64]
            acs_last = acs_col[L - 1:L, :]
            bdec = (bmat[g] * jnp.exp(acs_last - acs_col)).astype(BF16)
            st_ref[h] = jnp.exp(acs_last) * st + lax.dot_general(bdec, xdt, TN, preferred_element_type=F32)
            z = ssm_ref[pl.ds(rr, L), h * 64:(h + 1) * 64].astype(F32)
            yg = y * _silu(z)
            ssq = ssq + jnp.sum(yg * yg, axis=1, keepdims=True)
            gated.append(yg)
        scale = lax.rsqrt(ssq / (HEADS * SSM_P) + LN_EPS)
        for h in range(HEADS):
            y_ref[pl.ds(rr, L), h * 64:(h + 1) * 64] = (gated[h] * scale * nw[:, h * 64:(h + 1) * 64]).astype(BF16)
        return carry

    lax.fori_loop(0, CH // L, pair, 0)


def _ssd(ssm, gates, brow, alog, cw, cb, dskip, nw, B, S, CH=512):
    T = ssm.shape[0]
    nS = S // CH
    row = lambda b, j: (b * nS + j, 0)
    full = lambda b, j: (0, 0)
    return pl.pallas_call(
        functools.partial(_ssd_kernel, CH=CH),
        grid=(B, nS),
        in_specs=[pl.BlockSpec((CH, 768), row), pl.BlockSpec((CH, 128), row), pl.BlockSpec((1, 128), full),
                  pl.BlockSpec((1, 128), full), pl.BlockSpec((SSM_CONV, 512), full), pl.BlockSpec((1, 512), full),
                  pl.BlockSpec((1, 256), full), pl.BlockSpec((1, 256), full)],
        out_specs=pl.BlockSpec((CH, 256), row),
        out_shape=jax.ShapeDtypeStruct((T, 256), BF16),
        scratch_shapes=[pltpu.VMEM((CH + 8, 512), F32), pltpu.VMEM((CH, 512), F32),
                        pltpu.VMEM((HEADS, SSM_N, SSM_P), F32)],
        compiler_params=_params(2),
        name="ssd",
    )(ssm, gates, brow, alog, cw, cb, dskip, nw)


def _attn_kernel(at_ref, o_ref, lse_ref, *, N):
    W = AT_SPAN
    ri = lax.broadcasted_iota(jnp.int32, (W, W), 0)
    ci = lax.broadcasted_iota(jnp.int32, (W, W), 1)
    cur_ok = ri >= ci
    prev_ok = ci >= ri
    low_lanes = ci < AT_DH
    ones = jnp.ones((W, W), BF16)

    NB = 2 if (N // W) % 2 == 0 else 1

    def pair_attention(q2, kc, kp, vc, vp, pmask):
        outs, lses = [], []
        for first in (True, False):
            qh = jnp.where(low_lanes if first else jnp.logical_not(low_lanes), q2, jnp.zeros_like(q2))
            sc = jnp.where(cur_ok, lax.dot_general(qh, kc, NT, preferred_element_type=F32), -jnp.inf)
            sp = jnp.where(pmask, lax.dot_general(qh, kp, NT, preferred_element_type=F32), -jnp.inf)
            m = jnp.max(jnp.maximum(sc, sp), axis=1, keepdims=True)
            pc = jnp.exp(sc - m).astype(BF16)
            pp = jnp.exp(sp - m).astype(BF16)
            acc = jnp.dot(pc, vc, preferred_element_type=F32) + jnp.dot(pp, vp, preferred_element_type=F32)
            den = jnp.dot(pc, ones, preferred_element_type=F32) + jnp.dot(pp, ones, preferred_element_type=F32)
            outs.append(acc / den)
            lses.append(m + jnp.log(den[:, 0:1]))
        return jnp.where(low_lanes, outs[0], outs[1]).astype(BF16), jnp.where(low_lanes, lses[0], lses[1])

    def blk(i, carry):
        work = []
        for u in range(NB):
            n = i * NB + u
            r0 = pl.multiple_of(n * W, W)
            rp = pl.multiple_of(jnp.maximum(n - 1, 0) * W, W)
            for p in range(HEADS // 2):
                lanes = lambda base: slice(base + 128 * p, base + 128 * (p + 1))
                work.append((r0, p, prev_ok & (n > 0),
                             at_ref[pl.ds(r0, W), lanes(0)], at_ref[pl.ds(r0, W), lanes(256)],
                             at_ref[pl.ds(rp, W), lanes(256)], at_ref[pl.ds(r0, W), lanes(512)],
                             at_ref[pl.ds(rp, W), lanes(512)]))
        done = [(r0, p) + pair_attention(q2, kc, kp, vc, vp, pmask) for r0, p, pmask, q2, kc, kp, vc, vp in work]
        for r0, p, o2, lse2 in done:
            o_ref[pl.ds(r0, W), 128 * p:128 * (p + 1)] = o2
            lse_ref[pl.ds(r0, W), 128 * p:128 * (p + 1)] = lse2
        return carry

    assert (N // W) % NB == 0
    lax.fori_loop(0, N // (W * NB), blk, 0)


def _attn(atv, d, B, S):
    N = S // d
    o, lse = pl.pallas_call(
        functools.partial(_attn_kernel, N=N),
        grid=(B, d),
        in_specs=[pl.BlockSpec((None, N, 768), lambda b, r: (b, 0, r))],
        out_specs=[pl.BlockSpec((None, N, 256), lambda b, r: (b, 0, r)),
                   pl.BlockSpec((None, N, 256), lambda b, r: (b, 0, r))],
        out_shape=[jax.ShapeDtypeStruct((B, N, d * 256), BF16), jax.ShapeDtypeStruct((B, N, d * 256), F32)],
        compiler_params=_params(2),
        name=f"attn_d{d}",
    )(atv.reshape(B, N, d * 768))
    return o.reshape(B * N, d * 256), lse.reshape(B * N, d * 256)


def _gla_kernel(gla_ref, g_ref, w2_ref, b2_ref, nw_ref, y_ref, st_ref, *, CH):
    @pl.when(pl.program_id(1) == 0)
    def _init():
        st_ref[...] = jnp.zeros_like(st_ref)

    L = CHUNK
    SB = 16
    ri = lax.broadcasted_iota(jnp.int32, (L, L), 0)
    ci = lax.broadcasted_iota(jnp.int32, (L, L), 1)
    tril_blk = ((ri >= ci) & (ri // SB == ci // SB)).astype(F32)
    tl = lax.broadcasted_iota(jnp.int32, (L, 128), 0) % SB
    er = lax.broadcasted_iota(jnp.int32, (128, 256), 0) // GLA_DK
    ec = lax.broadcasted_iota(jnp.int32, (128, 256), 1) // GLA_DV
    head_expand = (er == ec).astype(BF16)
    sr = lax.broadcasted_iota(jnp.int32, (256, 128), 0) // GLA_DV
    sc = lax.broadcasted_iota(jnp.int32, (256, 128), 1) // GLA_DK
    st_mask = (sr == sc).astype(F32)
    nw = nw_ref[...]

    def chunk(ci_, carry):
        rr = pl.multiple_of(ci_ * L, L)
        lg = _log_sigmoid(jnp.dot(g_ref[pl.ds(rr, L), :], w2_ref[...], preferred_element_type=F32)
                          + b2_ref[...]) / GLA_TAU
        c = jnp.dot(tril_blk, lg, precision=HI, preferred_element_type=F32)
        q = gla_ref[pl.ds(rr, L), 0:128].astype(F32) * GLA_DK ** -0.5
        k = gla_ref[pl.ds(rr, L), 128:256].astype(F32)
        vb = gla_ref[pl.ds(rr, L), 256:512]
        v = vb.astype(F32)
        c4 = c.reshape(L // SB, SB, 128)
        k4 = k.reshape(L // SB, SB, 128)
        v4 = v.reshape(L // SB, SB, 256)

        def bcast(x4, j, width):
            return jnp.broadcast_to(x4[:, j:j + 1, :], (L // SB, SB, width)).reshape(L, width)

        o = jnp.zeros((L, 256), F32)
        for j in range(SB):
            dec = jnp.exp(jnp.minimum(c - bcast(c4, j, 128), 0.0))
            p = jnp.where(tl >= j, q * dec * bcast(k4, j, 128), 0.0)
            a = jnp.dot(p.astype(BF16), head_expand, preferred_element_type=F32)
            o = o + a * bcast(v4, j, 256)

        outs = []
        for i in range(L // SB):
            cblk = c[i * SB:(i + 1) * SB, :]
            clast = cblk[SB - 1:SB, :]
            st = st_ref[...]
            qe = (q[i * SB:(i + 1) * SB, :] * jnp.exp(cblk)).astype(BF16)
            outs.append(o[i * SB:(i + 1) * SB, :] + lax.dot_general(qe, st.astype(BF16), NT,
                                                                    preferred_element_type=F32))
            ke = (k[i * SB:(i + 1) * SB, :] * jnp.exp(clast - cblk)).astype(BF16)
            upd = lax.dot_general(vb[i * SB:(i + 1) * SB, :], ke, TN, preferred_element_type=F32)
            st_ref[...] = st * jnp.exp(clast) + st_mask * upd
        for i in range(L // SB):
            rg = gla_ref[pl.ds(pl.multiple_of(rr + i * SB, SB), SB), 512:768].astype(F32)
            for h in range(HEADS):
                oh = outs[i][:, h * 64:(h + 1) * 64]
                rms = lax.rsqrt(jnp.mean(oh * oh, axis=1, keepdims=True) + LN_EPS)
                y_ref[pl.ds(pl.multiple_of(rr + i * SB, SB), SB), h * 64:(h + 1) * 64] = (
                    oh * rms * nw[:, h * 64:(h + 1) * 64] * _silu(rg[:, h * 64:(h + 1) * 64])).astype(BF16)
        return carry

    lax.fori_loop(0, CH // L, chunk, 0, unroll=2)


def _gla(gla, gates, w2, b2, nw, B, S, CH=512):
    T = gla.shape[0]
    nS = S // CH
    row = lambda b, j: (b * nS + j, 0)
    full = lambda b, j: (0, 0)
    return pl.pallas_call(
        functools.partial(_gla_kernel, CH=CH),
        grid=(B, nS),
        in_specs=[pl.BlockSpec((CH, 768), row), pl.BlockSpec((CH, 128), row), pl.BlockSpec((128, 128), full),
                  pl.BlockSpec((1, 128), full), pl.BlockSpec((1, 256), full)],
        out_specs=pl.BlockSpec((CH, 256), row),
        out_shape=jax.ShapeDtypeStruct((T, 256), BF16),
        scratch_shapes=[pltpu.VMEM((HEADS * GLA_DV, HEADS * GLA_DK), F32)],
        compiler_params=_params(2),
        name="gla",
    )(gla, gates, w2, b2, nw)


def _out_proj_kernel(yml_ref, yssm_ref, o1_ref, o4_ref, o16_ref, l1_ref, l4_ref, l16_ref, ygla_ref, x_ref,
                     wo_ref, g_ref, b_ref, x1_ref, os_ref, ls_ref):
    tm = x_ref.shape[0]
    for i, (d, o_ref, l_ref) in enumerate(((DILATIONS[1], o4_ref, l4_ref), (DILATIONS[2], o16_ref, l16_ref))):
        for r in range(d):
            for j in range(2):
                cols = slice(256 * r + 128 * j, 256 * r + 128 * (j + 1))
                os_ref[i, j, pl.ds(r, tm // d, stride=d), :] = o_ref[:, cols].astype(F32)
                ls_ref[i, j, pl.ds(r, tm // d, stride=d), :] = l_ref[:, cols]
    yat = []
    for j in range(2):
        o1 = o1_ref[:, 128 * j:128 * (j + 1)].astype(F32)
        l1 = l1_ref[:, 128 * j:128 * (j + 1)]
        l4, l16 = ls_ref[0, j], ls_ref[1, j]
        mx = jnp.maximum(jnp.maximum(l1, l4), l16)
        e1, e4, e16 = jnp.exp(l1 - mx), jnp.exp(l4 - mx), jnp.exp(l16 - mx)
        yat.append(((e1 * o1 + e4 * os_ref[0, j] + e16 * os_ref[1, j]) / (e1 + e4 + e16)).astype(BF16))
    acc = jnp.dot(yml_ref[...], wo_ref[0:256, :], preferred_element_type=F32)
    acc = acc + jnp.dot(yssm_ref[...], wo_ref[256:512, :], preferred_element_type=F32)
    acc = acc + jnp.dot(yat[0], wo_ref[512:640, :], preferred_element_type=F32)
    acc = acc + jnp.dot(yat[1], wo_ref[640:768, :], preferred_element_type=F32)
    acc = acc + jnp.dot(ygla_ref[...], wo_ref[768:1024, :], preferred_element_type=F32)
    x1_ref[...] = _layer_norm(ALPHA * x_ref[...] + acc, g_ref[...], b_ref[...])


def _out_proj(yml, yssm, branches, ygla, x, wo, g, b, tm=512):
    T, D = x.shape
    row = lambda i: (i, 0)
    full = lambda i: (0, 0)
    small = pl.BlockSpec((tm, 256), row)
    dil = [pl.BlockSpec((tm // d, 256 * d), row) for d in DILATIONS]
    return pl.pallas_call(
        _out_proj_kernel,
        grid=(T // tm,),
        in_specs=[small, small] + dil + dil + [small, pl.BlockSpec((tm, D), row), pl.BlockSpec(wo.shape, full),
                                               pl.BlockSpec((1, D), full), pl.BlockSpec((1, D), full)],
        out_specs=pl.BlockSpec((tm, D), row),
        out_shape=jax.ShapeDtypeStruct((T, D), F32),
        scratch_shapes=[pltpu.VMEM((2, 2, tm, 128), F32), pltpu.VMEM((2, 2, tm, 128), F32)],
        compiler_params=_params(1),
        name="out_proj_ln",
    )(yml, yssm, *[o for o, _ in branches], *[l for _, l in branches], ygla, x, wo, g, b)


def _router_kernel(x_ref, wr_ref, rb_ref, row_ref, col_ref, cnt_ref, rt_ref):
    tm = x_ref.shape[0]
    gsz = N_EXPERTS // N_GROUPS
    logits = lax.dot_general(wr_ref[...], x_ref[...], NT, precision=HI, preferred_element_type=F32)
    scores = jax.nn.sigmoid(logits)
    sel = scores + rb_ref[...]
    eidx = lax.broadcasted_iota(jnp.int32, (gsz, tm), 0)
    big = jnp.int32(1 << 20)
    neg = -jnp.inf

    sel_g = [sel[g * gsz:(g + 1) * gsz, :] for g in range(N_GROUPS)]
    idx_g = [eidx + g * gsz for g in range(N_GROUPS)]
    gscore = []
    for g in range(N_GROUPS):
        v = sel_g[g]
        m1 = jnp.max(v, axis=0, keepdims=True)
        i1 = jnp.min(jnp.where(v == m1, idx_g[g], big), axis=0, keepdims=True)
        m2 = jnp.max(jnp.where(idx_g[g] == i1, neg, v), axis=0, keepdims=True)
        gscore.append(m1 + m2)
    gkeep = [jnp.zeros((1, tm), jnp.bool_) for _ in range(N_GROUPS)]
    for _ in range(TOPK_GROUPS):
        m = functools.reduce(jnp.maximum, gscore)
        gi = functools.reduce(jnp.minimum, [jnp.where(gscore[g] == m, g, big) for g in range(N_GROUPS)])
        for g in range(N_GROUPS):
            hit = gi == g
            gkeep[g] = gkeep[g] | hit
            gscore[g] = jnp.where(hit, neg, gscore[g])
    cand = [jnp.where(gkeep[g], sel_g[g], neg) for g in range(N_GROUPS)]
    chosen = [jnp.zeros((gsz, tm), jnp.bool_) for _ in range(N_GROUPS)]
    picks = []
    for _ in range(TOP_K):
        m = functools.reduce(jnp.maximum, [jnp.max(c, axis=0, keepdims=True) for c in cand])
        ei = functools.reduce(jnp.minimum, [jnp.min(jnp.where(cand[g] == m, idx_g[g], big), axis=0, keepdims=True)
                                            for g in range(N_GROUPS)])
        picks.append(ei)
        for g in range(N_GROUPS):
            hit = idx_g[g] == ei
            chosen[g] = chosen[g] | hit
            cand[g] = jnp.where(hit, neg, cand[g])
    picked = [jnp.where(chosen[g], scores[g * gsz:(g + 1) * gsz, :], 0.0) for g in range(N_GROUPS)]
    tot = functools.reduce(jnp.add, [jnp.sum(p, axis=0, keepdims=True) for p in picked])
    gates = [p / tot * ROUTED_SCALE for p in picked]

    chosen_b = jnp.concatenate([c.astype(F32) for c in chosen] + [jnp.zeros((N_EXPERTS, tm), F32)], axis=0).astype(BF16)
    before = (lax.broadcasted_iota(jnp.int32, (tm, tm), 0) < lax.broadcasted_iota(jnp.int32, (tm, tm), 1)).astype(BF16)
    rank = jnp.dot(chosen_b[0:N_EXPERTS, :], before, preferred_element_type=F32)
    cnt = jnp.dot(chosen_b[0:N_EXPERTS, :], jnp.ones((tm, 128), BF16), preferred_element_type=F32)
    padded = jnp.floor((cnt + (ROW_PAD - 1)) * (1.0 / ROW_PAD)) * ROW_PAD
    lower = (lax.broadcasted_iota(jnp.int32, (N_EXPERTS, N_EXPERTS), 0)
             > lax.broadcasted_iota(jnp.int32, (N_EXPERTS, N_EXPERTS), 1)).astype(F32)
    gstart = jnp.dot(lower, padded, precision=HI, preferred_element_type=F32)
    posmat = gstart[:, 0:1] + rank
    rt_ref[...] = jnp.zeros_like(rt_ref)
    for k in range(TOP_K):
        pos_k = jnp.zeros((1, tm), F32)
        gate_k = jnp.zeros((1, tm), F32)
        for g in range(N_GROUPS):
            hit = idx_g[g] == picks[k]
            pos_k = pos_k + jnp.sum(jnp.where(hit, posmat[g * gsz:(g + 1) * gsz, :], 0.0), axis=0, keepdims=True)
            gate_k = gate_k + jnp.sum(jnp.where(hit, gates[g], 0.0), axis=0, keepdims=True)
        rt_ref[k:k + 1, :] = pos_k
        rt_ref[TOP_K + k:TOP_K + k + 1, :] = gate_k
    row_ref[...] = rt_ref[0:2 * TOP_K, :]
    col_ref[...] = rt_ref[...].T
    cnt_ref[...] = lax.dot_general(jnp.ones((8, tm), BF16), chosen_b, NT, preferred_element_type=F32)


def _router(x1, wr_t, rb):
    T, D = x1.shape
    tm = TOK_TILE
    nst = T // tm
    return pl.pallas_call(
        _router_kernel,
        grid=(nst,),
        in_specs=[pl.BlockSpec((tm, D), lambda i: (i, 0)), pl.BlockSpec(wr_t.shape, lambda i: (0, 0)),
                  pl.BlockSpec(rb.shape, lambda i: (0, 0))],
        out_specs=[pl.BlockSpec((2 * TOP_K, tm), lambda i: (0, i)), pl.BlockSpec((tm, 128), lambda i: (i, 0)),
                   pl.BlockSpec((8, 128), lambda i: (i, 0))],
        out_shape=[jax.ShapeDtypeStruct((2 * TOP_K, T), F32), jax.ShapeDtypeStruct((T, 128), F32),
                   jax.ShapeDtypeStruct((nst * 8, 128), F32)],
        scratch_shapes=[pltpu.VMEM((128, tm), F32)],
        compiler_params=_params(1),
        name="router",
    )(x1, wr_t, rb)


M_LSTART, M_ROWS, M_GOFF = 0, N_EXPERTS, 2 * N_EXPERTS
M_USED, M_USED_PREV = 3 * N_EXPERTS, 3 * N_EXPERTS + 1
M_COLS = 4 * N_EXPERTS


def _ceil_div(x, n):
    assert n & (n - 1) == 0
    return lax.shift_right_logical(x + (n - 1), n.bit_length() - 1)


def _seg_copy(meta_ref, e, col, loc_ref, glob_ref, sem, to_global, placed=True):
    n = pl.multiple_of(meta_ref[0, 0, col + e] * ROW_PAD, ROW_PAD)
    ls = pl.multiple_of(meta_ref[0, 0, M_LSTART + e] * ROW_PAD, ROW_PAD) if placed else 0
    go = pl.multiple_of(meta_ref[0, 0, M_GOFF + e] * ROW_PAD, ROW_PAD) if placed else 0
    loc, glob = loc_ref.at[pl.ds(ls, n)], glob_ref.at[pl.ds(go, n)]
    return (pltpu.make_async_copy(loc, glob, sem) if to_global else pltpu.make_async_copy(glob, loc, sem)), n


def _dispatch_kernel(meta_ref, tail_ref, x_ref, row_ref, xg_ref, loc_ref, zero_ref, sems):
    s = pl.program_id(0)
    last = pl.num_programs(0) - 1
    slot = s % 2
    tm = x_ref.shape[0]

    @pl.when(s == 0)
    def _fill_tails():
        zero_ref[...] = jnp.zeros_like(zero_ref)

        def start(e, c):
            n = pl.multiple_of(tail_ref[1, e] * ROW_PAD, ROW_PAD)
            go = pl.multiple_of(tail_ref[0, e] * ROW_PAD, ROW_PAD)

            @pl.when(n > 0)
            def _():
                pltpu.make_async_copy(zero_ref.at[pl.ds(0, n)], xg_ref.at[pl.ds(go, n)], sems.at[2]).start()
            return c
        lax.fori_loop(0, N_EXPERTS, start, 0)

    RC = 256
    xb = x_ref[...].astype(BF16)
    pos = row_ref[0:TOP_K, :]
    hi = jnp.floor(pos * (1.0 / RC))
    lo = pos - hi * RC
    rows = lax.broadcasted_iota(jnp.int32, (RC, tm), 0).astype(F32)
    lo_hit = [jnp.where(lo[k:k + 1, :] == rows, 1.0, 0.0).astype(BF16) for k in range(TOP_K)]
    used = meta_ref[0, 0, M_USED]

    def chunk(c, carry):
        cf = c.astype(F32)
        onehot = lo_hit[0] * jnp.where(hi[0:1, :] == cf, 1.0, 0.0).astype(BF16)
        for k in range(1, TOP_K):
            onehot = onehot + lo_hit[k] * jnp.where(hi[k:k + 1, :] == cf, 1.0, 0.0).astype(BF16)
        loc_ref[slot, pl.ds(pl.multiple_of(c * RC, RC), RC), :] = jnp.dot(
            onehot, xb, preferred_element_type=F32).astype(BF16)
        return carry
    lax.fori_loop(0, _ceil_div(used, RC // ROW_PAD), chunk, 0)

    def start(e, c):
        cp, n = _seg_copy(meta_ref, e, M_ROWS, loc_ref.at[slot], xg_ref, sems.at[slot], True)

        @pl.when(n > 0)
        def _():
            cp.start()
        return c
    lax.fori_loop(0, N_EXPERTS, start, 0)

    def wait_all(total_col, buf):
        n = pl.multiple_of(meta_ref[0, 0, total_col] * ROW_PAD, ROW_PAD)

        @pl.when(n > 0)
        def _():
            pltpu.make_async_copy(loc_ref.at[buf, pl.ds(0, n)], xg_ref.at[pl.ds(0, n)], sems.at[buf]).wait()

    @pl.when(s > 0)
    def _wait_prev():
        wait_all(M_USED_PREV, 1 - slot)

    @pl.when(s == last)
    def _wait_own():
        wait_all(M_USED, slot)

    @pl.when(s == 0)
    def _wait_tails():
        def wait(e, c):
            n = pl.multiple_of(tail_ref[1, e] * ROW_PAD, ROW_PAD)
            go = pl.multiple_of(tail_ref[0, e] * ROW_PAD, ROW_PAD)

            @pl.when(n > 0)
            def _():
                pltpu.make_async_copy(zero_ref.at[pl.ds(0, n)], xg_ref.at[pl.ds(go, n)], sems.at[2]).wait()
            return c
        lax.fori_loop(0, N_EXPERTS, wait, 0)


def _dispatch(meta, tail, x1, rowform, p_rows):
    T, D = x1.shape
    tm = TOK_TILE
    return pl.pallas_call(
        _dispatch_kernel,
        grid=(T // tm,),
        in_specs=[pl.BlockSpec((1, 1, M_COLS), lambda i: (i, 0, 0), memory_space=pltpu.SMEM),
                  pl.BlockSpec(memory_space=pltpu.SMEM),
                  pl.BlockSpec((tm, D), lambda i: (i, 0)), pl.BlockSpec((2 * TOP_K, tm), lambda i: (0, i))],
        out_specs=pl.BlockSpec(memory_space=pl.ANY),
        out_shape=jax.ShapeDtypeStruct((p_rows, D), BF16),
        scratch_shapes=[pltpu.VMEM((2, LOC_ROWS, D), BF16), pltpu.VMEM((FFN_BLK, D), BF16),
                        pltpu.SemaphoreType.DMA((3,))],
        compiler_params=_params(1),
        name="moe_dispatch",
    )(meta, tail, x1, rowform)


def _ffn_kernel(bexp_ref, nused_ref, x_ref, wg_ref, wu_ref, wd_ref, y_ref, wgb_ref, wub_ref, wdb_ref):
    i = pl.program_id(0)

    @pl.when(i < nused_ref[0])
    def _():
        @pl.when((i == 0) | (bexp_ref[i] != bexp_ref[jnp.maximum(i - 1, 0)]))
        def _new_expert():
            wgb_ref[...] = wg_ref[...].astype(BF16)
            wub_ref[...] = wu_ref[...].astype(BF16)
            wdb_ref[...] = wd_ref[...].astype(BF16)

        xb = x_ref[...]
        a = jnp.dot(xb, wgb_ref[...], preferred_element_type=F32)
        u = jnp.dot(xb, wub_ref[...], preferred_element_type=F32)
        y_ref[...] = jnp.dot((_silu(a) * u).astype(BF16), wdb_ref[...], preferred_element_type=F32).astype(BF16)


def _ffn(blk_exp, nused, xg, wg, wu, wd, layer):
    P, D = xg.shape
    F = wg.shape[3]
    blk = lambda i, be, nu: (jnp.maximum(jnp.minimum(i, nu[0] - 1), 0), 0)
    wsel = lambda i, be, nu: (layer, be[jnp.maximum(jnp.minimum(i, nu[0] - 1), 0)], 0, 0)
    return pl.pallas_call(
        _ffn_kernel,
        grid_spec=pltpu.PrefetchScalarGridSpec(
            num_scalar_prefetch=2,
            grid=(P // FFN_BLK,),
            in_specs=[pl.BlockSpec((FFN_BLK, D), blk), pl.BlockSpec((None, None, D, F), wsel),
                      pl.BlockSpec((None, None, D, F), wsel), pl.BlockSpec((None, None, F, D), wsel)],
            out_specs=pl.BlockSpec((FFN_BLK, D), blk),
            scratch_shapes=[pltpu.VMEM((D, F), BF16), pltpu.VMEM((D, F), BF16), pltpu.VMEM((F, D), BF16)]),
        out_shape=jax.ShapeDtypeStruct((P, D), BF16),
        compiler_params=_params(1),
        name="moe_ffn",
    )(blk_exp, nused, xg, wg, wu, wd)


def _combine_kernel(meta_ref, next_ref, x_ref, col_ref, yg_ref, sg_ref, su_ref, sd_ref, g_ref, b_ref, x2_ref,
                    loc_ref, z_ref, sems):
    s = pl.program_id(0)
    last = pl.num_programs(0) - 1
    slot = s % 2
    tm = x_ref.shape[0]
    CW = 512
    cw_u = CW // ROW_PAD

    def used_rows(m_ref):
        return m_ref[0, 0, M_USED]

    def fetch(m_ref, buf):
        def start(e, c):
            cp, n = _seg_copy(m_ref, e, M_ROWS, loc_ref.at[buf], yg_ref, sems.at[buf], False)

            @pl.when(n > 0)
            def _():
                cp.start()
            return c
        lax.fori_loop(0, N_EXPERTS, start, 0)
        used = used_rows(m_ref)

        def clear(c, carry):
            loc_ref[buf, pl.ds(pl.multiple_of(c * ROW_PAD, ROW_PAD), ROW_PAD), :] = jnp.zeros(
                (ROW_PAD, loc_ref.shape[2]), BF16)
            return carry
        lax.fori_loop(used, _ceil_div(used, cw_u) * cw_u, clear, 0)

    @pl.when(s == 0)
    def _first():
        fetch(meta_ref, 0)

    @pl.when(s < last)
    def _prefetch():
        fetch(next_ref, 1 - slot)

    x1 = x_ref[...]
    xb = x1.astype(BF16)
    hid = _silu(jnp.dot(xb, sg_ref[...], preferred_element_type=F32)) * jnp.dot(xb, su_ref[...],
                                                                                preferred_element_type=F32)
    z_ref[...] = ALPHA * x1 + jnp.dot(hid.astype(BF16), sd_ref[...], preferred_element_type=F32)

    pos = col_ref[:, 0:TOP_K]
    gate = col_ref[:, TOP_K:2 * TOP_K]
    hi = jnp.floor(pos * (1.0 / CW))
    lo = pos - hi * CW
    lanes = lax.broadcasted_iota(jnp.int32, (tm, CW), 1).astype(F32)
    lo_hit = [jnp.where(lo[:, k:k + 1] == lanes, 1.0, 0.0).astype(BF16) for k in range(TOP_K)]
    n_own = pl.multiple_of(used_rows(meta_ref) * ROW_PAD, ROW_PAD)

    @pl.when(n_own > 0)
    def _wait_own():
        pltpu.make_async_copy(yg_ref.at[pl.ds(0, n_own)], loc_ref.at[slot, pl.ds(0, n_own)], sems.at[slot]).wait()

    def chunk(c, carry):
        cf = c.astype(F32)
        wmat = lo_hit[0] * jnp.where(hi[:, 0:1] == cf, gate[:, 0:1], 0.0).astype(BF16)
        for k in range(1, TOP_K):
            wmat = wmat + lo_hit[k] * jnp.where(hi[:, k:k + 1] == cf, gate[:, k:k + 1], 0.0).astype(BF16)
        rows = loc_ref[slot, pl.ds(pl.multiple_of(c * CW, CW), CW), :]
        z_ref[...] += jnp.dot(wmat, rows, preferred_element_type=F32)
        return carry
    lax.fori_loop(0, _ceil_div(used_rows(meta_ref), cw_u), chunk, 0)
    x2_ref[...] = _layer_norm(z_ref[...], g_ref[...], b_ref[...])


def _combine(meta, x1, colform, yg, sg, su, sd, g, b):
    T, D = x1.shape
    tm = TOK_TILE
    nst = T // tm
    full = lambda i: (0, 0)
    mspec = lambda f: pl.BlockSpec((1, 1, M_COLS), f, memory_space=pltpu.SMEM)
    return pl.pallas_call(
        _combine_kernel,
        grid=(nst,),
        in_specs=[mspec(lambda i: (i, 0, 0)), mspec(lambda i: (jnp.minimum(i + 1, nst - 1), 0, 0)),
                  pl.BlockSpec((tm, D), lambda i: (i, 0)), pl.BlockSpec((tm, 128), lambda i: (i, 0)),
                  pl.BlockSpec(memory_space=pl.ANY),
                  pl.BlockSpec(sg.shape, full), pl.BlockSpec(su.shape, full), pl.BlockSpec(sd.shape, full),
                  pl.BlockSpec((1, D), full), pl.BlockSpec((1, D), full)],
        out_specs=pl.BlockSpec((tm, D), lambda i: (i, 0)),
        out_shape=jax.ShapeDtypeStruct((T, D), F32),
        scratch_shapes=[pltpu.VMEM((2, LOC_ROWS, D), BF16), pltpu.VMEM((tm, D), F32),
                        pltpu.SemaphoreType.DMA((2,))],
        compiler_params=_params(1),
        name="moe_combine_ln",
    )(meta, meta, x1, colform, yg, sg, su, sd, g, b)


def _moe_plan(cnt_out, T):
    nst = T // TOK_TILE
    blk_u = FFN_BLK // ROW_PAD
    cnt = cnt_out.reshape(nst, 8, 128)[:, 0, :N_EXPERTS].astype(jnp.int32)
    rows = (cnt + ROW_PAD - 1) // ROW_PAD
    lstart = jnp.cumsum(rows, axis=1) - rows
    tot = rows.sum(axis=0)
    tot_pad = (tot + blk_u - 1) // blk_u * blk_u
    eend = jnp.cumsum(tot_pad)
    ebase = eend - tot_pad
    goff = ebase[None, :] + jnp.cumsum(rows, axis=0) - rows
    used = rows.sum(axis=1, keepdims=True)
    used_prev = jnp.concatenate([jnp.zeros((1, 1), jnp.int32), used[:-1]], axis=0)
    fill = jnp.zeros((nst, M_COLS - M_USED_PREV - 1), jnp.int32)
    meta = jnp.concatenate([lstart, rows, goff, used, used_prev, fill], axis=1).reshape(nst, 1, M_COLS)
    tail = jnp.stack([ebase + tot, tot_pad - tot])
    nblk = _moe_rows(T) // FFN_BLK
    nused = (eend[-1] // blk_u).reshape(1)
    first_row = jnp.arange(nblk, dtype=jnp.int32) * blk_u
    blk_exp = jnp.minimum(jnp.sum(first_row[:, None] >= eend[None, :], axis=1), N_EXPERTS - 1).astype(jnp.int32)
    return meta, tail, blk_exp, nused


def _moe_rows(T):
    worst = T * TOP_K + (T // TOK_TILE) * N_EXPERTS * (ROW_PAD - 1) + N_EXPERTS * (FFN_BLK - ROW_PAD)
    return -(-worst // FFN_BLK) * FFN_BLK


def _rope_tables(S):
    half = AT_DH // 2
    lane = jnp.arange(128)
    inv = ROPE_THETA ** (-(lane % half).astype(F32) / half)
    ang = jnp.arange(S, dtype=F32)[:, None] * inv[None, :]
    sign = jnp.where((lane % AT_DH) < half, -1.0, 1.0).astype(F32)
    return jnp.cos(ang), jnp.sin(ang) * sign[None, :]


def _pad_cols(a, width):
    return jnp.pad(a, ((0, 0), (0, width - a.shape[1])))


def kernel(x, w_in, ml_i_bias, ml_f_bias, ml_norm_w, ssm_conv_w, ssm_conv_b, ssm_dt_bias, ssm_a_log, ssm_d, ssm_norm_w, gla_gate_w2, gla_gate_b, gla_norm_w, w_out, ln1_g, ln1_b, router_w, router_bias, exp_w_gate, exp_w_up, exp_w_down, sh_w_gate, sh_w_up, sh_w_down, ln2_g, ln2_b):
    B, S, D = x.shape
    T = B * S
    depth = w_in.shape[0]
    assert D == 1024 and S % (AT_SPAN * max(DILATIONS)) == 0 and T % 1024 == 0
    cos, sin = _rope_tables(S)
    o_mi, o_sz, o_sdt, o_aq, o_gq, o_ga = 1024, 1032, 1800, 1804, 2572, 3340
    xf = x.reshape(T, D)
    for l in range(depth):
        w = w_in[l]
        wm = jnp.concatenate([w[:, 0:1024], w[:, o_sz:o_sdt], w[:, o_aq:o_gq], w[:, o_gq:o_ga]], axis=1).astype(BF16)
        wgt = _pad_cols(jnp.concatenate([w[:, o_mi:o_sz], w[:, o_sdt:o_aq], w[:, o_ga:o_ga + GLA_RANK]], axis=1),
                        128).astype(BF16)
        brow = _pad_cols(jnp.concatenate([ml_i_bias[l], ml_f_bias[l], ssm_dt_bias[l]])[None, :], 128).astype(F32)
        alog = jnp.pad(ssm_a_log[l].astype(F32), (G_DT, 128 - G_DT - HEADS))[None, :]
        w2 = jnp.pad(gla_gate_w2[l].astype(F32), ((G_GA, 128 - G_GA - GLA_RANK), (0, 0)))

        ml, ssm, at1, at4, at16, gla, gates = _in_proj(xf, wm, wgt, cos, sin, S)
        y_ml = _mlstm(ml, gates, brow, ml_norm_w[l][None, :].astype(F32), B, S)
        y_ssm = _ssd(ssm, gates, brow, alog, ssm_conv_w[l].astype(F32), ssm_conv_b[l][None, :].astype(F32),
                     jnp.repeat(ssm_d[l].astype(F32), SSM_P)[None, :], ssm_norm_w[l][None, :].astype(F32), B, S)
        branches = [_attn(atv, d, B, S) for atv, d in zip((at1, at4, at16), DILATIONS)]
        y_gla = _gla(gla, gates, w2, gla_gate_b[l][None, :].astype(F32), gla_norm_w[l][None, :].astype(F32), B, S)
        x1 = _out_proj(y_ml, y_ssm, branches, y_gla, xf,
                       w_out[l].astype(BF16), ln1_g[l][None, :].astype(F32), ln1_b[l][None, :].astype(F32))
        rowform, colform, cnt = _router(x1, router_w[l].T.astype(F32), router_bias[l][:, None].astype(F32))
        meta, tail, blk_exp, nused = _moe_plan(cnt, T)
        xg = _dispatch(meta, tail, x1, rowform, _moe_rows(T))
        yg = _ffn(blk_exp, nused, xg, exp_w_gate, exp_w_up, exp_w_down, l)
        xf = _combine(meta, x1, colform, yg, sh_w_gate[l].astype(BF16), sh_w_up[l].astype(BF16),
                      sh_w_down[l].astype(BF16), ln2_g[l][None, :].astype(F32), ln2_b[l][None, :].astype(F32))
    return xf.reshape(B, S, D)
```

```python
import functools

import jax
import jax.numpy as jnp
from jax import lax
from jax.experimental import pallas as pl
from jax.experimental.pallas import tpu as pltpu

F32 = jnp.float32
BF16 = jnp.bfloat16
HI = lax.Precision.HIGHEST
NT = (((1,), (1,)), ((), ()))
TN = (((0,), (0,)), ((), ()))

DEPTH = 4
HEADS = 4
ML_DH = 64
SSM_P = 64
SSM_N = 64
SSM_GROUPS = 2
SSM_CONV = 4
AT_DH = 64
DILATIONS = (1, 4, 16)
AT_SPAN = 128
ROPE_THETA = 10000.0
GLA_DK = 32
GLA_DV = 64
GLA_RANK = 16
GLA_TAU = 16.0
N_EXPERTS = 64
TOP_K = 8
N_GROUPS = 8
TOPK_GROUPS = 4
ROUTED_SCALE = 2.5
ALPHA = (2 * DEPTH) ** 0.25
LN_EPS = 1e-5
CHUNK = 64
TOK_TILE = 256
ROW_PAD = 16
FFN_BLK = 1024
LOC_ROWS = -(-(TOK_TILE * TOP_K + N_EXPERTS * (ROW_PAD - 1)) // 512) * 512

G_MI, G_MF, G_DT, G_GA = 0, 4, 8, 12

VMEM_LIMIT = 48 * 1024 * 1024


def _log_sigmoid(x):
    return jnp.minimum(x, 0.0) - jnp.log(1.0 + jnp.exp(-jnp.abs(x)))


def _softplus(x):
    return jnp.maximum(x, 0.0) + jnp.log(1.0 + jnp.exp(-jnp.abs(x)))


def _silu(x):
    return x * jax.nn.sigmoid(x)


def _layer_norm(z, g, b):
    mu = jnp.mean(z, axis=-1, keepdims=True)
    zc = z - mu
    var = jnp.mean(zc * zc, axis=-1, keepdims=True)
    return zc * lax.rsqrt(var + LN_EPS) * g + b


def _params(n_axes):
    return pltpu.CompilerParams(dimension_semantics=("arbitrary",) * n_axes, vmem_limit_bytes=VMEM_LIMIT)


W_MI, W_SZ, W_SDT, W_AQ, W_GQ, W_GA = 1024, 1032, 1800, 1804, 2572, 3340
W_MAIN = 3328


def _regroup_kernel(w_ref, wm_ref, wg_ref):
    RB = 256
    for c in range(0, w_ref.shape[0], RB):
        rows = slice(c, c + RB)
        wm_ref[rows, 0:1024] = w_ref[rows, 0:W_MI].astype(BF16)
        wm_ref[rows, 1024:1792] = w_ref[rows, W_SZ:W_SDT].astype(BF16)
        wm_ref[rows, 1792:2560] = w_ref[rows, W_AQ:W_GQ].astype(BF16)
        wm_ref[rows, 2560:W_MAIN] = w_ref[rows, W_GQ:W_GA].astype(BF16)
        wg_ref[rows, :] = jnp.zeros((RB, 128), BF16)
        wg_ref[rows, G_MI:G_DT] = w_ref[rows, W_MI:W_SZ].astype(BF16)
        wg_ref[rows, G_DT:G_GA] = w_ref[rows, W_SDT:W_AQ].astype(BF16)
        wg_ref[rows, G_GA:G_GA + GLA_RANK] = w_ref[rows, W_GA:W_GA + GLA_RANK].astype(BF16)


def _regroup_weights(w_in, layer):
    _, D, N = w_in.shape
    full = lambda i: (0, 0)
    return pl.pallas_call(
        _regroup_kernel,
        grid=(1,),
        in_specs=[pl.BlockSpec((None, D, N), lambda i: (layer, 0, 0))],
        out_specs=[pl.BlockSpec((D, W_MAIN), full), pl.BlockSpec((D, 128), full)],
        out_shape=[jax.ShapeDtypeStruct((D, W_MAIN), BF16), jax.ShapeDtypeStruct((D, 128), BF16)],
        compiler_params=_params(1),
        name="regroup_w_in",
    )(w_in)


def _in_proj_kernel(x_ref, wm_ref, wg_ref, cos_ref, sin_ref, ml_ref, ssm_ref, at1_ref, at4_ref, at16_ref, gla_ref,
                    g_ref, atf_ref):
    at_refs = (at1_ref, at4_ref, at16_ref)
    xb = x_ref[...].astype(BF16)

    def mm(lo):
        return jnp.dot(xb, wm_ref[:, lo:lo + 256], preferred_element_type=F32)

    for c in range(4):
        a = mm(256 * c)
        if c == 1:
            a = a * ML_DH ** -0.5
        ml_ref[:, 256 * c:256 * (c + 1)] = a.astype(BF16)
    for c in range(3):
        ssm_ref[:, 256 * c:256 * (c + 1)] = mm(1024 + 256 * c).astype(BF16)

    cos = cos_ref[...]
    sin = sin_ref[...]
    lane = lax.broadcasted_iota(jnp.int32, cos.shape, 1)
    first_half = (lane % AT_DH) < AT_DH // 2

    def rope(a):
        rot = jnp.where(first_half, pltpu.roll(a, 128 - AT_DH // 2, 1), pltpu.roll(a, AT_DH // 2, 1))
        return a * cos + rot * sin

    tm = x_ref.shape[0]
    for c in range(3):
        a = mm(1792 + 256 * c)
        for hh in range(2):
            tile = a[:, 128 * hh:128 * (hh + 1)]
            if c < 2:
                tile = rope(tile) * (AT_DH ** -0.5 if c == 0 else 1.0)
            atf_ref[2 * c + hh] = tile
    for d, ref in zip(DILATIONS, at_refs):
        for r in range(d):
            for j in range(6):
                rows = atf_ref[j] if d == 1 else atf_ref[j, pl.ds(r, tm // d, stride=d), :]
                ref[:, 768 * r + 128 * j:768 * r + 128 * (j + 1)] = rows.astype(BF16)
    for c in range(3):
        gla_ref[:, 256 * c:256 * (c + 1)] = mm(2560 + 256 * c).astype(BF16)
    g_ref[...] = jnp.dot(xb, wg_ref[...], preferred_element_type=F32)


def _in_proj(x, wm, wg, cos, sin, S, tm=512):
    T, D = x.shape
    nS = S // tm
    row = lambda i: (i, 0)
    full = lambda i: (0, 0)
    return pl.pallas_call(
        _in_proj_kernel,
        grid=(T // tm,),
        in_specs=[pl.BlockSpec((tm, D), row), pl.BlockSpec(wm.shape, full), pl.BlockSpec(wg.shape, full),
                  pl.BlockSpec((tm, 128), lambda i: (i % nS, 0)), pl.BlockSpec((tm, 128), lambda i: (i % nS, 0))],
        out_specs=[pl.BlockSpec((tm, 1024), row), pl.BlockSpec((tm, 768), row)]
        + [pl.BlockSpec((tm // d, 768 * d), row) for d in DILATIONS]
        + [pl.BlockSpec((tm, 768), row), pl.BlockSpec((tm, 128), row)],
        out_shape=[jax.ShapeDtypeStruct((T, 1024), BF16), jax.ShapeDtypeStruct((T, 768), BF16)]
        + [jax.ShapeDtypeStruct((T // d, 768 * d), BF16) for d in DILATIONS]
        + [jax.ShapeDtypeStruct((T, 768), BF16), jax.ShapeDtypeStruct((T, 128), F32)],
        scratch_shapes=[pltpu.VMEM((6, tm, 128), F32)],
        compiler_params=_params(1),
        name="in_proj",
    )(x, wm, wg, cos, sin)


def _tri_consts(L):
    ri = lax.broadcasted_iota(jnp.int32, (L, L), 0)
    ci = lax.broadcasted_iota(jnp.int32, (L, L), 1)
    causal = ri >= ci
    return causal, causal.astype(F32), (ri <= ci).astype(F32)


def _mlstm_kernel(ml_ref, g_ref, brow_ref, nw_ref, y_ref, c_ref, m_ref, *, CH):
    @pl.when(pl.program_id(1) == 0)
    def _init():
        c_ref[...] = jnp.zeros_like(c_ref)
        m_ref[...] = jnp.full_like(m_ref, -jnp.inf)

    L = 2 * CHUNK
    causal, tril, triu = _tri_consts(L)
    lane = lax.broadcasted_iota(jnp.int32, (L, 128), 1)
    is_f = (lane >= G_MF) & (lane < G_MF + HEADS)
    ones_v = jnp.ones((L, ML_DH), BF16)
    nw = nw_ref[...]

    def pair(pi, carry):
        rr = pl.multiple_of(pi * L, L)
        g2 = g_ref[pl.ds(rr, L), :] + brow_ref[...]
        vc = jnp.where(is_f, _log_sigmoid(g2), g2)
        vtc = vc.T[0:8, :]
        b_all = jnp.dot(tril, vc, precision=HI, preferred_element_type=F32)
        bt_all = jnp.dot(vtc, triu, precision=HI, preferred_element_type=F32)
        for h in range(HEADS):
            q = ml_ref[pl.ds(rr, L), h * 64:(h + 1) * 64]
            k = ml_ref[pl.ds(rr, L), 256 + h * 64:256 + (h + 1) * 64]
            v = ml_ref[pl.ds(rr, L), 512 + h * 64:512 + (h + 1) * 64]
            og = ml_ref[pl.ds(rr, L), 768 + h * 64:768 + (h + 1) * 64].astype(F32)
            b_col = b_all[:, G_MF + h:G_MF + h + 1]
            li_col = vc[:, G_MI + h:G_MI + h + 1]
            b_row = bt_all[G_MF + h:G_MF + h + 1, :]
            li_row = vtc[G_MI + h:G_MI + h + 1, :]
            m_prev = m_ref[h][0:1, 0:1]
            cst = c_ref[h]

            log_d = jnp.where(causal, b_col - b_row + li_row, -jnp.inf)
            log_inter = b_col + m_prev
            m_t = jnp.maximum(log_inter, jnp.max(log_d, axis=1, keepdims=True))
            w_inter = jnp.exp(log_inter - m_t)
            s = lax.dot_general(q, k, NT, preferred_element_type=F32) * jnp.exp(log_d - m_t)
            qc = jnp.dot(q, cst.astype(BF16), preferred_element_type=F32)
            num = jnp.dot(s.astype(BF16), v, preferred_element_type=F32) + w_inter * qc[:, 0:64]
            den = jnp.sum(s, axis=1, keepdims=True) + w_inter * qc[:, 64:65]
            hh = num / jnp.maximum(jnp.abs(den), jnp.exp(-m_t))

            b_last = b_col[L - 1:L, :]
            log_s = b_last - b_col + li_col
            m_new = jnp.maximum(b_last + m_prev, jnp.max(log_s, axis=0, keepdims=True))
            w_c = jnp.exp(b_last + m_prev - m_new)
            kw = (k.astype(F32) * jnp.exp(log_s - m_new)).astype(BF16)
            c_ref[h, :, 0:64] = w_c * cst[:, 0:64] + lax.dot_general(kw, v, TN, preferred_element_type=F32)
            c_ref[h, :, 64:128] = w_c * cst[:, 64:128] + lax.dot_general(kw, ones_v, TN,
                                                                          preferred_element_type=F32)
            m_ref[h] = jnp.broadcast_to(m_new, (8, 128))

            hc = hh - jnp.mean(hh, axis=1, keepdims=True)
            hn = hc * lax.rsqrt(jnp.mean(hc * hc, axis=1, keepdims=True) + LN_EPS) * nw[:, h * 64:(h + 1) * 64]
            y_ref[pl.ds(rr, L), h * 64:(h + 1) * 64] = (jax.nn.sigmoid(og) * hn).astype(BF16)
        return carry

    lax.fori_loop(0, CH // L, pair, 0)


def _mlstm(ml, gates, brow, nw, B, S, CH=512):
    T = ml.shape[0]
    nS = S // CH
    row = lambda b, j: (b * nS + j, 0)
    full = lambda b, j: (0, 0)
    return pl.pallas_call(
        functools.partial(_mlstm_kernel, CH=CH),
        grid=(B, nS),
        in_specs=[pl.BlockSpec((CH, 1024), row), pl.BlockSpec((CH, 128), row),
                  pl.BlockSpec((1, 128), full), pl.BlockSpec((1, 256), full)],
        out_specs=pl.BlockSpec((CH, 256), row),
        out_shape=jax.ShapeDtypeStruct((T, 256), BF16),
        scratch_shapes=[pltpu.VMEM((HEADS, ML_DH, 128), F32), pltpu.VMEM((HEADS, 8, 128), F32)],
        compiler_params=_params(2),
        name="mlstm",
    )(ml, gates, brow, nw)


def _ssd_kernel(ssm_ref, g_ref, brow_ref, alog_ref, cw_ref, cb_ref, d_ref, nw_ref, y_ref,
                xbuf_ref, xact_ref, st_ref, *, CH):
    @pl.when(pl.program_id(1) == 0)
    def _init():
        xbuf_ref[0:8, :] = jnp.zeros((8, 512), F32)
        st_ref[...] = jnp.zeros_like(st_ref)

    xbuf_ref[8:CH + 8, :] = ssm_ref[:, 256:768].astype(F32)
    conv = cb_ref[...] + cw_ref[0:1, :] * xbuf_ref[5:5 + CH, :]
    for j in range(1, SSM_CONV):
        conv = conv + cw_ref[j:j + 1, :] * xbuf_ref[5 + j:5 + j + CH, :]
    xact_ref[...] = _silu(conv)
    xbuf_ref[0:8, :] = xbuf_ref[CH:CH + 8, :]

    L = 2 * CHUNK
    causal, tril, triu = _tri_consts(L)
    lane = lax.broadcasted_iota(jnp.int32, (1, 128), 1)
    a_row = jnp.where((lane >= G_DT) & (lane < G_DT + HEADS), -jnp.exp(alog_ref[...]), 0.0)
    dskip = d_ref[...]
    nw = nw_ref[...]

    def pair(pi, carry):
        rr = pl.multiple_of(pi * L, L)
        dt2 = _softplus(g_ref[pl.ds(rr, L), :] + brow_ref[...])
        a2 = dt2 * a_row
        a2_t = a2.T
        acs_all = jnp.dot(tril, a2, precision=HI, preferred_element_type=F32)
        acs_t = jnp.dot(a2_t[G_DT:G_DT + 8, :], triu, precision=HI, preferred_element_type=F32)
        cb = []
        bmat = []
        cmat = []
        for g in range(SSM_GROUPS):
            bm = xact_ref[pl.ds(rr, L), 256 + g * 64:256 + (g + 1) * 64]
            cm = xact_ref[pl.ds(rr, L), 384 + g * 64:384 + (g + 1) * 64].astype(BF16)
            bmat.append(bm)
            cmat.append(cm)
            cb.append(lax.dot_general(cm, bm.astype(BF16), NT, preferred_element_type=F32))
        gated = []
        ssq = jnp.zeros((L, 1), F32)
        for h in range(HEADS):
            g = h // (HEADS // SSM_GROUPS)
            acs_col = acs_all[:, G_DT + h:G_DT + h + 1]
            acs_row = acs_t[h:h + 1, :]
            dt_col = dt2[:, G_DT + h:G_DT + h + 1]
            xh = xact_ref[pl.ds(rr, L), h * 64:(h + 1) * 64]
            xdt = (xh * dt_col).astype(BF16)
            st = st_ref[h]
            mmat = cb[g] * jnp.exp(jnp.where(causal, acs_col - acs_row, -jnp.inf))
            y = jnp.dot(mmat.astype(BF16), xdt, preferred_element_type=F32)
            y = y + jnp.dot(cmat[g], st.astype(BF16), preferred_element_type=F32) * jnp.exp(acs_col)
            y = y + xh * dskip[:, h * 64:(h + 1) * 64]
            acs_last = acs_col[L - 1:L, :]
            bdec = (bmat[g] * jnp.exp(acs_last - acs_col)).astype(BF16)
            st_ref[h] = jnp.exp(acs_last) * st + lax.dot_general(bdec, xdt, TN, preferred_element_type=F32)
            z = ssm_ref[pl.ds(rr, L), h * 64:(h + 1) * 64].astype(F32)
            yg = y * _silu(z)
            ssq = ssq + jnp.sum(yg * yg, axis=1, keepdims=True)
            gated.append(yg)
        scale = lax.rsqrt(ssq / (HEADS * SSM_P) + LN_EPS)
        for h in range(HEADS):
            y_ref[pl.ds(rr, L), h * 64:(h + 1) * 64] = (gated[h] * scale * nw[:, h * 64:(h + 1) * 64]).astype(BF16)
        return carry

    lax.fori_loop(0, CH // L, pair, 0)


def _ssd(ssm, gates, brow, alog, cw, cb, dskip, nw, B, S, CH=512):
    T = ssm.shape[0]
    nS = S // CH
    row = lambda b, j: (b * nS + j, 0)
    full = lambda b, j: (0, 0)
    return pl.pallas_call(
        functools.partial(_ssd_kernel, CH=CH),
        grid=(B, nS),
        in_specs=[pl.BlockSpec((CH, 768), row), pl.BlockSpec((CH, 128), row), pl.BlockSpec((1, 128), full),
                  pl.BlockSpec((1, 128), full), pl.BlockSpec((SSM_CONV, 512), full), pl.BlockSpec((1, 512), full),
                  pl.BlockSpec((1, 256), full), pl.BlockSpec((1, 256), full)],
        out_specs=pl.BlockSpec((CH, 256), row),
        out_shape=jax.ShapeDtypeStruct((T, 256), BF16),
        scratch_shapes=[pltpu.VMEM((CH + 8, 512), F32), pltpu.VMEM((CH, 512), F32),
                        pltpu.VMEM((HEADS, SSM_N, SSM_P), F32)],
        compiler_params=_params(2),
        name="ssd",
    )(ssm, gates, brow, alog, cw, cb, dskip, nw)


def _attn_kernel(at_ref, o_ref, lse_ref, *, N):
    W = AT_SPAN
    ri = lax.broadcasted_iota(jnp.int32, (W, W), 0)
    ci = lax.broadcasted_iota(jnp.int32, (W, W), 1)
    cur_ok = ri >= ci
    prev_ok = ci >= ri
    low_lanes = ci < AT_DH
    ones = jnp.ones((W, W), BF16)

    NB = 2 if (N // W) % 2 == 0 else 1

    def pair_attention(q2, kc, kp, vc, vp, pmask):
        outs, lses = [], []
        for first in (True, False):
            qh = jnp.where(low_lanes if first else jnp.logical_not(low_lanes), q2, jnp.zeros_like(q2))
            sc = jnp.where(cur_ok, lax.dot_general(qh, kc, NT, preferred_element_type=F32), -jnp.inf)
            sp = jnp.where(pmask, lax.dot_general(qh, kp, NT, preferred_element_type=F32), -jnp.inf)
            m = jnp.max(jnp.maximum(sc, sp), axis=1, keepdims=True)
            pc = jnp.exp(sc - m).astype(BF16)
            pp = jnp.exp(sp - m).astype(BF16)
            acc = jnp.dot(pc, vc, preferred_element_type=F32) + jnp.dot(pp, vp, preferred_element_type=F32)
            den = jnp.dot(pc, ones, preferred_element_type=F32) + jnp.dot(pp, ones, preferred_element_type=F32)
            outs.append(acc / den)
            lses.append(m + jnp.log(den[:, 0:1]))
        return jnp.where(low_lanes, outs[0], outs[1]).astype(BF16), jnp.where(low_lanes, lses[0], lses[1])

    def blk(i, carry):
        work = []
        for u in range(NB):
            n = i * NB + u
            r0 = pl.multiple_of(n * W, W)
            rp = pl.multiple_of(jnp.maximum(n - 1, 0) * W, W)
            for p in range(HEADS // 2):
                lanes = lambda base: slice(base + 128 * p, base + 128 * (p + 1))
                work.append((r0, p, prev_ok & (n > 0),
                             at_ref[pl.ds(r0, W), lanes(0)], at_ref[pl.ds(r0, W), lanes(256)],
                             at_ref[pl.ds(rp, W), lanes(256)], at_ref[pl.ds(r0, W), lanes(512)],
                             at_ref[pl.ds(rp, W), lanes(512)]))
        done = [(r0, p) + pair_attention(q2, kc, kp, vc, vp, pmask) for r0, p, pmask, q2, kc, kp, vc, vp in work]
        for r0, p, o2, lse2 in done:
            o_ref[pl.ds(r0, W), 128 * p:128 * (p + 1)] = o2
            lse_ref[pl.ds(r0, W), 128 * p:128 * (p + 1)] = lse2
        return carry

    assert (N // W) % NB == 0
    lax.fori_loop(0, N // (W * NB), blk, 0)


def _attn(atv, d, B, S):
    N = S // d
    o, lse = pl.pallas_call(
        functools.partial(_attn_kernel, N=N),
        grid=(B, d),
        in_specs=[pl.BlockSpec((None, N, 768), lambda b, r: (b, 0, r))],
        out_specs=[pl.BlockSpec((None, N, 256), lambda b, r: (b, 0, r)),
                   pl.BlockSpec((None, N, 256), lambda b, r: (b, 0, r))],
        out_shape=[jax.ShapeDtypeStruct((B, N, d * 256), BF16), jax.ShapeDtypeStruct((B, N, d * 256), F32)],
        compiler_params=_params(2),
        name=f"attn_d{d}",
    )(atv.reshape(B, N, d * 768))
    return o.reshape(B * N, d * 256), lse.reshape(B * N, d * 256)


def _gla_kernel(gla_ref, g_ref, w2_ref, b2_ref, nw_ref, y_ref, st_ref, *, CH):
    @pl.when(pl.program_id(1) == 0)
    def _init():
        st_ref[...] = jnp.zeros_like(st_ref)

    L = CHUNK
    SB = 16
    ri = lax.broadcasted_iota(jnp.int32, (L, L), 0)
    ci = lax.broadcasted_iota(jnp.int32, (L, L), 1)
    tril_blk = ((ri >= ci) & (ri // SB == ci // SB)).astype(F32)
    tl = lax.broadcasted_iota(jnp.int32, (L, 128), 0) % SB
    er = lax.broadcasted_iota(jnp.int32, (128, 256), 0) // GLA_DK
    ec = lax.broadcasted_iota(jnp.int32, (128, 256), 1) // GLA_DV
    head_expand = (er == ec).astype(BF16)
    sr = lax.broadcasted_iota(jnp.int32, (256, 128), 0) // GLA_DV
    sc = lax.broadcasted_iota(jnp.int32, (256, 128), 1) // GLA_DK
    st_mask = (sr == sc).astype(F32)
    nw = nw_ref[...]

    def chunk(ci_, carry):
        rr = pl.multiple_of(ci_ * L, L)
        lg = _log_sigmoid(jnp.dot(g_ref[pl.ds(rr, L), :], w2_ref[...], preferred_element_type=F32)
                          + b2_ref[...]) / GLA_TAU
        c = jnp.dot(tril_blk, lg, precision=HI, preferred_element_type=F32)
        q = gla_ref[pl.ds(rr, L), 0:128].astype(F32) * GLA_DK ** -0.5
        k = gla_ref[pl.ds(rr, L), 128:256].astype(F32)
        vb = gla_ref[pl.ds(rr, L), 256:512]
        v = vb.astype(F32)
        c4 = c.reshape(L // SB, SB, 128)
        k4 = k.reshape(L // SB, SB, 128)
        v4 = v.reshape(L // SB, SB, 256)

        def bcast(x4, j, width):
            return jnp.broadcast_to(x4[:, j:j + 1, :], (L // SB, SB, width)).reshape(L, width)

        o = jnp.zeros((L, 256), F32)
        for j in range(SB):
            dec = jnp.exp(jnp.minimum(c - bcast(c4, j, 128), 0.0))
            p = jnp.where(tl >= j, q * dec * bcast(k4, j, 128), 0.0)
            a = jnp.dot(p.astype(BF16), head_expand, preferred_element_type=F32)
            o = o + a * bcast(v4, j, 256)

        outs = []
        for i in range(L // SB):
            cblk = c[i * SB:(i + 1) * SB, :]
            clast = cblk[SB - 1:SB, :]
            st = st_ref[...]
            qe = (q[i * SB:(i + 1) * SB, :] * jnp.exp(cblk)).astype(BF16)
            outs.append(o[i * SB:(i + 1) * SB, :] + lax.dot_general(qe, st.astype(BF16), NT,
                                                                    preferred_element_type=F32))
            ke = (k[i * SB:(i + 1) * SB, :] * jnp.exp(clast - cblk)).astype(BF16)
            upd = lax.dot_general(vb[i * SB:(i + 1) * SB, :], ke, TN, preferred_element_type=F32)
            st_ref[...] = st * jnp.exp(clast) + st_mask * upd
        for i in range(L // SB):
            rg = gla_ref[pl.ds(pl.multiple_of(rr + i * SB, SB), SB), 512:768].astype(F32)
            for h in range(HEADS):
                oh = outs[i][:, h * 64:(h + 1) * 64]
                rms = lax.rsqrt(jnp.mean(oh * oh, axis=1, keepdims=True) + LN_EPS)
                y_ref[pl.ds(pl.multiple_of(rr + i * SB, SB), SB), h * 64:(h + 1) * 64] = (
                    oh * rms * nw[:, h * 64:(h + 1) * 64] * _silu(rg[:, h * 64:(h + 1) * 64])).astype(BF16)
        return carry

    lax.fori_loop(0, CH // L, chunk, 0, unroll=2)


def _gla(gla, gates, w2, b2, nw, B, S, CH=512):
    T = gla.shape[0]
    nS = S // CH
    row = lambda b, j: (b * nS + j, 0)
    full = lambda b, j: (0, 0)
    return pl.pallas_call(
        functools.partial(_gla_kernel, CH=CH),
        grid=(B, nS),
        in_specs=[pl.BlockSpec((CH, 768), row), pl.BlockSpec((CH, 128), row), pl.BlockSpec((128, 128), full),
                  pl.BlockSpec((1, 128), full), pl.BlockSpec((1, 256), full)],
        out_specs=pl.BlockSpec((CH, 256), row),
        out_shape=jax.ShapeDtypeStruct((T, 256), BF16),
        scratch_shapes=[pltpu.VMEM((HEADS * GLA_DV, HEADS * GLA_DK), F32)],
        compiler_params=_params(2),
        name="gla",
    )(gla, gates, w2, b2, nw)


def _out_proj_kernel(yml_ref, yssm_ref, o1_ref, o4_ref, o16_ref, l1_ref, l4_ref, l16_ref, ygla_ref, x_ref,
                     wo_ref, g_ref, b_ref, x1_ref, os_ref, ls_ref):
    tm = x_ref.shape[0]
    for i, (d, o_ref, l_ref) in enumerate(((DILATIONS[1], o4_ref, l4_ref), (DILATIONS[2], o16_ref, l16_ref))):
        for r in range(d):
            for j in range(2):
                cols = slice(256 * r + 128 * j, 256 * r + 128 * (j + 1))
                os_ref[i, j, pl.ds(r, tm // d, stride=d), :] = o_ref[:, cols].astype(F32)
                ls_ref[i, j, pl.ds(r, tm // d, stride=d), :] = l_ref[:, cols]
    yat = []
    for j in range(2):
        o1 = o1_ref[:, 128 * j:128 * (j + 1)].astype(F32)
        l1 = l1_ref[:, 128 * j:128 * (j + 1)]
        l4, l16 = ls_ref[0, j], ls_ref[1, j]
        mx = jnp.maximum(jnp.maximum(l1, l4), l16)
        e1, e4, e16 = jnp.exp(l1 - mx), jnp.exp(l4 - mx), jnp.exp(l16 - mx)
        yat.append(((e1 * o1 + e4 * os_ref[0, j] + e16 * os_ref[1, j]) / (e1 + e4 + e16)).astype(BF16))
    acc = jnp.dot(yml_ref[...], wo_ref[0:256, :], preferred_element_type=F32)
    acc = acc + jnp.dot(yssm_ref[...], wo_ref[256:512, :], preferred_element_type=F32)
    acc = acc + jnp.dot(yat[0], wo_ref[512:640, :], preferred_element_type=F32)
    acc = acc + jnp.dot(yat[1], wo_ref[640:768, :], preferred_element_type=F32)
    acc = acc + jnp.dot(ygla_ref[...], wo_ref[768:1024, :], preferred_element_type=F32)
    x1_ref[...] = _layer_norm(ALPHA * x_ref[...] + acc, g_ref[...], b_ref[...])


def _out_proj(yml, yssm, branches, ygla, x, wo, g, b, tm=512):
    T, D = x.shape
    row = lambda i: (i, 0)
    full = lambda i: (0, 0)
    small = pl.BlockSpec((tm, 256), row)
    dil = [pl.BlockSpec((tm // d, 256 * d), row) for d in DILATIONS]
    return pl.pallas_call(
        _out_proj_kernel,
        grid=(T // tm,),
        in_specs=[small, small] + dil + dil + [small, pl.BlockSpec((tm, D), row), pl.BlockSpec(wo.shape, full),
                                               pl.BlockSpec((1, D), full), pl.BlockSpec((1, D), full)],
        out_specs=pl.BlockSpec((tm, D), row),
        out_shape=jax.ShapeDtypeStruct((T, D), F32),
        scratch_shapes=[pltpu.VMEM((2, 2, tm, 128), F32), pltpu.VMEM((2, 2, tm, 128), F32)],
        compiler_params=_params(1),
        name="out_proj_ln",
    )(yml, yssm, *[o for o, _ in branches], *[l for _, l in branches], ygla, x, wo, g, b)


def _router_kernel(x_ref, wr_ref, rb_ref, row_ref, col_ref, cnt_ref, rt_ref):
    tm = x_ref.shape[0]
    gsz = N_EXPERTS // N_GROUPS
    logits = lax.dot_general(wr_ref[...], x_ref[...], NT, precision=HI, preferred_element_type=F32)
    scores = jax.nn.sigmoid(logits)
    sel = scores + rb_ref[...]
    eidx = lax.broadcasted_iota(jnp.int32, (gsz, tm), 0)
    big = jnp.int32(1 << 20)
    neg = -jnp.inf

    sel_g = [sel[g * gsz:(g + 1) * gsz, :] for g in range(N_GROUPS)]
    idx_g = [eidx + g * gsz for g in range(N_GROUPS)]
    gscore = []
    for g in range(N_GROUPS):
        v = sel_g[g]
        m1 = jnp.max(v, axis=0, keepdims=True)
        i1 = jnp.min(jnp.where(v == m1, idx_g[g], big), axis=0, keepdims=True)
        m2 = jnp.max(jnp.where(idx_g[g] == i1, neg, v), axis=0, keepdims=True)
        gscore.append(m1 + m2)
    gkeep = [jnp.zeros((1, tm), jnp.bool_) for _ in range(N_GROUPS)]
    for _ in range(TOPK_GROUPS):
        m = functools.reduce(jnp.maximum, gscore)
        gi = functools.reduce(jnp.minimum, [jnp.where(gscore[g] == m, g, big) for g in range(N_GROUPS)])
        for g in range(N_GROUPS):
            hit = gi == g
            gkeep[g] = gkeep[g] | hit
            gscore[g] = jnp.where(hit, neg, gscore[g])
    cand = [jnp.where(gkeep[g], sel_g[g], neg) for g in range(N_GROUPS)]
    chosen = [jnp.zeros((gsz, tm), jnp.bool_) for _ in range(N_GROUPS)]
    picks = []
    for _ in range(TOP_K):
        m = functools.reduce(jnp.maximum, [jnp.max(c, axis=0, keepdims=True) for c in cand])
        ei = functools.reduce(jnp.minimum, [jnp.min(jnp.where(cand[g] == m, idx_g[g], big), axis=0, keepdims=True)
                                            for g in range(N_GROUPS)])
        picks.append(ei)
        for g in range(N_GROUPS):
            hit = idx_g[g] == ei
            chosen[g] = chosen[g] | hit
            cand[g] = jnp.where(hit, neg, cand[g])
    picked = [jnp.where(chosen[g], scores[g * gsz:(g + 1) * gsz, :], 0.0) for g in range(N_GROUPS)]
    tot = functools.reduce(jnp.add, [jnp.sum(p, axis=0, keepdims=True) for p in picked])
    gates = [p / tot * ROUTED_SCALE for p in picked]

    chosen_b = jnp.concatenate([c.astype(F32) for c in chosen] + [jnp.zeros((N_EXPERTS, tm), F32)], axis=0).astype(BF16)
    before = (lax.broadcasted_iota(jnp.int32, (tm, tm), 0) < lax.broadcasted_iota(jnp.int32, (tm, tm), 1)).astype(BF16)
    rank = jnp.dot(chosen_b[0:N_EXPERTS, :], before, preferred_element_type=F32)
    cnt = jnp.dot(chosen_b[0:N_EXPERTS, :], jnp.ones((tm, 128), BF16), preferred_element_type=F32)
    padded = jnp.floor((cnt + (ROW_PAD - 1)) * (1.0 / ROW_PAD)) * ROW_PAD
    lower = (lax.broadcasted_iota(jnp.int32, (N_EXPERTS, N_EXPERTS), 0)
             > lax.broadcasted_iota(jnp.int32, (N_EXPERTS, N_EXPERTS), 1)).astype(F32)
    gstart = jnp.dot(lower, padded, precision=HI, preferred_element_type=F32)
    posmat = gstart[:, 0:1] + rank
    rt_ref[...] = jnp.zeros_like(rt_ref)
    for k in range(TOP_K):
        pos_k = jnp.zeros((1, tm), F32)
        gate_k = jnp.zeros((1, tm), F32)
        for g in range(N_GROUPS):
            hit = idx_g[g] == picks[k]
            pos_k = pos_k + jnp.sum(jnp.where(hit, posmat[g * gsz:(g + 1) * gsz, :], 0.0), axis=0, keepdims=True)
            gate_k = gate_k + jnp.sum(jnp.where(hit, gates[g], 0.0), axis=0, keepdims=True)
        rt_ref[k:k + 1, :] = pos_k
        rt_ref[TOP_K + k:TOP_K + k + 1, :] = gate_k
    row_ref[...] = rt_ref[0:2 * TOP_K, :]
    col_ref[...] = rt_ref[...].T
    cnt_ref[...] = lax.dot_general(jnp.ones((8, tm), BF16), chosen_b, NT, preferred_element_type=F32)


def _router(x1, wr_t, rb):
    T, D = x1.shape
    tm = TOK_TILE
    nst = T // tm
    return pl.pallas_call(
        _router_kernel,
        grid=(nst,),
        in_specs=[pl.BlockSpec((tm, D), lambda i: (i, 0)), pl.BlockSpec(wr_t.shape, lambda i: (0, 0)),
                  pl.BlockSpec(rb.shape, lambda i: (0, 0))],
        out_specs=[pl.BlockSpec((2 * TOP_K, tm), lambda i: (0, i)), pl.BlockSpec((tm, 128), lambda i: (i, 0)),
                   pl.BlockSpec((8, 128), lambda i: (i, 0))],
        out_shape=[jax.ShapeDtypeStruct((2 * TOP_K, T), F32), jax.ShapeDtypeStruct((T, 128), F32),
                   jax.ShapeDtypeStruct((nst * 8, 128), F32)],
        scratch_shapes=[pltpu.VMEM((128, tm), F32)],
        compiler_params=_params(1),
        name="router",
    )(x1, wr_t, rb)


M_LSTART, M_ROWS, M_GOFF = 0, N_EXPERTS, 2 * N_EXPERTS
M_USED, M_USED_PREV = 3 * N_EXPERTS, 3 * N_EXPERTS + 1
M_COLS = 4 * N_EXPERTS


def _ceil_div(x, n):
    assert n & (n - 1) == 0
    return lax.shift_right_logical(x + (n - 1), n.bit_length() - 1)


def _seg_copy(meta_ref, e, col, loc_ref, glob_ref, sem, to_global, placed=True):
    n = pl.multiple_of(meta_ref[0, 0, col + e] * ROW_PAD, ROW_PAD)
    ls = pl.multiple_of(meta_ref[0, 0, M_LSTART + e] * ROW_PAD, ROW_PAD) if placed else 0
    go = pl.multiple_of(meta_ref[0, 0, M_GOFF + e] * ROW_PAD, ROW_PAD) if placed else 0
    loc, glob = loc_ref.at[pl.ds(ls, n)], glob_ref.at[pl.ds(go, n)]
    return (pltpu.make_async_copy(loc, glob, sem) if to_global else pltpu.make_async_copy(glob, loc, sem)), n


def _dispatch_kernel(meta_ref, tail_ref, x_ref, row_ref, xg_ref, loc_ref, zero_ref, sems):
    s = pl.program_id(0)
    last = pl.num_programs(0) - 1
    slot = s % 2
    tm = x_ref.shape[0]

    @pl.when(s == 0)
    def _fill_tails():
        zero_ref[...] = jnp.zeros_like(zero_ref)

        def start(e, c):
            n = pl.multiple_of(tail_ref[1, e] * ROW_PAD, ROW_PAD)
            go = pl.multiple_of(tail_ref[0, e] * ROW_PAD, ROW_PAD)

            @pl.when(n > 0)
            def _():
                pltpu.make_async_copy(zero_ref.at[pl.ds(0, n)], xg_ref.at[pl.ds(go, n)], sems.at[2]).start()
            return c
        lax.fori_loop(0, N_EXPERTS, start, 0)

    RC = 256
    xb = x_ref[...].astype(BF16)
    pos = row_ref[0:TOP_K, :]
    hi = jnp.floor(pos * (1.0 / RC))
    lo = pos - hi * RC
    rows = lax.broadcasted_iota(jnp.int32, (RC, tm), 0).astype(F32)
    lo_hit = [jnp.where(lo[k:k + 1, :] == rows, 1.0, 0.0).astype(BF16) for k in range(TOP_K)]
    used = meta_ref[0, 0, M_USED]

    def chunk(c, carry):
        cf = c.astype(F32)
        onehot = lo_hit[0] * jnp.where(hi[0:1, :] == cf, 1.0, 0.0).astype(BF16)
        for k in range(1, TOP_K):
            onehot = onehot + lo_hit[k] * jnp.where(hi[k:k + 1, :] == cf, 1.0, 0.0).astype(BF16)
        loc_ref[slot, pl.ds(pl.multiple_of(c * RC, RC), RC), :] = jnp.dot(
            onehot, xb, preferred_element_type=F32).astype(BF16)
        return carry
    lax.fori_loop(0, _ceil_div(used, RC // ROW_PAD), chunk, 0)

    def start(e, c):
        cp, n = _seg_copy(meta_ref, e, M_ROWS, loc_ref.at[slot], xg_ref, sems.at[slot], True)

        @pl.when(n > 0)
        def _():
            cp.start()
        return c
    lax.fori_loop(0, N_EXPERTS, start, 0)

    def wait_all(total_col, buf):
        n = pl.multiple_of(meta_ref[0, 0, total_col] * ROW_PAD, ROW_PAD)

        @pl.when(n > 0)
        def _():
            pltpu.make_async_copy(loc_ref.at[buf, pl.ds(0, n)], xg_ref.at[pl.ds(0, n)], sems.at[buf]).wait()

    @pl.when(s > 0)
    def _wait_prev():
        wait_all(M_USED_PREV, 1 - slot)

    @pl.when(s == last)
    def _wait_own():
        wait_all(M_USED, slot)

    @pl.when(s == 0)
    def _wait_tails():
        def wait(e, c):
            n = pl.multiple_of(tail_ref[1, e] * ROW_PAD, ROW_PAD)
            go = pl.multiple_of(tail_ref[0, e] * ROW_PAD, ROW_PAD)

            @pl.when(n > 0)
            def _():
                pltpu.make_async_copy(zero_ref.at[pl.ds(0, n)], xg_ref.at[pl.ds(go, n)], sems.at[2]).wait()
            return c
        lax.fori_loop(0, N_EXPERTS, wait, 0)


def _dispatch(meta, tail, x1, rowform, p_rows):
    T, D = x1.shape
    tm = TOK_TILE
    return pl.pallas_call(
        _dispatch_kernel,
        grid=(T // tm,),
        in_specs=[pl.BlockSpec((1, 1, M_COLS), lambda i: (i, 0, 0), memory_space=pltpu.SMEM),
                  pl.BlockSpec(memory_space=pltpu.SMEM),
                  pl.BlockSpec((tm, D), lambda i: (i, 0)), pl.BlockSpec((2 * TOP_K, tm), lambda i: (0, i))],
        out_specs=pl.BlockSpec(memory_space=pl.ANY),
        out_shape=jax.ShapeDtypeStruct((p_rows, D), BF16),
        scratch_shapes=[pltpu.VMEM((2, LOC_ROWS, D), BF16), pltpu.VMEM((FFN_BLK, D), BF16),
                        pltpu.SemaphoreType.DMA((3,))],
        compiler_params=_params(1),
        name="moe_dispatch",
    )(meta, tail, x1, rowform)


def _ffn_kernel(bexp_ref, nused_ref, x_ref, wg_ref, wu_ref, wd_ref, y_ref, wgb_ref, wub_ref, wdb_ref):
    i = pl.program_id(0)

    @pl.when(i < nused_ref[0])
    def _():
        @pl.when((i == 0) | (bexp_ref[i] != bexp_ref[jnp.maximum(i - 1, 0)]))
        def _new_expert():
            wgb_ref[...] = wg_ref[...].astype(BF16)
            wub_ref[...] = wu_ref[...].astype(BF16)
            wdb_ref[...] = wd_ref[...].astype(BF16)

        xb = x_ref[...]
        a = jnp.dot(xb, wgb_ref[...], preferred_element_type=F32)
        u = jnp.dot(xb, wub_ref[...], preferred_element_type=F32)
        y_ref[...] = jnp.dot((_silu(a) * u).astype(BF16), wdb_ref[...], preferred_element_type=F32).astype(BF16)


def _ffn(blk_exp, nused, xg, wg, wu, wd, layer):
    P, D = xg.shape
    F = wg.shape[3]
    blk = lambda i, be, nu: (jnp.maximum(jnp.minimum(i, nu[0] - 1), 0), 0)
    wsel = lambda i, be, nu: (layer, be[jnp.maximum(jnp.minimum(i, nu[0] - 1), 0)], 0, 0)
    return pl.pallas_call(
        _ffn_kernel,
        grid_spec=pltpu.PrefetchScalarGridSpec(
            num_scalar_prefetch=2,
            grid=(P // FFN_BLK,),
            in_specs=[pl.BlockSpec((FFN_BLK, D), blk), pl.BlockSpec((None, None, D, F), wsel),
                      pl.BlockSpec((None, None, D, F), wsel), pl.BlockSpec((None, None, F, D), wsel)],
            out_specs=pl.BlockSpec((FFN_BLK, D), blk),
            scratch_shapes=[pltpu.VMEM((D, F), BF16), pltpu.VMEM((D, F), BF16), pltpu.VMEM((F, D), BF16)]),
        out_shape=jax.ShapeDtypeStruct((P, D), BF16),
        compiler_params=_params(1),
        name="moe_ffn",
    )(blk_exp, nused, xg, wg, wu, wd)


def _combine_kernel(meta_ref, next_ref, x_ref, col_ref, yg_ref, sg_ref, su_ref, sd_ref, g_ref, b_ref, x2_ref,
                    loc_ref, z_ref, sems):
    s = pl.program_id(0)
    last = pl.num_programs(0) - 1
    slot = s % 2
    tm = x_ref.shape[0]
    CW = 512
    cw_u = CW // ROW_PAD

    def used_rows(m_ref):
        return m_ref[0, 0, M_USED]

    def fetch(m_ref, buf):
        def start(e, c):
            cp, n = _seg_copy(m_ref, e, M_ROWS, loc_ref.at[buf], yg_ref, sems.at[buf], False)

            @pl.when(n > 0)
            def _():
                cp.start()
            return c
        lax.fori_loop(0, N_EXPERTS, start, 0)
        used = used_rows(m_ref)

        def clear(c, carry):
            loc_ref[buf, pl.ds(pl.multiple_of(c * ROW_PAD, ROW_PAD), ROW_PAD), :] = jnp.zeros(
                (ROW_PAD, loc_ref.shape[2]), BF16)
            return carry
        lax.fori_loop(used, _ceil_div(used, cw_u) * cw_u, clear, 0)

    @pl.when(s == 0)
    def _first():
        fetch(meta_ref, 0)

    @pl.when(s < last)
    def _prefetch():
        fetch(next_ref, 1 - slot)

    x1 = x_ref[...]
    xb = x1.astype(BF16)
    hid = _silu(jnp.dot(xb, sg_ref[...], preferred_element_type=F32)) * jnp.dot(xb, su_ref[...],
                                                                                preferred_element_type=F32)
    z_ref[...] = ALPHA * x1 + jnp.dot(hid.astype(BF16), sd_ref[...], preferred_element_type=F32)

    pos = col_ref[:, 0:TOP_K]
    gate = col_ref[:, TOP_K:2 * TOP_K]
    hi = jnp.floor(pos * (1.0 / CW))
    lo = pos - hi * CW
    lanes = lax.broadcasted_iota(jnp.int32, (tm, CW), 1).astype(F32)
    lo_hit = [jnp.where(lo[:, k:k + 1] == lanes, 1.0, 0.0).astype(BF16) for k in range(TOP_K)]
    n_own = pl.multiple_of(used_rows(meta_ref) * ROW_PAD, ROW_PAD)

    @pl.when(n_own > 0)
    def _wait_own():
        pltpu.make_async_copy(yg_ref.at[pl.ds(0, n_own)], loc_ref.at[slot, pl.ds(0, n_own)], sems.at[slot]).wait()

    def chunk(c, carry):
        cf = c.astype(F32)
        wmat = lo_hit[0] * jnp.where(hi[:, 0:1] == cf, gate[:, 0:1], 0.0).astype(BF16)
        for k in range(1, TOP_K):
            wmat = wmat + lo_hit[k] * jnp.where(hi[:, k:k + 1] == cf, gate[:, k:k + 1], 0.0).astype(BF16)
        rows = loc_ref[slot, pl.ds(pl.multiple_of(c * CW, CW), CW), :]
        z_ref[...] += jnp.dot(wmat, rows, preferred_element_type=F32)
        return carry
    lax.fori_loop(0, _ceil_div(used_rows(meta_ref), cw_u), chunk, 0)
    x2_ref[...] = _layer_norm(z_ref[...], g_ref[...], b_ref[...])


def _combine(meta, x1, colform, yg, sg, su, sd, g, b):
    T, D = x1.shape
    tm = TOK_TILE
    nst = T // tm
    full = lambda i: (0, 0)
    mspec = lambda f: pl.BlockSpec((1, 1, M_COLS), f, memory_space=pltpu.SMEM)
    return pl.pallas_call(
        _combine_kernel,
        grid=(nst,),
        in_specs=[mspec(lambda i: (i, 0, 0)), mspec(lambda i: (jnp.minimum(i + 1, nst - 1), 0, 0)),
                  pl.BlockSpec((tm, D), lambda i: (i, 0)), pl.BlockSpec((tm, 128), lambda i: (i, 0)),
                  pl.BlockSpec(memory_space=pl.ANY),
                  pl.BlockSpec(sg.shape, full), pl.BlockSpec(su.shape, full), pl.BlockSpec(sd.shape, full),
                  pl.BlockSpec((1, D), full), pl.BlockSpec((1, D), full)],
        out_specs=pl.BlockSpec((tm, D), lambda i: (i, 0)),
        out_shape=jax.ShapeDtypeStruct((T, D), F32),
        scratch_shapes=[pltpu.VMEM((2, LOC_ROWS, D), BF16), pltpu.VMEM((tm, D), F32),
                        pltpu.SemaphoreType.DMA((2,))],
        compiler_params=_params(1),
        name="moe_combine_ln",
    )(meta, meta, x1, colform, yg, sg, su, sd, g, b)


def _moe_plan(cnt_out, T):
    nst = T // TOK_TILE
    blk_u = FFN_BLK // ROW_PAD
    cnt = cnt_out.reshape(nst, 8, 128)[:, 0, :N_EXPERTS].astype(jnp.int32)
    rows = (cnt + ROW_PAD - 1) // ROW_PAD
    lstart = jnp.cumsum(rows, axis=1) - rows
    tot = rows.sum(axis=0)
    tot_pad = (tot + blk_u - 1) // blk_u * blk_u
    eend = jnp.cumsum(tot_pad)
    ebase = eend - tot_pad
    goff = ebase[None, :] + jnp.cumsum(rows, axis=0) - rows
    used = rows.sum(axis=1, keepdims=True)
    used_prev = jnp.concatenate([jnp.zeros((1, 1), jnp.int32), used[:-1]], axis=0)
    fill = jnp.zeros((nst, M_COLS - M_USED_PREV - 1), jnp.int32)
    meta = jnp.concatenate([lstart, rows, goff, used, used_prev, fill], axis=1).reshape(nst, 1, M_COLS)
    tail = jnp.stack([ebase + tot, tot_pad - tot])
    nblk = _moe_rows(T) // FFN_BLK
    nused = (eend[-1] // blk_u).reshape(1)
    first_row = jnp.arange(nblk, dtype=jnp.int32) * blk_u
    blk_exp = jnp.minimum(jnp.sum(first_row[:, None] >= eend[None, :], axis=1), N_EXPERTS - 1).astype(jnp.int32)
    return meta, tail, blk_exp, nused


def _moe_rows(T):
    worst = T * TOP_K + (T // TOK_TILE) * N_EXPERTS * (ROW_PAD - 1) + N_EXPERTS * (FFN_BLK - ROW_PAD)
    return -(-worst // FFN_BLK) * FFN_BLK


def _rope_tables(S):
    half = AT_DH // 2
    lane = jnp.arange(128)
    inv = ROPE_THETA ** (-(lane % half).astype(F32) / half)
    ang = jnp.arange(S, dtype=F32)[:, None] * inv[None, :]
    sign = jnp.where((lane % AT_DH) < half, -1.0, 1.0).astype(F32)
    return jnp.cos(ang), jnp.sin(ang) * sign[None, :]


def _pad_cols(a, width):
    return jnp.pad(a, ((0, 0), (0, width - a.shape[1])))


def kernel(x, w_in, ml_i_bias, ml_f_bias, ml_norm_w, ssm_conv_w, ssm_conv_b, ssm_dt_bias, ssm_a_log, ssm_d, ssm_norm_w, gla_gate_w2, gla_gate_b, gla_norm_w, w_out, ln1_g, ln1_b, router_w, router_bias, exp_w_gate, exp_w_up, exp_w_down, sh_w_gate, sh_w_up, sh_w_down, ln2_g, ln2_b):
    B, S, D = x.shape
    T = B * S
    depth = w_in.shape[0]
    assert D == 1024 and S % (AT_SPAN * max(DILATIONS)) == 0 and T % 1024 == 0
    cos, sin = _rope_tables(S)
    xf = x.reshape(T, D)
    for l in range(depth):
        wm, wgt = _regroup_weights(w_in, l)
        brow = _pad_cols(jnp.concatenate([ml_i_bias[l], ml_f_bias[l], ssm_dt_bias[l]])[None, :], 128).astype(F32)
        alog = jnp.pad(ssm_a_log[l].astype(F32), (G_DT, 128 - G_DT - HEADS))[None, :]
        w2 = jnp.pad(gla_gate_w2[l].astype(F32), ((G_GA, 128 - G_GA - GLA_RANK), (0, 0)))

        ml, ssm, at1, at4, at16, gla, gates = _in_proj(xf, wm, wgt, cos, sin, S)
        y_ml = _mlstm(ml, gates, brow, ml_norm_w[l][None, :].astype(F32), B, S)
        y_ssm = _ssd(ssm, gates, brow, alog, ssm_conv_w[l].astype(F32), ssm_conv_b[l][None, :].astype(F32),
                     jnp.repeat(ssm_d[l].astype(F32), SSM_P)[None, :], ssm_norm_w[l][None, :].astype(F32), B, S)
        branches = [_attn(atv, d, B, S) for atv, d in zip((at1, at4, at16), DILATIONS)]
        y_gla = _gla(gla, gates, w2, gla_gate_b[l][None, :].astype(F32), gla_norm_w[l][None, :].astype(F32), B, S)
        x1 = _out_proj(y_ml, y_ssm, branches, y_gla, xf,
                       w_out[l].astype(BF16), ln1_g[l][None, :].astype(F32), ln1_b[l][None, :].astype(F32))
        rowform, colform, cnt = _router(x1, router_w[l].T.astype(F32), router_bias[l][:, None].astype(F32))
        meta, tail, blk_exp, nused = _moe_plan(cnt, T)
        xg = _dispatch(meta, tail, x1, rowform, _moe_rows(T))
        yg = _ffn(blk_exp, nused, xg, exp_w_gate, exp_w_up, exp_w_down, l)
        xf = _combine(meta, x1, colform, yg, sh_w_gate[l].astype(BF16), sh_w_up[l].astype(BF16),
                      sh_w_down[l].astype(BF16), ln2_g[l][None, :].astype(F32), ln2_b[l][None, :].astype(F32))
    return xf.reshape(B, S, D)
```

```python
import functools

import jax
import jax.numpy as jnp
from jax import lax
from jax.experimental import pallas as pl
from jax.experimental.pallas import tpu as pltpu

F32 = jnp.float32
BF16 = jnp.bfloat16
HI = lax.Precision.HIGHEST
NT = (((1,), (1,)), ((), ()))
TN = (((0,), (0,)), ((), ()))

DEPTH = 4
HEADS = 4
ML_DH = 64
SSM_P = 64
SSM_N = 64
SSM_GROUPS = 2
SSM_CONV = 4
AT_DH = 64
DILATIONS = (1, 4, 16)
AT_SPAN = 128
ROPE_THETA = 10000.0
GLA_DK = 32
GLA_DV = 64
GLA_RANK = 16
GLA_TAU = 16.0
N_EXPERTS = 64
TOP_K = 8
N_GROUPS = 8
TOPK_GROUPS = 4
ROUTED_SCALE = 2.5
ALPHA = (2 * DEPTH) ** 0.25
LN_EPS = 1e-5
CHUNK = 64
TOK_TILE = 256
ROUTER_TILE = 512
ROW_PAD = 16
FFN_BLK = 1024
LOC_ROWS = -(-(TOK_TILE * TOP_K + N_EXPERTS * ROW_PAD) // 512) * 512

G_MI, G_MF, G_DT, G_GA = 0, 4, 8, 12

VMEM_LIMIT = 48 * 1024 * 1024


def _log_sigmoid(x):
    return jnp.minimum(x, 0.0) - jnp.log(1.0 + jnp.exp(-jnp.abs(x)))


def _softplus(x):
    return jnp.maximum(x, 0.0) + jnp.log(1.0 + jnp.exp(-jnp.abs(x)))


def _silu(x):
    return x * jax.nn.sigmoid(x)


def _layer_norm(z, g, b):
    mu = jnp.mean(z, axis=-1, keepdims=True)
    zc = z - mu
    var = jnp.mean(zc * zc, axis=-1, keepdims=True)
    return zc * lax.rsqrt(var + LN_EPS) * g + b


def _params(n_axes):
    return pltpu.CompilerParams(dimension_semantics=("arbitrary",) * n_axes, vmem_limit_bytes=VMEM_LIMIT)


W_MI, W_SZ, W_SDT, W_AQ, W_GQ, W_GA = 1024, 1032, 1800, 1804, 2572, 3340
W_MAIN = 3328


def _regroup_kernel(w_ref, wm_ref, wg_ref):
    RB = 256
    for c in range(0, w_ref.shape[0], RB):
        rows = slice(c, c + RB)
        wm_ref[rows, 0:1024] = w_ref[rows, 0:W_MI].astype(BF16)
        wm_ref[rows, 1024:1792] = w_ref[rows, W_SZ:W_SDT].astype(BF16)
        wm_ref[rows, 1792:2560] = w_ref[rows, W_AQ:W_GQ].astype(BF16)
        wm_ref[rows, 2560:W_MAIN] = w_ref[rows, W_GQ:W_GA].astype(BF16)
        wg_ref[rows, :] = jnp.zeros((RB, 128), BF16)
        wg_ref[rows, G_MI:G_DT] = w_ref[rows, W_MI:W_SZ].astype(BF16)
        wg_ref[rows, G_DT:G_GA] = w_ref[rows, W_SDT:W_AQ].astype(BF16)
        wg_ref[rows, G_GA:G_GA + GLA_RANK] = w_ref[rows, W_GA:W_GA + GLA_RANK].astype(BF16)


def _regroup_weights(w_in, layer):
    _, D, N = w_in.shape
    full = lambda i: (0, 0)
    return pl.pallas_call(
        _regroup_kernel,
        grid=(1,),
        in_specs=[pl.BlockSpec((None, D, N), lambda i: (layer, 0, 0))],
        out_specs=[pl.BlockSpec((D, W_MAIN), full), pl.BlockSpec((D, 128), full)],
        out_shape=[jax.ShapeDtypeStruct((D, W_MAIN), BF16), jax.ShapeDtypeStruct((D, 128), BF16)],
        compiler_params=_params(1),
        name="regroup_w_in",
    )(w_in)


def _in_proj_kernel(x_ref, wm_ref, wg_ref, cos_ref, sin_ref, ml_ref, ssm_ref, at1_ref, at4_ref, at16_ref, gla_ref,
                    g_ref, atf_ref):
    at_refs = (at1_ref, at4_ref, at16_ref)
    xb = x_ref[...].astype(BF16)

    def mm(lo):
        return jnp.dot(xb, wm_ref[:, lo:lo + 256], preferred_element_type=F32)

    for c in range(4):
        a = mm(256 * c)
        if c == 1:
            a = a * ML_DH ** -0.5
        ml_ref[:, 256 * c:256 * (c + 1)] = a.astype(BF16)
    for c in range(3):
        ssm_ref[:, 256 * c:256 * (c + 1)] = mm(1024 + 256 * c).astype(BF16)

    cos = cos_ref[...]
    sin = sin_ref[...]
    lane = lax.broadcasted_iota(jnp.int32, cos.shape, 1)
    first_half = (lane % AT_DH) < AT_DH // 2

    def rope(a):
        rot = jnp.where(first_half, pltpu.roll(a, 128 - AT_DH // 2, 1), pltpu.roll(a, AT_DH // 2, 1))
        return a * cos + rot * sin

    tm = x_ref.shape[0]
    for c in range(3):
        a = mm(1792 + 256 * c)
        for hh in range(2):
            tile = a[:, 128 * hh:128 * (hh + 1)]
            if c < 2:
                tile = rope(tile) * (AT_DH ** -0.5 if c == 0 else 1.0)
            atf_ref[2 * c + hh] = tile
    for d, ref in zip(DILATIONS, at_refs):
        for r in range(d):
            for j in range(6):
                rows = atf_ref[j] if d == 1 else atf_ref[j, pl.ds(r, tm // d, stride=d), :]
                ref[:, 768 * r + 128 * j:768 * r + 128 * (j + 1)] = rows.astype(BF16)
    for c in range(3):
        gla_ref[:, 256 * c:256 * (c + 1)] = mm(2560 + 256 * c).astype(BF16)
    g_ref[...] = jnp.dot(xb, wg_ref[...], preferred_element_type=F32)


def _in_proj(x, wm, wg, cos, sin, S, tm=512):
    T, D = x.shape
    nS = S // tm
    row = lambda i: (i, 0)
    full = lambda i: (0, 0)
    return pl.pallas_call(
        _in_proj_kernel,
        grid=(T // tm,),
        in_specs=[pl.BlockSpec((tm, D), row), pl.BlockSpec(wm.shape, full), pl.BlockSpec(wg.shape, full),
                  pl.BlockSpec((tm, 128), lambda i: (i % nS, 0)), pl.BlockSpec((tm, 128), lambda i: (i % nS, 0))],
        out_specs=[pl.BlockSpec((tm, 1024), row), pl.BlockSpec((tm, 768), row)]
        + [pl.BlockSpec((tm // d, 768 * d), row) for d in DILATIONS]
        + [pl.BlockSpec((tm, 768), row), pl.BlockSpec((tm, 128), row)],
        out_shape=[jax.ShapeDtypeStruct((T, 1024), BF16), jax.ShapeDtypeStruct((T, 768), BF16)]
        + [jax.ShapeDtypeStruct((T // d, 768 * d), BF16) for d in DILATIONS]
        + [jax.ShapeDtypeStruct((T, 768), BF16), jax.ShapeDtypeStruct((T, 128), F32)],
        scratch_shapes=[pltpu.VMEM((6, tm, 128), F32)],
        compiler_params=_params(1),
        name="in_proj",
    )(x, wm, wg, cos, sin)


def _tri_consts(L):
    ri = lax.broadcasted_iota(jnp.int32, (L, L), 0)
    ci = lax.broadcasted_iota(jnp.int32, (L, L), 1)
    causal = ri >= ci
    return causal, causal.astype(F32), (ri <= ci).astype(F32)


def _mlstm_kernel(ml_ref, g_ref, brow_ref, nw_ref, y_ref, c_ref, m_ref, *, CH):
    @pl.when(pl.program_id(1) == 0)
    def _init():
        c_ref[...] = jnp.zeros_like(c_ref)
        m_ref[...] = jnp.full_like(m_ref, -jnp.inf)

    L = 2 * CHUNK
    causal, tril, triu = _tri_consts(L)
    lane = lax.broadcasted_iota(jnp.int32, (L, 128), 1)
    is_f = (lane >= G_MF) & (lane < G_MF + HEADS)
    ones_v = jnp.ones((L, ML_DH), BF16)
    nw = nw_ref[...]

    def pair(pi, carry):
        rr = pl.multiple_of(pi * L, L)
        g2 = g_ref[pl.ds(rr, L), :] + brow_ref[...]
        vc = jnp.where(is_f, _log_sigmoid(g2), g2)
        vtc = vc.T[0:8, :]
        b_all = jnp.dot(tril, vc, precision=HI, preferred_element_type=F32)
        bt_all = jnp.dot(vtc, triu, precision=HI, preferred_element_type=F32)
        for h in range(HEADS):
            q = ml_ref[pl.ds(rr, L), h * 64:(h + 1) * 64]
            k = ml_ref[pl.ds(rr, L), 256 + h * 64:256 + (h + 1) * 64]
            v = ml_ref[pl.ds(rr, L), 512 + h * 64:512 + (h + 1) * 64]
            og = ml_ref[pl.ds(rr, L), 768 + h * 64:768 + (h + 1) * 64].astype(F32)
            b_col = b_all[:, G_MF + h:G_MF + h + 1]
            li_col = vc[:, G_MI + h:G_MI + h + 1]
            b_row = bt_all[G_MF + h:G_MF + h + 1, :]
            li_row = vtc[G_MI + h:G_MI + h + 1, :]
            m_prev = m_ref[h][0:1, 0:1]
            cst = c_ref[h]

            log_d = jnp.where(causal, b_col - b_row + li_row, -jnp.inf)
            log_inter = b_col + m_prev
            m_t = jnp.maximum(log_inter, jnp.max(log_d, axis=1, keepdims=True))
            w_inter = jnp.exp(log_inter - m_t)
            s = lax.dot_general(q, k, NT, preferred_element_type=F32) * jnp.exp(log_d - m_t)
            qc = jnp.dot(q, cst.astype(BF16), preferred_element_type=F32)
            num = jnp.dot(s.astype(BF16), v, preferred_element_type=F32) + w_inter * qc[:, 0:64]
            den = jnp.sum(s, axis=1, keepdims=True) + w_inter * qc[:, 64:65]
            hh = num / jnp.maximum(jnp.abs(den), jnp.exp(-m_t))

            b_last = b_col[L - 1:L, :]
            log_s = b_last - b_col + li_col
            m_new = jnp.maximum(b_last + m_prev, jnp.max(log_s, axis=0, keepdims=True))
            w_c = jnp.exp(b_last + m_prev - m_new)
            kw = (k.astype(F32) * jnp.exp(log_s - m_new)).astype(BF16)
            c_ref[h, :, 0:64] = w_c * cst[:, 0:64] + lax.dot_general(kw, v, TN, preferred_element_type=F32)
            c_ref[h, :, 64:128] = w_c * cst[:, 64:128] + lax.dot_general(kw, ones_v, TN,
                                                                          preferred_element_type=F32)
            m_ref[h] = jnp.broadcast_to(m_new, (8, 128))

            hc = hh - jnp.mean(hh, axis=1, keepdims=True)
            hn = hc * lax.rsqrt(jnp.mean(hc * hc, axis=1, keepdims=True) + LN_EPS) * nw[:, h * 64:(h + 1) * 64]
            y_ref[pl.ds(rr, L), h * 64:(h + 1) * 64] = (jax.nn.sigmoid(og) * hn).astype(BF16)
        return carry

    lax.fori_loop(0, CH // L, pair, 0)


def _mlstm(ml, gates, brow, nw, B, S, CH=512):
    T = ml.shape[0]
    nS = S // CH
    row = lambda b, j: (b * nS + j, 0)
    full = lambda b, j: (0, 0)
    return pl.pallas_call(
        functools.partial(_mlstm_kernel, CH=CH),
        grid=(B, nS),
        in_specs=[pl.BlockSpec((CH, 1024), row), pl.BlockSpec((CH, 128), row),
                  pl.BlockSpec((1, 128), full), pl.BlockSpec((1, 256), full)],
        out_specs=pl.BlockSpec((CH, 256), row),
        out_shape=jax.ShapeDtypeStruct((T, 256), BF16),
        scratch_shapes=[pltpu.VMEM((HEADS, ML_DH, 128), F32), pltpu.VMEM((HEADS, 8, 128), F32)],
        compiler_params=_params(2),
        name="mlstm",
    )(ml, gates, brow, nw)


def _ssd_kernel(ssm_ref, g_ref, brow_ref, alog_ref, cw_ref, cb_ref, d_ref, nw_ref, y_ref,
                xbuf_ref, xact_ref, st_ref, *, CH):
    @pl.when(pl.program_id(1) == 0)
    def _init():
        xbuf_ref[0:8, :] = jnp.zeros((8, 512), F32)
        st_ref[...] = jnp.zeros_like(st_ref)

    xbuf_ref[8:CH + 8, :] = ssm_ref[:, 256:768].astype(F32)
    conv = cb_ref[...] + cw_ref[0:1, :] * xbuf_ref[5:5 + CH, :]
    for j in range(1, SSM_CONV):
        conv = conv + cw_ref[j:j + 1, :] * xbuf_ref[5 + j:5 + j + CH, :]
    xact_ref[...] = _silu(conv)
    xbuf_ref[0:8, :] = xbuf_ref[CH:CH + 8, :]

    L = 2 * CHUNK
    causal, tril, triu = _tri_consts(L)
    lane = lax.broadcasted_iota(jnp.int32, (1, 128), 1)
    a_row = jnp.where((lane >= G_DT) & (lane < G_DT + HEADS), -jnp.exp(alog_ref[...]), 0.0)
    dskip = d_ref[...]
    nw = nw_ref[...]

    def pair(pi, carry):
        rr = pl.multiple_of(pi * L, L)
        dt2 = _softplus(g_ref[pl.ds(rr, L), :] + brow_ref[...])
        a2 = dt2 * a_row
        a2_t = a2.T
        acs_all = jnp.dot(tril, a2, precision=HI, preferred_element_type=F32)
        acs_t = jnp.dot(a2_t[G_DT:G_DT + 8, :], triu, precision=HI, preferred_element_type=F32)
        cb = []
        bmat = []
        cmat = []
        for g in range(SSM_GROUPS):
            bm = xact_ref[pl.ds(rr, L), 256 + g * 64:256 + (g + 1) * 64]
            cm = xact_ref[pl.ds(rr, L), 384 + g * 64:384 + (g + 1) * 64].astype(BF16)
            bmat.append(bm)
            cmat.append(cm)
            cb.append(lax.dot_general(cm, bm.astype(BF16), NT, preferred_element_type=F32))
        gated = []
        ssq = jnp.zeros((L, 1), F32)
        for h in range(HEADS):
            g = h // (HEADS // SSM_GROUPS)
            acs_col = acs_all[:, G_DT + h:G_DT + h + 1]
            acs_row = acs_t[h:h + 1, :]
            dt_col = dt2[:, G_DT + h:G_DT + h + 1]
            xh = xact_ref[pl.ds(rr, L), h * 64:(h + 1) * 64]
            xdt = (xh * dt_col).astype(BF16)
            st = st_ref[h]
            mmat = cb[g] * jnp.exp(jnp.where(causal, acs_col - acs_row, -jnp.inf))
            y = jnp.dot(mmat.astype(BF16), xdt, preferred_element_type=F32)
            y = y + jnp.dot(cmat[g], st.astype(BF16), preferred_element_type=F32) * jnp.exp(acs_col)
            y = y + xh * dskip[:, h * 64:(h + 1) * 64]
            acs_last = acs_col[L - 1:L, :]
            bdec = (bmat[g] * jnp.exp(acs_last - acs_col)).astype(BF16)
            st_ref[h] = jnp.exp(acs_last) * st + lax.dot_general(bdec, xdt, TN, preferred_element_type=F32)
            z = ssm_ref[pl.ds(rr, L), h * 64:(h + 1) * 64].astype(F32)
            yg = y * _silu(z)
            ssq = ssq + jnp.sum(yg * yg, axis=1, keepdims=True)
            gated.append(yg)
        scale = lax.rsqrt(ssq / (HEADS * SSM_P) + LN_EPS)
        for h in range(HEADS):
            y_ref[pl.ds(rr, L), h * 64:(h + 1) * 64] = (gated[h] * scale * nw[:, h * 64:(h + 1) * 64]).astype(BF16)
        return carry

    lax.fori_loop(0, CH // L, pair, 0)


def _ssd(ssm, gates, brow, alog, cw, cb, dskip, nw, B, S, CH=512):
    T = ssm.shape[0]
    nS = S // CH
    row = lambda b, j: (b * nS + j, 0)
    full = lambda b, j: (0, 0)
    return pl.pallas_call(
        functools.partial(_ssd_kernel, CH=CH),
        grid=(B, nS),
        in_specs=[pl.BlockSpec((CH, 768), row), pl.BlockSpec((CH, 128), row), pl.BlockSpec((1, 128), full),
                  pl.BlockSpec((1, 128), full), pl.BlockSpec((SSM_CONV, 512), full), pl.BlockSpec((1, 512), full),
                  pl.BlockSpec((1, 256), full), pl.BlockSpec((1, 256), full)],
        out_specs=pl.BlockSpec((CH, 256), row),
        out_shape=jax.ShapeDtypeStruct((T, 256), BF16),
        scratch_shapes=[pltpu.VMEM((CH + 8, 512), F32), pltpu.VMEM((CH, 512), F32),
                        pltpu.VMEM((HEADS, SSM_N, SSM_P), F32)],
        compiler_params=_params(2),
        name="ssd",
    )(ssm, gates, brow, alog, cw, cb, dskip, nw)


def _attn_kernel(at_ref, o_ref, lse_ref, *, N):
    W = AT_SPAN
    ri = lax.broadcasted_iota(jnp.int32, (W, W), 0)
    ci = lax.broadcasted_iota(jnp.int32, (W, W), 1)
    cur_ok = ri >= ci
    prev_ok = ci >= ri
    low_lanes = ci < AT_DH
    ones = jnp.ones((W, W), BF16)

    NB = 2 if (N // W) % 2 == 0 else 1

    def pair_attention(q2, kc, kp, vc, vp, pmask):
        outs, lses = [], []
        for first in (True, False):
            qh = jnp.where(low_lanes if first else jnp.logical_not(low_lanes), q2, jnp.zeros_like(q2))
            sc = jnp.where(cur_ok, lax.dot_general(qh, kc, NT, preferred_element_type=F32), -jnp.inf)
            sp = jnp.where(pmask, lax.dot_general(qh, kp, NT, preferred_element_type=F32), -jnp.inf)
            m = jnp.max(jnp.maximum(sc, sp), axis=1, keepdims=True)
            pc = jnp.exp(sc - m).astype(BF16)
            pp = jnp.exp(sp - m).astype(BF16)
            acc = jnp.dot(pc, vc, preferred_element_type=F32) + jnp.dot(pp, vp, preferred_element_type=F32)
            den = jnp.dot(pc, ones, preferred_element_type=F32) + jnp.dot(pp, ones, preferred_element_type=F32)
            outs.append(acc / den)
            lses.append(m + jnp.log(den[:, 0:1]))
        return jnp.where(low_lanes, outs[0], outs[1]).astype(BF16), jnp.where(low_lanes, lses[0], lses[1])

    def blk(i, carry):
        work = []
        for u in range(NB):
            n = i * NB + u
            r0 = pl.multiple_of(n * W, W)
            rp = pl.multiple_of(jnp.maximum(n - 1, 0) * W, W)
            for p in range(HEADS // 2):
                lanes = lambda base: slice(base + 128 * p, base + 128 * (p + 1))
                work.append((r0, p, prev_ok & (n > 0),
                             at_ref[pl.ds(r0, W), lanes(0)], at_ref[pl.ds(r0, W), lanes(256)],
                             at_ref[pl.ds(rp, W), lanes(256)], at_ref[pl.ds(r0, W), lanes(512)],
                             at_ref[pl.ds(rp, W), lanes(512)]))
        done = [(r0, p) + pair_attention(q2, kc, kp, vc, vp, pmask) for r0, p, pmask, q2, kc, kp, vc, vp in work]
        for r0, p, o2, lse2 in done:
            o_ref[pl.ds(r0, W), 128 * p:128 * (p + 1)] = o2
            lse_ref[pl.ds(r0, W), 128 * p:128 * (p + 1)] = lse2
        return carry

    assert (N // W) % NB == 0
    lax.fori_loop(0, N // (W * NB), blk, 0)


def _attn(atv, d, B, S):
    N = S // d
    o, lse = pl.pallas_call(
        functools.partial(_attn_kernel, N=N),
        grid=(B, d),
        in_specs=[pl.BlockSpec((None, N, 768), lambda b, r: (b, 0, r))],
        out_specs=[pl.BlockSpec((None, N, 256), lambda b, r: (b, 0, r)),
                   pl.BlockSpec((None, N, 256), lambda b, r: (b, 0, r))],
        out_shape=[jax.ShapeDtypeStruct((B, N, d * 256), BF16), jax.ShapeDtypeStruct((B, N, d * 256), F32)],
        compiler_params=_params(2),
        name=f"attn_d{d}",
    )(atv.reshape(B, N, d * 768))
    return o.reshape(B * N, d * 256), lse.reshape(B * N, d * 256)


def _gla_kernel(gla_ref, g_ref, w2_ref, b2_ref, nw_ref, y_ref, st_ref, *, CH):
    @pl.when(pl.program_id(1) == 0)
    def _init():
        st_ref[...] = jnp.zeros_like(st_ref)

    L = CHUNK
    SB = 16
    ri = lax.broadcasted_iota(jnp.int32, (L, L), 0)
    ci = lax.broadcasted_iota(jnp.int32, (L, L), 1)
    tril_blk = ((ri >= ci) & (ri // SB == ci // SB)).astype(F32)
    tl = lax.broadcasted_iota(jnp.int32, (L, 128), 0) % SB
    er = lax.broadcasted_iota(jnp.int32, (128, 256), 0) // GLA_DK
    ec = lax.broadcasted_iota(jnp.int32, (128, 256), 1) // GLA_DV
    head_expand = (er == ec).astype(BF16)
    sr = lax.broadcasted_iota(jnp.int32, (256, 128), 0) // GLA_DV
    sc = lax.broadcasted_iota(jnp.int32, (256, 128), 1) // GLA_DK
    st_mask = (sr == sc).astype(F32)
    nw = nw_ref[...]

    def chunk(ci_, carry):
        rr = pl.multiple_of(ci_ * L, L)
        lg = _log_sigmoid(jnp.dot(g_ref[pl.ds(rr, L), :], w2_ref[...], preferred_element_type=F32)
                          + b2_ref[...]) / GLA_TAU
        c = jnp.dot(tril_blk, lg, precision=HI, preferred_element_type=F32)
        q = gla_ref[pl.ds(rr, L), 0:128].astype(F32) * GLA_DK ** -0.5
        k = gla_ref[pl.ds(rr, L), 128:256].astype(F32)
        vb = gla_ref[pl.ds(rr, L), 256:512]
        v = vb.astype(F32)
        c4 = c.reshape(L // SB, SB, 128)
        k4 = k.reshape(L // SB, SB, 128)
        v4 = v.reshape(L // SB, SB, 256)

        def bcast(x4, j, width):
            return jnp.broadcast_to(x4[:, j:j + 1, :], (L // SB, SB, width)).reshape(L, width)

        o = jnp.zeros((L, 256), F32)
        for j in range(SB):
            dec = jnp.exp(jnp.minimum(c - bcast(c4, j, 128), 0.0))
            p = jnp.where(tl >= j, q * dec * bcast(k4, j, 128), 0.0)
            a = jnp.dot(p.astype(BF16), head_expand, preferred_element_type=F32)
            o = o + a * bcast(v4, j, 256)

        outs = []
        for i in range(L // SB):
            cblk = c[i * SB:(i + 1) * SB, :]
            clast = cblk[SB - 1:SB, :]
            st = st_ref[...]
            qe = (q[i * SB:(i + 1) * SB, :] * jnp.exp(cblk)).astype(BF16)
            outs.append(o[i * SB:(i + 1) * SB, :] + lax.dot_general(qe, st.astype(BF16), NT,
                                                                    preferred_element_type=F32))
            ke = (k[i * SB:(i + 1) * SB, :] * jnp.exp(clast - cblk)).astype(BF16)
            upd = lax.dot_general(vb[i * SB:(i + 1) * SB, :], ke, TN, preferred_element_type=F32)
            st_ref[...] = st * jnp.exp(clast) + st_mask * upd
        for i in range(L // SB):
            rg = gla_ref[pl.ds(pl.multiple_of(rr + i * SB, SB), SB), 512:768].astype(F32)
            for h in range(HEADS):
                oh = outs[i][:, h * 64:(h + 1) * 64]
                rms = lax.rsqrt(jnp.mean(oh * oh, axis=1, keepdims=True) + LN_EPS)
                y_ref[pl.ds(pl.multiple_of(rr + i * SB, SB), SB), h * 64:(h + 1) * 64] = (
                    oh * rms * nw[:, h * 64:(h + 1) * 64] * _silu(rg[:, h * 64:(h + 1) * 64])).astype(BF16)
        return carry

    lax.fori_loop(0, CH // L, chunk, 0, unroll=2)


def _gla(gla, gates, w2, b2, nw, B, S, CH=512):
    T = gla.shape[0]
    nS = S // CH
    row = lambda b, j: (b * nS + j, 0)
    full = lambda b, j: (0, 0)
    return pl.pallas_call(
        functools.partial(_gla_kernel, CH=CH),
        grid=(B, nS),
        in_specs=[pl.BlockSpec((CH, 768), row), pl.BlockSpec((CH, 128), row), pl.BlockSpec((128, 128), full),
                  pl.BlockSpec((1, 128), full), pl.BlockSpec((1, 256), full)],
        out_specs=pl.BlockSpec((CH, 256), row),
        out_shape=jax.ShapeDtypeStruct((T, 256), BF16),
        scratch_shapes=[pltpu.VMEM((HEADS * GLA_DV, HEADS * GLA_DK), F32)],
        compiler_params=_params(2),
        name="gla",
    )(gla, gates, w2, b2, nw)


def _out_proj_kernel(yml_ref, yssm_ref, o1_ref, o4_ref, o16_ref, l1_ref, l4_ref, l16_ref, ygla_ref, x_ref,
                     wo_ref, g_ref, b_ref, x1_ref, os_ref, ls_ref):
    tm = x_ref.shape[0]
    for i, (d, o_ref, l_ref) in enumerate(((DILATIONS[1], o4_ref, l4_ref), (DILATIONS[2], o16_ref, l16_ref))):
        for r in range(d):
            for j in range(2):
                cols = slice(256 * r + 128 * j, 256 * r + 128 * (j + 1))
                os_ref[i, j, pl.ds(r, tm // d, stride=d), :] = o_ref[:, cols].astype(F32)
                ls_ref[i, j, pl.ds(r, tm // d, stride=d), :] = l_ref[:, cols]
    yat = []
    for j in range(2):
        o1 = o1_ref[:, 128 * j:128 * (j + 1)].astype(F32)
        l1 = l1_ref[:, 128 * j:128 * (j + 1)]
        l4, l16 = ls_ref[0, j], ls_ref[1, j]
        mx = jnp.maximum(jnp.maximum(l1, l4), l16)
        e1, e4, e16 = jnp.exp(l1 - mx), jnp.exp(l4 - mx), jnp.exp(l16 - mx)
        yat.append(((e1 * o1 + e4 * os_ref[0, j] + e16 * os_ref[1, j]) / (e1 + e4 + e16)).astype(BF16))
    acc = jnp.dot(yml_ref[...], wo_ref[0:256, :], preferred_element_type=F32)
    acc = acc + jnp.dot(yssm_ref[...], wo_ref[256:512, :], preferred_element_type=F32)
    acc = acc + jnp.dot(yat[0], wo_ref[512:640, :], preferred_element_type=F32)
    acc = acc + jnp.dot(yat[1], wo_ref[640:768, :], preferred_element_type=F32)
    acc = acc + jnp.dot(ygla_ref[...], wo_ref[768:1024, :], preferred_element_type=F32)
    x1_ref[...] = _layer_norm(ALPHA * x_ref[...] + acc, g_ref[...], b_ref[...])


def _out_proj(yml, yssm, branches, ygla, x, wo, g, b, tm=512):
    T, D = x.shape
    row = lambda i: (i, 0)
    full = lambda i: (0, 0)
    small = pl.BlockSpec((tm, 256), row)
    dil = [pl.BlockSpec((tm // d, 256 * d), row) for d in DILATIONS]
    return pl.pallas_call(
        _out_proj_kernel,
        grid=(T // tm,),
        in_specs=[small, small] + dil + dil + [small, pl.BlockSpec((tm, D), row), pl.BlockSpec(wo.shape, full),
                                               pl.BlockSpec((1, D), full), pl.BlockSpec((1, D), full)],
        out_specs=pl.BlockSpec((tm, D), row),
        out_shape=jax.ShapeDtypeStruct((T, D), F32),
        scratch_shapes=[pltpu.VMEM((2, 2, tm, 128), F32), pltpu.VMEM((2, 2, tm, 128), F32)],
        compiler_params=_params(1),
        name="out_proj_ln",
    )(yml, yssm, *[o for o, _ in branches], *[l for _, l in branches], ygla, x, wo, g, b)


def _router_kernel(x_ref, wr_ref, rb_ref, row_ref, col_ref, cnt_ref, rt_ref):
    tm = x_ref.shape[0]
    gsz = N_EXPERTS // N_GROUPS
    logits = lax.dot_general(wr_ref[...], x_ref[...], NT, precision=HI, preferred_element_type=F32)
    scores = jax.nn.sigmoid(logits)
    sel = scores + rb_ref[...]
    eidx = lax.broadcasted_iota(jnp.int32, (gsz, tm), 0)
    big = jnp.int32(1 << 20)
    neg = -jnp.inf

    sel_g = [sel[g * gsz:(g + 1) * gsz, :] for g in range(N_GROUPS)]
    idx_g = [eidx + g * gsz for g in range(N_GROUPS)]
    gscore = []
    for g in range(N_GROUPS):
        v = sel_g[g]
        m1 = jnp.max(v, axis=0, keepdims=True)
        i1 = jnp.min(jnp.where(v == m1, idx_g[g], big), axis=0, keepdims=True)
        m2 = jnp.max(jnp.where(idx_g[g] == i1, neg, v), axis=0, keepdims=True)
        gscore.append(m1 + m2)
    gkeep = [jnp.zeros((1, tm), jnp.bool_) for _ in range(N_GROUPS)]
    for _ in range(TOPK_GROUPS):
        m = functools.reduce(jnp.maximum, gscore)
        gi = functools.reduce(jnp.minimum, [jnp.where(gscore[g] == m, g, big) for g in range(N_GROUPS)])
        for g in range(N_GROUPS):
            hit = gi == g
            gkeep[g] = gkeep[g] | hit
            gscore[g] = jnp.where(hit, neg, gscore[g])
    cand = [jnp.where(gkeep[g], sel_g[g], neg) for g in range(N_GROUPS)]
    chosen = [jnp.zeros((gsz, tm), jnp.bool_) for _ in range(N_GROUPS)]
    picks = []
    for _ in range(TOP_K):
        m = functools.reduce(jnp.maximum, [jnp.max(c, axis=0, keepdims=True) for c in cand])
        ei = functools.reduce(jnp.minimum, [jnp.min(jnp.where(cand[g] == m, idx_g[g], big), axis=0, keepdims=True)
                                            for g in range(N_GROUPS)])
        picks.append(ei)
        for g in range(N_GROUPS):
            hit = idx_g[g] == ei
            chosen[g] = chosen[g] | hit
            cand[g] = jnp.where(hit, neg, cand[g])
    picked = [jnp.where(chosen[g], scores[g * gsz:(g + 1) * gsz, :], 0.0) for g in range(N_GROUPS)]
    tot = functools.reduce(jnp.add, [jnp.sum(p, axis=0, keepdims=True) for p in picked])
    gates = [p / tot * ROUTED_SCALE for p in picked]

    chosen_b = jnp.concatenate([c.astype(F32) for c in chosen] + [jnp.zeros((N_EXPERTS, tm), F32)], axis=0).astype(BF16)
    ti = lax.broadcasted_iota(jnp.int32, (tm, tm), 0)
    tj = lax.broadcasted_iota(jnp.int32, (tm, tm), 1)
    before = ((ti < tj) & (ti // TOK_TILE == tj // TOK_TILE)).astype(BF16)
    rank = jnp.dot(chosen_b[0:N_EXPERTS, :], before, preferred_element_type=F32)
    lower = (lax.broadcasted_iota(jnp.int32, (N_EXPERTS, N_EXPERTS), 0)
             > lax.broadcasted_iota(jnp.int32, (N_EXPERTS, N_EXPERTS), 1)).astype(F32)
    tile_of_lane = lax.broadcasted_iota(jnp.int32, (1, tm), 1) // TOK_TILE
    posmat = rank
    for t in range(tm // TOK_TILE):
        in_tile = chosen_b[:, t * TOK_TILE:(t + 1) * TOK_TILE]
        cnt = jnp.dot(in_tile[0:N_EXPERTS, :], jnp.ones((TOK_TILE, 128), BF16), preferred_element_type=F32)
        padded = jnp.maximum(jnp.floor((cnt + (ROW_PAD - 1)) * (1.0 / ROW_PAD)) * ROW_PAD, float(ROW_PAD))
        gstart = jnp.dot(lower, padded, precision=HI, preferred_element_type=F32)
        posmat = posmat + jnp.where(tile_of_lane == t, gstart[:, 0:1], 0.0)
        cnt_ref[8 * t:8 * (t + 1), :] = lax.dot_general(jnp.ones((8, TOK_TILE), BF16), in_tile, NT,
                                                        preferred_element_type=F32)
    rt_ref[...] = jnp.zeros_like(rt_ref)
    for k in range(TOP_K):
        pos_k = jnp.zeros((1, tm), F32)
        gate_k = jnp.zeros((1, tm), F32)
        for g in range(N_GROUPS):
            hit = idx_g[g] == picks[k]
            pos_k = pos_k + jnp.sum(jnp.where(hit, posmat[g * gsz:(g + 1) * gsz, :], 0.0), axis=0, keepdims=True)
            gate_k = gate_k + jnp.sum(jnp.where(hit, gates[g], 0.0), axis=0, keepdims=True)
        rt_ref[k:k + 1, :] = pos_k
        rt_ref[TOP_K + k:TOP_K + k + 1, :] = gate_k
    row_ref[...] = rt_ref[0:2 * TOP_K, :]
    col_ref[...] = rt_ref[...].T


def _router(x1, wr_t, rb):
    T, D = x1.shape
    tm = ROUTER_TILE
    sub = tm // TOK_TILE
    return pl.pallas_call(
        _router_kernel,
        grid=(T // tm,),
        in_specs=[pl.BlockSpec((tm, D), lambda i: (i, 0)), pl.BlockSpec(wr_t.shape, lambda i: (0, 0)),
                  pl.BlockSpec(rb.shape, lambda i: (0, 0))],
        out_specs=[pl.BlockSpec((2 * TOP_K, tm), lambda i: (0, i)), pl.BlockSpec((tm, 128), lambda i: (i, 0)),
                   pl.BlockSpec((8 * sub, 128), lambda i: (i, 0))],
        out_shape=[jax.ShapeDtypeStruct((2 * TOP_K, T), F32), jax.ShapeDtypeStruct((T, 128), F32),
                   jax.ShapeDtypeStruct((T // TOK_TILE * 8, 128), F32)],
        scratch_shapes=[pltpu.VMEM((128, tm), F32)],
        compiler_params=_params(1),
        name="router",
    )(x1, wr_t, rb)


M_LSTART, M_ROWS, M_GOFF = 0, N_EXPERTS, 2 * N_EXPERTS
M_USED, M_USED_PREV = 3 * N_EXPERTS, 3 * N_EXPERTS + 1
M_COLS = 4 * N_EXPERTS


def _ceil_div(x, n):
    assert n & (n - 1) == 0
    return lax.shift_right_logical(x + (n - 1), n.bit_length() - 1)


def _seg_copy(meta_ref, e, loc_ref, glob_ref, sem, to_global):
    n = pl.multiple_of(meta_ref[0, 0, M_ROWS + e] * ROW_PAD, ROW_PAD)
    ls = pl.multiple_of(meta_ref[0, 0, M_LSTART + e] * ROW_PAD, ROW_PAD)
    go = pl.multiple_of(meta_ref[0, 0, M_GOFF + e] * ROW_PAD, ROW_PAD)
    loc, glob = loc_ref.at[pl.ds(ls, n)], glob_ref.at[pl.ds(go, n)]
    return pltpu.make_async_copy(loc, glob, sem) if to_global else pltpu.make_async_copy(glob, loc, sem)


def _dispatch_kernel(meta_ref, tail_ref, x_ref, row_ref, xg_ref, loc_ref, zero_ref, sems):
    s = pl.program_id(0)
    last = pl.num_programs(0) - 1
    slot = s % 2
    tm = x_ref.shape[0]

    @pl.when(s == 0)
    def _fill_tails():
        zero_ref[...] = jnp.zeros_like(zero_ref)

        def start(e, c):
            n = pl.multiple_of(tail_ref[1, e] * ROW_PAD, ROW_PAD)
            go = pl.multiple_of(tail_ref[0, e] * ROW_PAD, ROW_PAD)

            @pl.when(n > 0)
            def _():
                pltpu.make_async_copy(zero_ref.at[pl.ds(0, n)], xg_ref.at[pl.ds(go, n)], sems.at[2]).start()
            return c
        lax.fori_loop(0, N_EXPERTS, start, 0)

    RC = 256
    xb = x_ref[...].astype(BF16)
    pos = row_ref[0:TOP_K, :]
    hi = jnp.floor(pos * (1.0 / RC))
    lo = pos - hi * RC
    rows = lax.broadcasted_iota(jnp.int32, (RC, tm), 0).astype(F32)
    lo_hit = [jnp.where(lo[k:k + 1, :] == rows, 1.0, 0.0).astype(BF16) for k in range(TOP_K)]
    used = meta_ref[0, 0, M_USED]

    def chunk(c, carry):
        cf = c.astype(F32)
        onehot = lo_hit[0] * jnp.where(hi[0:1, :] == cf, 1.0, 0.0).astype(BF16)
        for k in range(1, TOP_K):
            onehot = onehot + lo_hit[k] * jnp.where(hi[k:k + 1, :] == cf, 1.0, 0.0).astype(BF16)
        loc_ref[slot, pl.ds(pl.multiple_of(c * RC, RC), RC), :] = jnp.dot(
            onehot, xb, preferred_element_type=F32).astype(BF16)
        return carry
    lax.fori_loop(0, _ceil_div(used, RC // ROW_PAD), chunk, 0)

    def start(e, c):
        _seg_copy(meta_ref, e, loc_ref.at[slot], xg_ref, sems.at[slot], True).start()
        return c
    lax.fori_loop(0, N_EXPERTS, start, 0)

    def wait_all(total_col, buf):
        n = pl.multiple_of(meta_ref[0, 0, total_col] * ROW_PAD, ROW_PAD)

        @pl.when(n > 0)
        def _():
            pltpu.make_async_copy(loc_ref.at[buf, pl.ds(0, n)], xg_ref.at[pl.ds(0, n)], sems.at[buf]).wait()

    @pl.when(s > 0)
    def _wait_prev():
        wait_all(M_USED_PREV, 1 - slot)

    @pl.when(s == last)
    def _wait_own():
        wait_all(M_USED, slot)

    @pl.when(s == 0)
    def _wait_tails():
        def wait(e, c):
            n = pl.multiple_of(tail_ref[1, e] * ROW_PAD, ROW_PAD)
            go = pl.multiple_of(tail_ref[0, e] * ROW_PAD, ROW_PAD)

            @pl.when(n > 0)
            def _():
                pltpu.make_async_copy(zero_ref.at[pl.ds(0, n)], xg_ref.at[pl.ds(go, n)], sems.at[2]).wait()
            return c
        lax.fori_loop(0, N_EXPERTS, wait, 0)


def _dispatch(meta, tail, x1, rowform, p_rows):
    T, D = x1.shape
    tm = TOK_TILE
    return pl.pallas_call(
        _dispatch_kernel,
        grid=(T // tm,),
        in_specs=[pl.BlockSpec((1, 1, M_COLS), lambda i: (i, 0, 0), memory_space=pltpu.SMEM),
                  pl.BlockSpec(memory_space=pltpu.SMEM),
                  pl.BlockSpec((tm, D), lambda i: (i, 0)), pl.BlockSpec((2 * TOP_K, tm), lambda i: (0, i))],
        out_specs=pl.BlockSpec(memory_space=pl.ANY),
        out_shape=jax.ShapeDtypeStruct((p_rows, D), BF16),
        scratch_shapes=[pltpu.VMEM((2, LOC_ROWS, D), BF16), pltpu.VMEM((FFN_BLK, D), BF16),
                        pltpu.SemaphoreType.DMA((3,))],
        compiler_params=_params(1),
        name="moe_dispatch",
    )(meta, tail, x1, rowform)


def _ffn_kernel(bexp_ref, nused_ref, x_ref, wg_ref, wu_ref, wd_ref, y_ref, wgb_ref, wub_ref, wdb_ref):
    i = pl.program_id(0)

    @pl.when(i < nused_ref[0])
    def _():
        @pl.when((i == 0) | (bexp_ref[i] != bexp_ref[jnp.maximum(i - 1, 0)]))
        def _new_expert():
            wgb_ref[...] = wg_ref[...].astype(BF16)
            wub_ref[...] = wu_ref[...].astype(BF16)
            wdb_ref[...] = wd_ref[...].astype(BF16)

        xb = x_ref[...]
        a = jnp.dot(xb, wgb_ref[...], preferred_element_type=F32)
        u = jnp.dot(xb, wub_ref[...], preferred_element_type=F32)
        y_ref[...] = jnp.dot((_silu(a) * u).astype(BF16), wdb_ref[...], preferred_element_type=F32).astype(BF16)


def _ffn(blk_exp, nused, xg, wg, wu, wd, layer):
    P, D = xg.shape
    F = wg.shape[3]
    blk = lambda i, be, nu: (jnp.maximum(jnp.minimum(i, nu[0] - 1), 0), 0)
    wsel = lambda i, be, nu: (layer, be[jnp.maximum(jnp.minimum(i, nu[0] - 1), 0)], 0, 0)
    return pl.pallas_call(
        _ffn_kernel,
        grid_spec=pltpu.PrefetchScalarGridSpec(
            num_scalar_prefetch=2,
            grid=(P // FFN_BLK,),
            in_specs=[pl.BlockSpec((FFN_BLK, D), blk), pl.BlockSpec((None, None, D, F), wsel),
                      pl.BlockSpec((None, None, D, F), wsel), pl.BlockSpec((None, None, F, D), wsel)],
            out_specs=pl.BlockSpec((FFN_BLK, D), blk),
            scratch_shapes=[pltpu.VMEM((D, F), BF16), pltpu.VMEM((D, F), BF16), pltpu.VMEM((F, D), BF16)]),
        out_shape=jax.ShapeDtypeStruct((P, D), BF16),
        compiler_params=_params(1),
        name="moe_ffn",
    )(blk_exp, nused, xg, wg, wu, wd)


def _combine_kernel(meta_ref, next_ref, x_ref, col_ref, yg_ref, sg_ref, su_ref, sd_ref, g_ref, b_ref, x2_ref,
                    loc_ref, z_ref, sems):
    s = pl.program_id(0)
    last = pl.num_programs(0) - 1
    slot = s % 2
    tm = x_ref.shape[0]
    CW = 512
    cw_u = CW // ROW_PAD

    def used_rows(m_ref):
        return m_ref[0, 0, M_USED]

    def fetch(m_ref, buf):
        def start(e, c):
            _seg_copy(m_ref, e, loc_ref.at[buf], yg_ref, sems.at[buf], False).start()
            return c
        lax.fori_loop(0, N_EXPERTS, start, 0)
        used = used_rows(m_ref)

        def clear(c, carry):
            loc_ref[buf, pl.ds(pl.multiple_of(c * ROW_PAD, ROW_PAD), ROW_PAD), :] = jnp.zeros(
                (ROW_PAD, loc_ref.shape[2]), BF16)
            return carry
        lax.fori_loop(used, _ceil_div(used, cw_u) * cw_u, clear, 0)

    @pl.when(s == 0)
    def _first():
        fetch(meta_ref, 0)

    @pl.when(s < last)
    def _prefetch():
        fetch(next_ref, 1 - slot)

    x1 = x_ref[...]
    xb = x1.astype(BF16)
    hid = _silu(jnp.dot(xb, sg_ref[...], preferred_element_type=F32)) * jnp.dot(xb, su_ref[...],
                                                                                preferred_element_type=F32)
    z_ref[...] = ALPHA * x1 + jnp.dot(hid.astype(BF16), sd_ref[...], preferred_element_type=F32)

    pos = col_ref[:, 0:TOP_K]
    gate = col_ref[:, TOP_K:2 * TOP_K]
    hi = jnp.floor(pos * (1.0 / CW))
    lo = pos - hi * CW
    lanes = lax.broadcasted_iota(jnp.int32, (tm, CW), 1).astype(F32)
    lo_hit = [jnp.where(lo[:, k:k + 1] == lanes, 1.0, 0.0).astype(BF16) for k in range(TOP_K)]
    n_own = pl.multiple_of(used_rows(meta_ref) * ROW_PAD, ROW_PAD)

    @pl.when(n_own > 0)
    def _wait_own():
        pltpu.make_async_copy(yg_ref.at[pl.ds(0, n_own)], loc_ref.at[slot, pl.ds(0, n_own)], sems.at[slot]).wait()

    def chunk(c, carry):
        cf = c.astype(F32)
        wmat = lo_hit[0] * jnp.where(hi[:, 0:1] == cf, gate[:, 0:1], 0.0).astype(BF16)
        for k in range(1, TOP_K):
            wmat = wmat + lo_hit[k] * jnp.where(hi[:, k:k + 1] == cf, gate[:, k:k + 1], 0.0).astype(BF16)
        rows = loc_ref[slot, pl.ds(pl.multiple_of(c * CW, CW), CW), :]
        z_ref[...] += jnp.dot(wmat, rows, preferred_element_type=F32)
        return carry
    lax.fori_loop(0, _ceil_div(used_rows(meta_ref), cw_u), chunk, 0)
    x2_ref[...] = _layer_norm(z_ref[...], g_ref[...], b_ref[...])


def _combine(meta, x1, colform, yg, sg, su, sd, g, b):
    T, D = x1.shape
    tm = TOK_TILE
    nst = T // tm
    full = lambda i: (0, 0)
    mspec = lambda f: pl.BlockSpec((1, 1, M_COLS), f, memory_space=pltpu.SMEM)
    return pl.pallas_call(
        _combine_kernel,
        grid=(nst,),
        in_specs=[mspec(lambda i: (i, 0, 0)), mspec(lambda i: (jnp.minimum(i + 1, nst - 1), 0, 0)),
                  pl.BlockSpec((tm, D), lambda i: (i, 0)), pl.BlockSpec((tm, 128), lambda i: (i, 0)),
                  pl.BlockSpec(memory_space=pl.ANY),
                  pl.BlockSpec(sg.shape, full), pl.BlockSpec(su.shape, full), pl.BlockSpec(sd.shape, full),
                  pl.BlockSpec((1, D), full), pl.BlockSpec((1, D), full)],
        out_specs=pl.BlockSpec((tm, D), lambda i: (i, 0)),
        out_shape=jax.ShapeDtypeStruct((T, D), F32),
        scratch_shapes=[pltpu.VMEM((2, LOC_ROWS, D), BF16), pltpu.VMEM((tm, D), F32),
                        pltpu.SemaphoreType.DMA((2,))],
        compiler_params=_params(1),
        name="moe_combine_ln",
    )(meta, meta, x1, colform, yg, sg, su, sd, g, b)


def _moe_plan(cnt_out, T):
    nst = T // TOK_TILE
    blk_u = FFN_BLK // ROW_PAD
    cnt = cnt_out.reshape(nst, 8, 128)[:, 0, :N_EXPERTS].astype(jnp.int32)
    rows = jnp.maximum((cnt + ROW_PAD - 1) // ROW_PAD, 1)
    lstart = jnp.cumsum(rows, axis=1) - rows
    tot = rows.sum(axis=0)
    tot_pad = (tot + blk_u - 1) // blk_u * blk_u
    eend = jnp.cumsum(tot_pad)
    ebase = eend - tot_pad
    goff = ebase[None, :] + jnp.cumsum(rows, axis=0) - rows
    used = rows.sum(axis=1, keepdims=True)
    used_prev = jnp.concatenate([jnp.zeros((1, 1), jnp.int32), used[:-1]], axis=0)
    fill = jnp.zeros((nst, M_COLS - M_USED_PREV - 1), jnp.int32)
    meta = jnp.concatenate([lstart, rows, goff, used, used_prev, fill], axis=1).reshape(nst, 1, M_COLS)
    tail = jnp.stack([ebase + tot, tot_pad - tot])
    nblk = _moe_rows(T) // FFN_BLK
    nused = (eend[-1] // blk_u).reshape(1)
    first_row = jnp.arange(nblk, dtype=jnp.int32) * blk_u
    blk_exp = jnp.minimum(jnp.sum(first_row[:, None] >= eend[None, :], axis=1), N_EXPERTS - 1).astype(jnp.int32)
    return meta, tail, blk_exp, nused


def _moe_rows(T):
    worst = T * TOP_K + (T // TOK_TILE) * N_EXPERTS * ROW_PAD + N_EXPERTS * (FFN_BLK - ROW_PAD)
    return -(-worst // FFN_BLK) * FFN_BLK


def _rope_tables(S):
    half = AT_DH // 2
    lane = jnp.arange(128)
    inv = ROPE_THETA ** (-(lane % half).astype(F32) / half)
    ang = jnp.arange(S, dtype=F32)[:, None] * inv[None, :]
    sign = jnp.where((lane % AT_DH) < half, -1.0, 1.0).astype(F32)
    return jnp.cos(ang), jnp.sin(ang) * sign[None, :]


def _pad_cols(a, width):
    return jnp.pad(a, ((0, 0), (0, width - a.shape[1])))


def kernel(x, w_in, ml_i_bias, ml_f_bias, ml_norm_w, ssm_conv_w, ssm_conv_b, ssm_dt_bias, ssm_a_log, ssm_d, ssm_norm_w, gla_gate_w2, gla_gate_b, gla_norm_w, w_out, ln1_g, ln1_b, router_w, router_bias, exp_w_gate, exp_w_up, exp_w_down, sh_w_gate, sh_w_up, sh_w_down, ln2_g, ln2_b):
    B, S, D = x.shape
    T = B * S
    depth = w_in.shape[0]
    assert D == 1024 and S % (AT_SPAN * max(DILATIONS)) == 0 and T % 1024 == 0
    cos, sin = _rope_tables(S)
    xf = x.reshape(T, D)
    for l in range(depth):
        wm, wgt = _regroup_weights(w_in, l)
        brow = _pad_cols(jnp.concatenate([ml_i_bias[l], ml_f_bias[l], ssm_dt_bias[l]])[None, :], 128).astype(F32)
        alog = jnp.pad(ssm_a_log[l].astype(F32), (G_DT, 128 - G_DT - HEADS))[None, :]
        w2 = jnp.pad(gla_gate_w2[l].astype(F32), ((G_GA, 128 - G_GA - GLA_RANK), (0, 0)))

        ml, ssm, at1, at4, at16, gla, gates = _in_proj(xf, wm, wgt, cos, sin, S)
        y_ml = _mlstm(ml, gates, brow, ml_norm_w[l][None, :].astype(F32), B, S)
        y_ssm = _ssd(ssm, gates, brow, alog, ssm_conv_w[l].astype(F32), ssm_conv_b[l][None, :].astype(F32),
                     jnp.repeat(ssm_d[l].astype(F32), SSM_P)[None, :], ssm_norm_w[l][None, :].astype(F32), B, S)
        branches = [_attn(atv, d, B, S) for atv, d in zip((at1, at4, at16), DILATIONS)]
        y_gla = _gla(gla, gates, w2, gla_gate_b[l][None, :].astype(F32), gla_norm_w[l][None, :].astype(F32), B, S)
        x1 = _out_proj(y_ml, y_ssm, branches, y_gla, xf,
                       w_out[l].astype(BF16), ln1_g[l][None, :].astype(F32), ln1_b[l][None, :].astype(F32))
        rowform, colform, cnt = _router(x1, router_w[l].T.astype(F32), router_bias[l][:, None].astype(F32))
        meta, tail, blk_exp, nused = _moe_plan(cnt, T)
        xg = _dispatch(meta, tail, x1, rowform, _moe_rows(T))
        yg = _ffn(blk_exp, nused, xg, exp_w_gate, exp_w_up, exp_w_down, l)
        xf = _combine(meta, x1, colform, yg, sh_w_gate[l].astype(BF16), sh_w_up[l].astype(BF16),
                      sh_w_down[l].astype(BF16), ln2_g[l][None, :].astype(F32), ln2_b[l][None, :].astype(F32))
    return xf.reshape(B, S, D)
```

```python
import functools

import jax
import jax.numpy as jnp
from jax import lax
from jax.experimental import pallas as pl
from jax.experimental.pallas import tpu as pltpu

F32 = jnp.float32
BF16 = jnp.bfloat16
HI = lax.Precision.HIGHEST
NT = (((1,), (1,)), ((), ()))
TN = (((0,), (0,)), ((), ()))

DEPTH = 4
HEADS = 4
ML_DH = 64
SSM_P = 64
SSM_N = 64
SSM_GROUPS = 2
SSM_CONV = 4
AT_DH = 64
DILATIONS = (1, 4, 16)
AT_SPAN = 128
ROPE_THETA = 10000.0
GLA_DK = 32
GLA_DV = 64
GLA_RANK = 16
GLA_TAU = 16.0
N_EXPERTS = 64
TOP_K = 8
N_GROUPS = 8
TOPK_GROUPS = 4
ROUTED_SCALE = 2.5
ALPHA = (2 * DEPTH) ** 0.25
LN_EPS = 1e-5
CHUNK = 64
TOK_TILE = 256
ROUTER_TILE = 512
ROW_PAD = 16
FFN_BLK = 1024
LOC_ROWS = -(-(TOK_TILE * TOP_K + N_EXPERTS * ROW_PAD) // 512) * 512

G_MI, G_MF, G_DT, G_GA = 0, 4, 8, 12

VMEM_LIMIT = 48 * 1024 * 1024


def _log_sigmoid(x):
    return jnp.minimum(x, 0.0) - jnp.log(1.0 + jnp.exp(-jnp.abs(x)))


def _softplus(x):
    return jnp.maximum(x, 0.0) + jnp.log(1.0 + jnp.exp(-jnp.abs(x)))


def _silu(x):
    return x * jax.nn.sigmoid(x)


def _layer_norm(z, g, b):
    mu = jnp.mean(z, axis=-1, keepdims=True)
    zc = z - mu
    var = jnp.mean(zc * zc, axis=-1, keepdims=True)
    return zc * lax.rsqrt(var + LN_EPS) * g + b


def _params(n_axes):
    return pltpu.CompilerParams(dimension_semantics=("arbitrary",) * n_axes, vmem_limit_bytes=VMEM_LIMIT)


W_MI, W_SZ, W_SDT, W_AQ, W_GQ, W_GA = 1024, 1032, 1800, 1804, 2572, 3340
W_MAIN = 3328


def _regroup_kernel(w_ref, wm_ref, wg_ref):
    RB = 256
    for c in range(0, w_ref.shape[0], RB):
        rows = slice(c, c + RB)
        wm_ref[rows, 0:1024] = w_ref[rows, 0:W_MI].astype(BF16)
        wm_ref[rows, 1024:1792] = w_ref[rows, W_SZ:W_SDT].astype(BF16)
        wm_ref[rows, 1792:2560] = w_ref[rows, W_AQ:W_GQ].astype(BF16)
        wm_ref[rows, 2560:W_MAIN] = w_ref[rows, W_GQ:W_GA].astype(BF16)
        wg_ref[rows, :] = jnp.zeros((RB, 128), BF16)
        wg_ref[rows, G_MI:G_DT] = w_ref[rows, W_MI:W_SZ].astype(BF16)
        wg_ref[rows, G_DT:G_GA] = w_ref[rows, W_SDT:W_AQ].astype(BF16)
        wg_ref[rows, G_GA:G_GA + GLA_RANK] = w_ref[rows, W_GA:W_GA + GLA_RANK].astype(BF16)


def _regroup_weights(w_in, layer):
    _, D, N = w_in.shape
    full = lambda i: (0, 0)
    return pl.pallas_call(
        _regroup_kernel,
        grid=(1,),
        in_specs=[pl.BlockSpec((None, D, N), lambda i: (layer, 0, 0))],
        out_specs=[pl.BlockSpec((D, W_MAIN), full), pl.BlockSpec((D, 128), full)],
        out_shape=[jax.ShapeDtypeStruct((D, W_MAIN), BF16), jax.ShapeDtypeStruct((D, 128), BF16)],
        compiler_params=_params(1),
        name="regroup_w_in",
    )(w_in)


def _in_proj_kernel(x_ref, wm_ref, wg_ref, cos_ref, sin_ref, ml_ref, ssm_ref, at1_ref, at4_ref, at16_ref, gla_ref,
                    g_ref, atf_ref):
    at_refs = (at1_ref, at4_ref, at16_ref)
    xb = x_ref[...].astype(BF16)

    def mm(lo):
        return jnp.dot(xb, wm_ref[:, lo:lo + 256], preferred_element_type=F32)

    for c in range(4):
        a = mm(256 * c)
        if c == 1:
            a = a * ML_DH ** -0.5
        ml_ref[:, 256 * c:256 * (c + 1)] = a.astype(BF16)
    for c in range(3):
        ssm_ref[:, 256 * c:256 * (c + 1)] = mm(1024 + 256 * c).astype(BF16)

    cos = cos_ref[...]
    sin = sin_ref[...]
    lane = lax.broadcasted_iota(jnp.int32, cos.shape, 1)
    first_half = (lane % AT_DH) < AT_DH // 2

    def rope(a):
        rot = jnp.where(first_half, pltpu.roll(a, 128 - AT_DH // 2, 1), pltpu.roll(a, AT_DH // 2, 1))
        return a * cos + rot * sin

    tm = x_ref.shape[0]
    for c in range(3):
        a = mm(1792 + 256 * c)
        for hh in range(2):
            tile = a[:, 128 * hh:128 * (hh + 1)]
            if c < 2:
                tile = rope(tile) * (AT_DH ** -0.5 if c == 0 else 1.0)
            atf_ref[2 * c + hh] = tile
    for d, ref in zip(DILATIONS, at_refs):
        for r in range(d):
            for j in range(6):
                rows = atf_ref[j] if d == 1 else atf_ref[j, pl.ds(r, tm // d, stride=d), :]
                ref[:, 768 * r + 128 * j:768 * r + 128 * (j + 1)] = rows.astype(BF16)
    for c in range(3):
        gla_ref[:, 256 * c:256 * (c + 1)] = mm(2560 + 256 * c).astype(BF16)
    g_ref[...] = jnp.dot(xb, wg_ref[...], preferred_element_type=F32)


def _in_proj(x, wm, wg, cos, sin, S, tm=512):
    T, D = x.shape
    nS = S // tm
    row = lambda i: (i, 0)
    full = lambda i: (0, 0)
    return pl.pallas_call(
        _in_proj_kernel,
        grid=(T // tm,),
        in_specs=[pl.BlockSpec((tm, D), row), pl.BlockSpec(wm.shape, full), pl.BlockSpec(wg.shape, full),
                  pl.BlockSpec((tm, 128), lambda i: (i % nS, 0)), pl.BlockSpec((tm, 128), lambda i: (i % nS, 0))],
        out_specs=[pl.BlockSpec((tm, 1024), row), pl.BlockSpec((tm, 768), row)]
        + [pl.BlockSpec((tm // d, 768 * d), row) for d in DILATIONS]
        + [pl.BlockSpec((tm, 768), row), pl.BlockSpec((tm, 128), row)],
        out_shape=[jax.ShapeDtypeStruct((T, 1024), BF16), jax.ShapeDtypeStruct((T, 768), BF16)]
        + [jax.ShapeDtypeStruct((T // d, 768 * d), BF16) for d in DILATIONS]
        + [jax.ShapeDtypeStruct((T, 768), BF16), jax.ShapeDtypeStruct((T, 128), F32)],
        scratch_shapes=[pltpu.VMEM((6, tm, 128), F32)],
        compiler_params=_params(1),
        name="in_proj",
    )(x, wm, wg, cos, sin)


def _tri_consts(L):
    ri = lax.broadcasted_iota(jnp.int32, (L, L), 0)
    ci = lax.broadcasted_iota(jnp.int32, (L, L), 1)
    causal = ri >= ci
    return causal, causal.astype(F32), (ri <= ci).astype(F32)


def _mlstm_kernel(ml_ref, g_ref, brow_ref, nw_ref, y_ref, c_ref, m_ref, *, CH):
    @pl.when(pl.program_id(1) == 0)
    def _init():
        c_ref[...] = jnp.zeros_like(c_ref)
        m_ref[...] = jnp.full_like(m_ref, -jnp.inf)

    L = 2 * CHUNK
    causal, tril, triu = _tri_consts(L)
    lane = lax.broadcasted_iota(jnp.int32, (L, 128), 1)
    is_f = (lane >= G_MF) & (lane < G_MF + HEADS)
    ones_v = jnp.ones((L, ML_DH), BF16)
    nw = nw_ref[...]

    def pair(pi, carry):
        rr = pl.multiple_of(pi * L, L)
        g2 = g_ref[pl.ds(rr, L), :] + brow_ref[...]
        vc = jnp.where(is_f, _log_sigmoid(g2), g2)
        vtc = vc.T[0:8, :]
        b_all = jnp.dot(tril, vc, precision=HI, preferred_element_type=F32)
        bt_all = jnp.dot(vtc, triu, precision=HI, preferred_element_type=F32)
        for h in range(HEADS):
            q = ml_ref[pl.ds(rr, L), h * 64:(h + 1) * 64]
            k = ml_ref[pl.ds(rr, L), 256 + h * 64:256 + (h + 1) * 64]
            v = ml_ref[pl.ds(rr, L), 512 + h * 64:512 + (h + 1) * 64]
            og = ml_ref[pl.ds(rr, L), 768 + h * 64:768 + (h + 1) * 64].astype(F32)
            b_col = b_all[:, G_MF + h:G_MF + h + 1]
            li_col = vc[:, G_MI + h:G_MI + h + 1]
            b_row = bt_all[G_MF + h:G_MF + h + 1, :]
            li_row = vtc[G_MI + h:G_MI + h + 1, :]
            m_prev = m_ref[h][0:1, 0:1]
            cst = c_ref[h]

            log_d = jnp.where(causal, b_col - b_row + li_row, -jnp.inf)
            log_inter = b_col + m_prev
            m_t = jnp.maximum(log_inter, jnp.max(log_d, axis=1, keepdims=True))
            w_inter = jnp.exp(log_inter - m_t)
            s = lax.dot_general(q, k, NT, preferred_element_type=F32) * jnp.exp(log_d - m_t)
            qc = jnp.dot(q, cst.astype(BF16), preferred_element_type=F32)
            num = jnp.dot(s.astype(BF16), v, preferred_element_type=F32) + w_inter * qc[:, 0:64]
            den = jnp.sum(s, axis=1, keepdims=True) + w_inter * qc[:, 64:65]
            hh = num / jnp.maximum(jnp.abs(den), jnp.exp(-m_t))

            b_last = b_col[L - 1:L, :]
            log_s = b_last - b_col + li_col
            m_new = jnp.maximum(b_last + m_prev, jnp.max(log_s, axis=0, keepdims=True))
            w_c = jnp.exp(b_last + m_prev - m_new)
            kw = (k.astype(F32) * jnp.exp(log_s - m_new)).astype(BF16)
            c_ref[h, :, 0:64] = w_c * cst[:, 0:64] + lax.dot_general(kw, v, TN, preferred_element_type=F32)
            c_ref[h, :, 64:128] = w_c * cst[:, 64:128] + lax.dot_general(kw, ones_v, TN,
                                                                          preferred_element_type=F32)
            m_ref[h] = jnp.broadcast_to(m_new, (8, 128))

            hc = hh - jnp.mean(hh, axis=1, keepdims=True)
            hn = hc * lax.rsqrt(jnp.mean(hc * hc, axis=1, keepdims=True) + LN_EPS) * nw[:, h * 64:(h + 1) * 64]
            y_ref[pl.ds(rr, L), h * 64:(h + 1) * 64] = (jax.nn.sigmoid(og) * hn).astype(BF16)
        return carry

    lax.fori_loop(0, CH // L, pair, 0)


def _mlstm(ml, gates, brow, nw, B, S, CH=512):
    T = ml.shape[0]
    nS = S // CH
    row = lambda b, j: (b * nS + j, 0)
    full = lambda b, j: (0, 0)
    return pl.pallas_call(
        functools.partial(_mlstm_kernel, CH=CH),
        grid=(B, nS),
        in_specs=[pl.BlockSpec((CH, 1024), row), pl.BlockSpec((CH, 128), row),
                  pl.BlockSpec((1, 128), full), pl.BlockSpec((1, 256), full)],
        out_specs=pl.BlockSpec((CH, 256), row),
        out_shape=jax.ShapeDtypeStruct((T, 256), BF16),
        scratch_shapes=[pltpu.VMEM((HEADS, ML_DH, 128), F32), pltpu.VMEM((HEADS, 8, 128), F32)],
        compiler_params=_params(2),
        name="mlstm",
    )(ml, gates, brow, nw)


def _ssd_kernel(ssm_ref, g_ref, brow_ref, alog_ref, cw_ref, cb_ref, d_ref, nw_ref, y_ref,
                xbuf_ref, xact_ref, st_ref, *, CH):
    @pl.when(pl.program_id(1) == 0)
    def _init():
        xbuf_ref[0:8, :] = jnp.zeros((8, 512), F32)
        st_ref[...] = jnp.zeros_like(st_ref)

    xbuf_ref[8:CH + 8, :] = ssm_ref[:, 256:768].astype(F32)
    conv = cb_ref[...] + cw_ref[0:1, :] * xbuf_ref[5:5 + CH, :]
    for j in range(1, SSM_CONV):
        conv = conv + cw_ref[j:j + 1, :] * xbuf_ref[5 + j:5 + j + CH, :]
    xact_ref[...] = _silu(conv)
    xbuf_ref[0:8, :] = xbuf_ref[CH:CH + 8, :]

    L = 2 * CHUNK
    causal, tril, triu = _tri_consts(L)
    lane = lax.broadcasted_iota(jnp.int32, (1, 128), 1)
    a_row = jnp.where((lane >= G_DT) & (lane < G_DT + HEADS), -jnp.exp(alog_ref[...]), 0.0)
    dskip = d_ref[...]
    nw = nw_ref[...]

    def pair(pi, carry):
        rr = pl.multiple_of(pi * L, L)
        dt2 = _softplus(g_ref[pl.ds(rr, L), :] + brow_ref[...])
        a2 = dt2 * a_row
        a2_t = a2.T
        acs_all = jnp.dot(tril, a2, precision=HI, preferred_element_type=F32)
        acs_t = jnp.dot(a2_t[G_DT:G_DT + 8, :], triu, precision=HI, preferred_element_type=F32)
        cb = []
        bmat = []
        cmat = []
        for g in range(SSM_GROUPS):
            bm = xact_ref[pl.ds(rr, L), 256 + g * 64:256 + (g + 1) * 64]
            cm = xact_ref[pl.ds(rr, L), 384 + g * 64:384 + (g + 1) * 64].astype(BF16)
            bmat.append(bm)
            cmat.append(cm)
            cb.append(lax.dot_general(cm, bm.astype(BF16), NT, preferred_element_type=F32))
        gated = []
        ssq = jnp.zeros((L, 1), F32)
        for h in range(HEADS):
            g = h // (HEADS // SSM_GROUPS)
            acs_col = acs_all[:, G_DT + h:G_DT + h + 1]
            acs_row = acs_t[h:h + 1, :]
            dt_col = dt2[:, G_DT + h:G_DT + h + 1]
            xh = xact_ref[pl.ds(rr, L), h * 64:(h + 1) * 64]
            xdt = (xh * dt_col).astype(BF16)
            st = st_ref[h]
            mmat = cb[g] * jnp.exp(jnp.where(causal, acs_col - acs_row, -jnp.inf))
            y = jnp.dot(mmat.astype(BF16), xdt, preferred_element_type=F32)
            y = y + jnp.dot(cmat[g], st.astype(BF16), preferred_element_type=F32) * jnp.exp(acs_col)
            y = y + xh * dskip[:, h * 64:(h + 1) * 64]
            acs_last = acs_col[L - 1:L, :]
            bdec = (bmat[g] * jnp.exp(acs_last - acs_col)).astype(BF16)
            st_ref[h] = jnp.exp(acs_last) * st + lax.dot_general(bdec, xdt, TN, preferred_element_type=F32)
            z = ssm_ref[pl.ds(rr, L), h * 64:(h + 1) * 64].astype(F32)
            yg = y * _silu(z)
            ssq = ssq + jnp.sum(yg * yg, axis=1, keepdims=True)
            gated.append(yg)
        scale = lax.rsqrt(ssq / (HEADS * SSM_P) + LN_EPS)
        for h in range(HEADS):
            y_ref[pl.ds(rr, L), h * 64:(h + 1) * 64] = (gated[h] * scale * nw[:, h * 64:(h + 1) * 64]).astype(BF16)
        return carry

    lax.fori_loop(0, CH // L, pair, 0)


def _ssd(ssm, gates, brow, alog, cw, cb, dskip, nw, B, S, CH=512):
    T = ssm.shape[0]
    nS = S // CH
    row = lambda b, j: (b * nS + j, 0)
    full = lambda b, j: (0, 0)
    return pl.pallas_call(
        functools.partial(_ssd_kernel, CH=CH),
        grid=(B, nS),
        in_specs=[pl.BlockSpec((CH, 768), row), pl.BlockSpec((CH, 128), row), pl.BlockSpec((1, 128), full),
                  pl.BlockSpec((1, 128), full), pl.BlockSpec((SSM_CONV, 512), full), pl.BlockSpec((1, 512), full),
                  pl.BlockSpec((1, 256), full), pl.BlockSpec((1, 256), full)],
        out_specs=pl.BlockSpec((CH, 256), row),
        out_shape=jax.ShapeDtypeStruct((T, 256), BF16),
        scratch_shapes=[pltpu.VMEM((CH + 8, 512), F32), pltpu.VMEM((CH, 512), F32),
                        pltpu.VMEM((HEADS, SSM_N, SSM_P), F32)],
        compiler_params=_params(2),
        name="ssd",
    )(ssm, gates, brow, alog, cw, cb, dskip, nw)


def _attn_kernel(at_ref, o_ref, lse_ref, *, N):
    W = AT_SPAN
    ri = lax.broadcasted_iota(jnp.int32, (W, W), 0)
    ci = lax.broadcasted_iota(jnp.int32, (W, W), 1)
    cur_ok = ri >= ci
    prev_ok = ci >= ri
    low_lanes = ci < AT_DH
    ones = jnp.ones((W, W), BF16)

    NB = next(n for n in (4, 2, 1) if (N // W) % n == 0)

    def pair_attention(q2, kc, kp, vc, vp, pmask):
        outs, lses = [], []
        for first in (True, False):
            qh = jnp.where(low_lanes if first else jnp.logical_not(low_lanes), q2, jnp.zeros_like(q2))
            sc = jnp.where(cur_ok, lax.dot_general(qh, kc, NT, preferred_element_type=F32), -jnp.inf)
            sp = jnp.where(pmask, lax.dot_general(qh, kp, NT, preferred_element_type=F32), -jnp.inf)
            m = jnp.max(jnp.maximum(sc, sp), axis=1, keepdims=True)
            pc = jnp.exp(sc - m).astype(BF16)
            pp = jnp.exp(sp - m).astype(BF16)
            acc = jnp.dot(pc, vc, preferred_element_type=F32) + jnp.dot(pp, vp, preferred_element_type=F32)
            den = jnp.dot(pc, ones, preferred_element_type=F32) + jnp.dot(pp, ones, preferred_element_type=F32)
            outs.append(acc / den)
            lses.append(m + jnp.log(den[:, 0:1]))
        return jnp.where(low_lanes, outs[0], outs[1]).astype(BF16), jnp.where(low_lanes, lses[0], lses[1])

    def blk(i, carry):
        work = []
        for u in range(NB):
            n = i * NB + u
            r0 = pl.multiple_of(n * W, W)
            rp = pl.multiple_of(jnp.maximum(n - 1, 0) * W, W)
            for p in range(HEADS // 2):
                lanes = lambda base: slice(base + 128 * p, base + 128 * (p + 1))
                work.append((r0, p, prev_ok & (n > 0),
                             at_ref[pl.ds(r0, W), lanes(0)], at_ref[pl.ds(r0, W), lanes(256)],
                             at_ref[pl.ds(rp, W), lanes(256)], at_ref[pl.ds(r0, W), lanes(512)],
                             at_ref[pl.ds(rp, W), lanes(512)]))
        done = [(r0, p) + pair_attention(q2, kc, kp, vc, vp, pmask) for r0, p, pmask, q2, kc, kp, vc, vp in work]
        for r0, p, o2, lse2 in done:
            o_ref[pl.ds(r0, W), 128 * p:128 * (p + 1)] = o2
            lse_ref[pl.ds(r0, W), 128 * p:128 * (p + 1)] = lse2
        return carry

    assert (N // W) % NB == 0
    lax.fori_loop(0, N // (W * NB), blk, 0)


def _attn(atv, d, B, S):
    N = S // d
    o, lse = pl.pallas_call(
        functools.partial(_attn_kernel, N=N),
        grid=(B, d),
        in_specs=[pl.BlockSpec((None, N, 768), lambda b, r: (b, 0, r))],
        out_specs=[pl.BlockSpec((None, N, 256), lambda b, r: (b, 0, r)),
                   pl.BlockSpec((None, N, 256), lambda b, r: (b, 0, r))],
        out_shape=[jax.ShapeDtypeStruct((B, N, d * 256), BF16), jax.ShapeDtypeStruct((B, N, d * 256), F32)],
        compiler_params=_params(2),
        name=f"attn_d{d}",
    )(atv.reshape(B, N, d * 768))
    return o.reshape(B * N, d * 256), lse.reshape(B * N, d * 256)


def _gla_kernel(gla_ref, g_ref, w2_ref, b2_ref, nw_ref, y_ref, st_ref, *, CH):
    @pl.when(pl.program_id(1) == 0)
    def _init():
        st_ref[...] = jnp.zeros_like(st_ref)

    L = CHUNK
    SB = 16
    ri = lax.broadcasted_iota(jnp.int32, (L, L), 0)
    ci = lax.broadcasted_iota(jnp.int32, (L, L), 1)
    tril_blk = ((ri >= ci) & (ri // SB == ci // SB)).astype(F32)
    tl = lax.broadcasted_iota(jnp.int32, (L, 128), 0) % SB
    er = lax.broadcasted_iota(jnp.int32, (128, 256), 0) // GLA_DK
    ec = lax.broadcasted_iota(jnp.int32, (128, 256), 1) // GLA_DV
    head_expand = (er == ec).astype(BF16)
    sr = lax.broadcasted_iota(jnp.int32, (256, 128), 0) // GLA_DV
    sc = lax.broadcasted_iota(jnp.int32, (256, 128), 1) // GLA_DK
    st_mask = (sr == sc).astype(F32)
    nw = nw_ref[...]

    def chunk(ci_, st):
        rr = pl.multiple_of(ci_ * L, L)
        lg = _log_sigmoid(jnp.dot(g_ref[pl.ds(rr, L), :], w2_ref[...], preferred_element_type=F32)
                          + b2_ref[...]) / GLA_TAU
        c = jnp.dot(tril_blk, lg, precision=HI, preferred_element_type=F32)
        q = gla_ref[pl.ds(rr, L), 0:128].astype(F32) * GLA_DK ** -0.5
        k = gla_ref[pl.ds(rr, L), 128:256].astype(F32)
        vb = gla_ref[pl.ds(rr, L), 256:512]
        v = vb.astype(F32)
        c4 = c.reshape(L // SB, SB, 128)
        k4 = k.reshape(L // SB, SB, 128)
        v4 = v.reshape(L // SB, SB, 256)

        def bcast(x4, j, width):
            return jnp.broadcast_to(x4[:, j:j + 1, :], (L // SB, SB, width)).reshape(L, width)

        o = jnp.zeros((L, 256), F32)
        for j in range(SB):
            dec = jnp.exp(jnp.minimum(c - bcast(c4, j, 128), 0.0))
            p = jnp.where(tl >= j, q * dec * bcast(k4, j, 128), 0.0)
            a = jnp.dot(p.astype(BF16), head_expand, preferred_element_type=F32)
            o = o + a * bcast(v4, j, 256)

        outs = []
        for i in range(L // SB):
            cblk = c[i * SB:(i + 1) * SB, :]
            clast = cblk[SB - 1:SB, :]
            qe = (q[i * SB:(i + 1) * SB, :] * jnp.exp(cblk)).astype(BF16)
            outs.append(o[i * SB:(i + 1) * SB, :] + lax.dot_general(qe, st.astype(BF16), NT,
                                                                    preferred_element_type=F32))
            ke = (k[i * SB:(i + 1) * SB, :] * jnp.exp(clast - cblk)).astype(BF16)
            upd = lax.dot_general(vb[i * SB:(i + 1) * SB, :], ke, TN, preferred_element_type=F32)
            st = st * jnp.exp(clast) + st_mask * upd
        for i in range(L // SB):
            rg = gla_ref[pl.ds(pl.multiple_of(rr + i * SB, SB), SB), 512:768].astype(F32)
            for h in range(HEADS):
                oh = outs[i][:, h * 64:(h + 1) * 64]
                rms = lax.rsqrt(jnp.mean(oh * oh, axis=1, keepdims=True) + LN_EPS)
                y_ref[pl.ds(pl.multiple_of(rr + i * SB, SB), SB), h * 64:(h + 1) * 64] = (
                    oh * rms * nw[:, h * 64:(h + 1) * 64] * _silu(rg[:, h * 64:(h + 1) * 64])).astype(BF16)
        return st

    st_ref[...] = lax.fori_loop(0, CH // L, chunk, st_ref[...], unroll=2)


def _gla(gla, gates, w2, b2, nw, B, S, CH=512):
    T = gla.shape[0]
    nS = S // CH
    row = lambda b, j: (b * nS + j, 0)
    full = lambda b, j: (0, 0)
    return pl.pallas_call(
        functools.partial(_gla_kernel, CH=CH),
        grid=(B, nS),
        in_specs=[pl.BlockSpec((CH, 768), row), pl.BlockSpec((CH, 128), row), pl.BlockSpec((128, 128), full),
                  pl.BlockSpec((1, 128), full), pl.BlockSpec((1, 256), full)],
        out_specs=pl.BlockSpec((CH, 256), row),
        out_shape=jax.ShapeDtypeStruct((T, 256), BF16),
        scratch_shapes=[pltpu.VMEM((HEADS * GLA_DV, HEADS * GLA_DK), F32)],
        compiler_params=_params(2),
        name="gla",
    )(gla, gates, w2, b2, nw)


def _out_proj_kernel(yml_ref, yssm_ref, o1_ref, o4_ref, o16_ref, l1_ref, l4_ref, l16_ref, ygla_ref, x_ref,
                     wo_ref, g_ref, b_ref, x1_ref, os_ref, ls_ref):
    tm = x_ref.shape[0]
    for i, (d, o_ref, l_ref) in enumerate(((DILATIONS[1], o4_ref, l4_ref), (DILATIONS[2], o16_ref, l16_ref))):
        for r in range(d):
            for j in range(2):
                cols = slice(256 * r + 128 * j, 256 * r + 128 * (j + 1))
                os_ref[i, j, pl.ds(r, tm // d, stride=d), :] = o_ref[:, cols].astype(F32)
                ls_ref[i, j, pl.ds(r, tm // d, stride=d), :] = l_ref[:, cols]
    yat = []
    for j in range(2):
        o1 = o1_ref[:, 128 * j:128 * (j + 1)].astype(F32)
        l1 = l1_ref[:, 128 * j:128 * (j + 1)]
        l4, l16 = ls_ref[0, j], ls_ref[1, j]
        mx = jnp.maximum(jnp.maximum(l1, l4), l16)
        e1, e4, e16 = jnp.exp(l1 - mx), jnp.exp(l4 - mx), jnp.exp(l16 - mx)
        yat.append(((e1 * o1 + e4 * os_ref[0, j] + e16 * os_ref[1, j]) / (e1 + e4 + e16)).astype(BF16))
    acc = jnp.dot(yml_ref[...], wo_ref[0:256, :], preferred_element_type=F32)
    acc = acc + jnp.dot(yssm_ref[...], wo_ref[256:512, :], preferred_element_type=F32)
    acc = acc + jnp.dot(yat[0], wo_ref[512:640, :], preferred_element_type=F32)
    acc = acc + jnp.dot(yat[1], wo_ref[640:768, :], preferred_element_type=F32)
    acc = acc + jnp.dot(ygla_ref[...], wo_ref[768:1024, :], preferred_element_type=F32)
    x1_ref[...] = _layer_norm(ALPHA * x_ref[...] + acc, g_ref[...], b_ref[...])


def _out_proj(yml, yssm, branches, ygla, x, wo, g, b, tm=512):
    T, D = x.shape
    row = lambda i: (i, 0)
    full = lambda i: (0, 0)
    small = pl.BlockSpec((tm, 256), row)
    dil = [pl.BlockSpec((tm // d, 256 * d), row) for d in DILATIONS]
    return pl.pallas_call(
        _out_proj_kernel,
        grid=(T // tm,),
        in_specs=[small, small] + dil + dil + [small, pl.BlockSpec((tm, D), row), pl.BlockSpec(wo.shape, full),
                                               pl.BlockSpec((1, D), full), pl.BlockSpec((1, D), full)],
        out_specs=pl.BlockSpec((tm, D), row),
        out_shape=jax.ShapeDtypeStruct((T, D), F32),
        scratch_shapes=[pltpu.VMEM((2, 2, tm, 128), F32), pltpu.VMEM((2, 2, tm, 128), F32)],
        compiler_params=_params(1),
        name="out_proj_ln",
    )(yml, yssm, *[o for o, _ in branches], *[l for _, l in branches], ygla, x, wo, g, b)


def _router_kernel(x_ref, wr_ref, rb_ref, row_ref, col_ref, cnt_ref, rt_ref):
    tm = x_ref.shape[0]
    gsz = N_EXPERTS // N_GROUPS
    logits = lax.dot_general(wr_ref[...], x_ref[...], NT, precision=HI, preferred_element_type=F32)
    scores = jax.nn.sigmoid(logits)
    sel = scores + rb_ref[...]
    eidx = lax.broadcasted_iota(jnp.int32, (gsz, tm), 0)
    big = jnp.int32(1 << 20)
    neg = -jnp.inf

    sel_g = [sel[g * gsz:(g + 1) * gsz, :] for g in range(N_GROUPS)]
    idx_g = [eidx + g * gsz for g in range(N_GROUPS)]
    gscore = []
    for g in range(N_GROUPS):
        v = sel_g[g]
        m1 = jnp.max(v, axis=0, keepdims=True)
        i1 = jnp.min(jnp.where(v == m1, idx_g[g], big), axis=0, keepdims=True)
        m2 = jnp.max(jnp.where(idx_g[g] == i1, neg, v), axis=0, keepdims=True)
        gscore.append(m1 + m2)
    gkeep = [jnp.zeros((1, tm), jnp.bool_) for _ in range(N_GROUPS)]
    for _ in range(TOPK_GROUPS):
        m = functools.reduce(jnp.maximum, gscore)
        gi = functools.reduce(jnp.minimum, [jnp.where(gscore[g] == m, g, big) for g in range(N_GROUPS)])
        for g in range(N_GROUPS):
            hit = gi == g
            gkeep[g] = gkeep[g] | hit
            gscore[g] = jnp.where(hit, neg, gscore[g])
    cand = [jnp.where(gkeep[g], sel_g[g], neg) for g in range(N_GROUPS)]
    chosen = [jnp.zeros((gsz, tm), jnp.bool_) for _ in range(N_GROUPS)]
    picks = []
    for _ in range(TOP_K):
        m = functools.reduce(jnp.maximum, [jnp.max(c, axis=0, keepdims=True) for c in cand])
        ei = functools.reduce(jnp.minimum, [jnp.min(jnp.where(cand[g] == m, idx_g[g], big), axis=0, keepdims=True)
                                            for g in range(N_GROUPS)])
        picks.append(ei)
        for g in range(N_GROUPS):
            hit = idx_g[g] == ei
            chosen[g] = chosen[g] | hit
            cand[g] = jnp.where(hit, neg, cand[g])
    picked = [jnp.where(chosen[g], scores[g * gsz:(g + 1) * gsz, :], 0.0) for g in range(N_GROUPS)]
    tot = functools.reduce(jnp.add, [jnp.sum(p, axis=0, keepdims=True) for p in picked])
    gates = [p / tot * ROUTED_SCALE for p in picked]

    chosen_b = jnp.concatenate([c.astype(F32) for c in chosen] + [jnp.zeros((N_EXPERTS, tm), F32)], axis=0).astype(BF16)
    ti = lax.broadcasted_iota(jnp.int32, (tm, tm), 0)
    tj = lax.broadcasted_iota(jnp.int32, (tm, tm), 1)
    before = ((ti < tj) & (ti // TOK_TILE == tj // TOK_TILE)).astype(BF16)
    rank = jnp.dot(chosen_b[0:N_EXPERTS, :], before, preferred_element_type=F32)
    lower = (lax.broadcasted_iota(jnp.int32, (N_EXPERTS, N_EXPERTS), 0)
             > lax.broadcasted_iota(jnp.int32, (N_EXPERTS, N_EXPERTS), 1)).astype(F32)
    tile_of_lane = lax.broadcasted_iota(jnp.int32, (1, tm), 1) // TOK_TILE
    posmat = rank
    for t in range(tm // TOK_TILE):
        in_tile = chosen_b[:, t * TOK_TILE:(t + 1) * TOK_TILE]
        cnt = jnp.dot(in_tile[0:N_EXPERTS, :], jnp.ones((TOK_TILE, 128), BF16), preferred_element_type=F32)
        padded = jnp.maximum(jnp.floor((cnt + (ROW_PAD - 1)) * (1.0 / ROW_PAD)) * ROW_PAD, float(ROW_PAD))
        gstart = jnp.dot(lower, padded, precision=HI, preferred_element_type=F32)
        posmat = posmat + jnp.where(tile_of_lane == t, gstart[:, 0:1], 0.0)
        cnt_ref[8 * t:8 * (t + 1), :] = lax.dot_general(jnp.ones((8, TOK_TILE), BF16), in_tile, NT,
                                                        preferred_element_type=F32)
    rt_ref[...] = jnp.zeros_like(rt_ref)
    for k in range(TOP_K):
        pos_k = jnp.zeros((1, tm), F32)
        gate_k = jnp.zeros((1, tm), F32)
        for g in range(N_GROUPS):
            hit = idx_g[g] == picks[k]
            pos_k = pos_k + jnp.sum(jnp.where(hit, posmat[g * gsz:(g + 1) * gsz, :], 0.0), axis=0, keepdims=True)
            gate_k = gate_k + jnp.sum(jnp.where(hit, gates[g], 0.0), axis=0, keepdims=True)
        rt_ref[k:k + 1, :] = pos_k
        rt_ref[TOP_K + k:TOP_K + k + 1, :] = gate_k
    row_ref[...] = rt_ref[0:2 * TOP_K, :]
    col_ref[...] = rt_ref[...].T


def _router(x1, wr_t, rb):
    T, D = x1.shape
    tm = ROUTER_TILE
    sub = tm // TOK_TILE
    return pl.pallas_call(
        _router_kernel,
        grid=(T // tm,),
        in_specs=[pl.BlockSpec((tm, D), lambda i: (i, 0)), pl.BlockSpec(wr_t.shape, lambda i: (0, 0)),
                  pl.BlockSpec(rb.shape, lambda i: (0, 0))],
        out_specs=[pl.BlockSpec((2 * TOP_K, tm), lambda i: (0, i)), pl.BlockSpec((tm, 128), lambda i: (i, 0)),
                   pl.BlockSpec((8 * sub, 128), lambda i: (i, 0))],
        out_shape=[jax.ShapeDtypeStruct((2 * TOP_K, T), F32), jax.ShapeDtypeStruct((T, 128), F32),
                   jax.ShapeDtypeStruct((T // TOK_TILE * 8, 128), F32)],
        scratch_shapes=[pltpu.VMEM((128, tm), F32)],
        compiler_params=_params(1),
        name="router",
    )(x1, wr_t, rb)


M_LSTART, M_ROWS, M_GOFF = 0, N_EXPERTS, 2 * N_EXPERTS
M_USED, M_USED_PREV = 3 * N_EXPERTS, 3 * N_EXPERTS + 1
M_COLS = 4 * N_EXPERTS


def _ceil_div(x, n):
    assert n & (n - 1) == 0
    return lax.shift_right_logical(x + (n - 1), n.bit_length() - 1)


def _seg_copy(meta_ref, e, loc_ref, glob_ref, sem, to_global):
    n = pl.multiple_of(meta_ref[0, 0, M_ROWS + e] * ROW_PAD, ROW_PAD)
    ls = pl.multiple_of(meta_ref[0, 0, M_LSTART + e] * ROW_PAD, ROW_PAD)
    go = pl.multiple_of(meta_ref[0, 0, M_GOFF + e] * ROW_PAD, ROW_PAD)
    loc, glob = loc_ref.at[pl.ds(ls, n)], glob_ref.at[pl.ds(go, n)]
    return pltpu.make_async_copy(loc, glob, sem) if to_global else pltpu.make_async_copy(glob, loc, sem)


def _dispatch_kernel(meta_ref, tail_ref, x_ref, row_ref, xg_ref, loc_ref, zero_ref, sems):
    s = pl.program_id(0)
    last = pl.num_programs(0) - 1
    slot = s % 2
    tm = x_ref.shape[0]

    @pl.when(s == 0)
    def _fill_tails():
        zero_ref[...] = jnp.zeros_like(zero_ref)

        def start(e, c):
            n = pl.multiple_of(tail_ref[1, e] * ROW_PAD, ROW_PAD)
            go = pl.multiple_of(tail_ref[0, e] * ROW_PAD, ROW_PAD)

            @pl.when(n > 0)
            def _():
                pltpu.make_async_copy(zero_ref.at[pl.ds(0, n)], xg_ref.at[pl.ds(go, n)], sems.at[2]).start()
            return c
        lax.fori_loop(0, N_EXPERTS, start, 0)

    RC = 256
    xb = x_ref[...].astype(BF16)
    pos = row_ref[0:TOP_K, :]
    hi = jnp.floor(pos * (1.0 / RC))
    lo = pos - hi * RC
    rows = lax.broadcasted_iota(jnp.int32, (RC, tm), 0).astype(F32)
    lo_hit = [jnp.where(lo[k:k + 1, :] == rows, 1.0, 0.0).astype(BF16) for k in range(TOP_K)]
    used = meta_ref[0, 0, M_USED]

    def chunk(c, carry):
        cf = c.astype(F32)
        onehot = lo_hit[0] * jnp.where(hi[0:1, :] == cf, 1.0, 0.0).astype(BF16)
        for k in range(1, TOP_K):
            onehot = onehot + lo_hit[k] * jnp.where(hi[k:k + 1, :] == cf, 1.0, 0.0).astype(BF16)
        loc_ref[slot, pl.ds(pl.multiple_of(c * RC, RC), RC), :] = jnp.dot(
            onehot, xb, preferred_element_type=F32).astype(BF16)
        return carry
    lax.fori_loop(0, _ceil_div(used, RC // ROW_PAD), chunk, 0)

    def start(e, c):
        _seg_copy(meta_ref, e, loc_ref.at[slot], xg_ref, sems.at[slot], True).start()
        return c
    lax.fori_loop(0, N_EXPERTS, start, 0)

    def wait_all(total_col, buf):
        n = pl.multiple_of(meta_ref[0, 0, total_col] * ROW_PAD, ROW_PAD)

        @pl.when(n > 0)
        def _():
            pltpu.make_async_copy(loc_ref.at[buf, pl.ds(0, n)], xg_ref.at[pl.ds(0, n)], sems.at[buf]).wait()

    @pl.when(s > 0)
    def _wait_prev():
        wait_all(M_USED_PREV, 1 - slot)

    @pl.when(s == last)
    def _wait_own():
        wait_all(M_USED, slot)

    @pl.when(s == 0)
    def _wait_tails():
        def wait(e, c):
            n = pl.multiple_of(tail_ref[1, e] * ROW_PAD, ROW_PAD)
            go = pl.multiple_of(tail_ref[0, e] * ROW_PAD, ROW_PAD)

            @pl.when(n > 0)
            def _():
                pltpu.make_async_copy(zero_ref.at[pl.ds(0, n)], xg_ref.at[pl.ds(go, n)], sems.at[2]).wait()
            return c
        lax.fori_loop(0, N_EXPERTS, wait, 0)


def _dispatch(meta, tail, x1, rowform, p_rows):
    T, D = x1.shape
    tm = TOK_TILE
    return pl.pallas_call(
        _dispatch_kernel,
        grid=(T // tm,),
        in_specs=[pl.BlockSpec((1, 1, M_COLS), lambda i: (i, 0, 0), memory_space=pltpu.SMEM),
                  pl.BlockSpec(memory_space=pltpu.SMEM),
                  pl.BlockSpec((tm, D), lambda i: (i, 0)), pl.BlockSpec((2 * TOP_K, tm), lambda i: (0, i))],
        out_specs=pl.BlockSpec(memory_space=pl.ANY),
        out_shape=jax.ShapeDtypeStruct((p_rows, D), BF16),
        scratch_shapes=[pltpu.VMEM((2, LOC_ROWS, D), BF16), pltpu.VMEM((FFN_BLK, D), BF16),
                        pltpu.SemaphoreType.DMA((3,))],
        compiler_params=_params(1),
        name="moe_dispatch",
    )(meta, tail, x1, rowform)


def _ffn_kernel(bexp_ref, nused_ref, x_ref, wg_ref, wu_ref, wd_ref, y_ref, wgb_ref, wub_ref, wdb_ref):
    i = pl.program_id(0)

    @pl.when(i < nused_ref[0])
    def _():
        @pl.when((i == 0) | (bexp_ref[i] != bexp_ref[jnp.maximum(i - 1, 0)]))
        def _new_expert():
            wgb_ref[...] = wg_ref[...].astype(BF16)
            wub_ref[...] = wu_ref[...].astype(BF16)
            wdb_ref[...] = wd_ref[...].astype(BF16)

        xb = x_ref[...]
        a = jnp.dot(xb, wgb_ref[...], preferred_element_type=F32)
        u = jnp.dot(xb, wub_ref[...], preferred_element_type=F32)
        y_ref[...] = jnp.dot((_silu(a) * u).astype(BF16), wdb_ref[...], preferred_element_type=F32).astype(BF16)


def _ffn(blk_exp, nused, xg, wg, wu, wd, layer):
    P, D = xg.shape
    F = wg.shape[3]
    blk = lambda i, be, nu: (jnp.maximum(jnp.minimum(i, nu[0] - 1), 0), 0)
    wsel = lambda i, be, nu: (layer, be[jnp.maximum(jnp.minimum(i, nu[0] - 1), 0)], 0, 0)
    return pl.pallas_call(
        _ffn_kernel,
        grid_spec=pltpu.PrefetchScalarGridSpec(
            num_scalar_prefetch=2,
            grid=(P // FFN_BLK,),
            in_specs=[pl.BlockSpec((FFN_BLK, D), blk), pl.BlockSpec((None, None, D, F), wsel),
                      pl.BlockSpec((None, None, D, F), wsel), pl.BlockSpec((None, None, F, D), wsel)],
            out_specs=pl.BlockSpec((FFN_BLK, D), blk),
            scratch_shapes=[pltpu.VMEM((D, F), BF16), pltpu.VMEM((D, F), BF16), pltpu.VMEM((F, D), BF16)]),
        out_shape=jax.ShapeDtypeStruct((P, D), BF16),
        compiler_params=_params(1),
        name="moe_ffn",
    )(blk_exp, nused, xg, wg, wu, wd)


def _combine_kernel(meta_ref, next_ref, x_ref, col_ref, yg_ref, sg_ref, su_ref, sd_ref, g_ref, b_ref, x2_ref,
                    loc_ref, z_ref, sems):
    s = pl.program_id(0)
    last = pl.num_programs(0) - 1
    slot = s % 2
    tm = x_ref.shape[0]
    CW = 512
    cw_u = CW // ROW_PAD

    def used_rows(m_ref):
        return m_ref[0, 0, M_USED]

    def fetch(m_ref, buf):
        def start(e, c):
            _seg_copy(m_ref, e, loc_ref.at[buf], yg_ref, sems.at[buf], False).start()
            return c
        lax.fori_loop(0, N_EXPERTS, start, 0)
        used = used_rows(m_ref)

        def clear(c, carry):
            loc_ref[buf, pl.ds(pl.multiple_of(c * ROW_PAD, ROW_PAD), ROW_PAD), :] = jnp.zeros(
                (ROW_PAD, loc_ref.shape[2]), BF16)
            return carry
        lax.fori_loop(used, _ceil_div(used, cw_u) * cw_u, clear, 0)

    @pl.when(s == 0)
    def _first():
        fetch(meta_ref, 0)

    @pl.when(s < last)
    def _prefetch():
        fetch(next_ref, 1 - slot)

    x1 = x_ref[...]
    xb = x1.astype(BF16)
    hid = _silu(jnp.dot(xb, sg_ref[...], preferred_element_type=F32)) * jnp.dot(xb, su_ref[...],
                                                                                preferred_element_type=F32)
    z_ref[...] = ALPHA * x1 + jnp.dot(hid.astype(BF16), sd_ref[...], preferred_element_type=F32)

    pos = col_ref[:, 0:TOP_K]
    gate = col_ref[:, TOP_K:2 * TOP_K]
    hi = jnp.floor(pos * (1.0 / CW))
    lo = pos - hi * CW
    lanes = lax.broadcasted_iota(jnp.int32, (tm, CW), 1).astype(F32)
    lo_hit = [jnp.where(lo[:, k:k + 1] == lanes, 1.0, 0.0).astype(BF16) for k in range(TOP_K)]
    n_own = pl.multiple_of(used_rows(meta_ref) * ROW_PAD, ROW_PAD)

    @pl.when(n_own > 0)
    def _wait_own():
        pltpu.make_async_copy(yg_ref.at[pl.ds(0, n_own)], loc_ref.at[slot, pl.ds(0, n_own)], sems.at[slot]).wait()

    def chunk(c, carry):
        cf = c.astype(F32)
        wmat = lo_hit[0] * jnp.where(hi[:, 0:1] == cf, gate[:, 0:1], 0.0).astype(BF16)
        for k in range(1, TOP_K):
            wmat = wmat + lo_hit[k] * jnp.where(hi[:, k:k + 1] == cf, gate[:, k:k + 1], 0.0).astype(BF16)
        rows = loc_ref[slot, pl.ds(pl.multiple_of(c * CW, CW), CW), :]
        z_ref[...] += jnp.dot(wmat, rows, preferred_element_type=F32)
        return carry
    lax.fori_loop(0, _ceil_div(used_rows(meta_ref), cw_u), chunk, 0)
    x2_ref[...] = _layer_norm(z_ref[...], g_ref[...], b_ref[...])


def _combine(meta, x1, colform, yg, sg, su, sd, g, b):
    T, D = x1.shape
    tm = TOK_TILE
    nst = T // tm
    full = lambda i: (0, 0)
    mspec = lambda f: pl.BlockSpec((1, 1, M_COLS), f, memory_space=pltpu.SMEM)
    return pl.pallas_call(
        _combine_kernel,
        grid=(nst,),
        in_specs=[mspec(lambda i: (i, 0, 0)), mspec(lambda i: (jnp.minimum(i + 1, nst - 1), 0, 0)),
                  pl.BlockSpec((tm, D), lambda i: (i, 0)), pl.BlockSpec((tm, 128), lambda i: (i, 0)),
                  pl.BlockSpec(memory_space=pl.ANY),
                  pl.BlockSpec(sg.shape, full), pl.BlockSpec(su.shape, full), pl.BlockSpec(sd.shape, full),
                  pl.BlockSpec((1, D), full), pl.BlockSpec((1, D), full)],
        out_specs=pl.BlockSpec((tm, D), lambda i: (i, 0)),
        out_shape=jax.ShapeDtypeStruct((T, D), F32),
        scratch_shapes=[pltpu.VMEM((2, LOC_ROWS, D), BF16), pltpu.VMEM((tm, D), F32),
                        pltpu.SemaphoreType.DMA((2,))],
        compiler_params=_params(1),
        name="moe_combine_ln",
    )(meta, meta, x1, colform, yg, sg, su, sd, g, b)


def _moe_plan(cnt_out, T):
    nst = T // TOK_TILE
    blk_u = FFN_BLK // ROW_PAD
    cnt = cnt_out.reshape(nst, 8, 128)[:, 0, :N_EXPERTS].astype(jnp.int32)
    rows = jnp.maximum((cnt + ROW_PAD - 1) // ROW_PAD, 1)
    lstart = jnp.cumsum(rows, axis=1) - rows
    tot = rows.sum(axis=0)
    tot_pad = (tot + blk_u - 1) // blk_u * blk_u
    eend = jnp.cumsum(tot_pad)
    ebase = eend - tot_pad
    goff = ebase[None, :] + jnp.cumsum(rows, axis=0) - rows
    used = rows.sum(axis=1, keepdims=True)
    used_prev = jnp.concatenate([jnp.zeros((1, 1), jnp.int32), used[:-1]], axis=0)
    fill = jnp.zeros((nst, M_COLS - M_USED_PREV - 1), jnp.int32)
    meta = jnp.concatenate([lstart, rows, goff, used, used_prev, fill], axis=1).reshape(nst, 1, M_COLS)
    tail = jnp.stack([ebase + tot, tot_pad - tot])
    nblk = _moe_rows(T) // FFN_BLK
    nused = (eend[-1] // blk_u).reshape(1)
    first_row = jnp.arange(nblk, dtype=jnp.int32) * blk_u
    blk_exp = jnp.minimum(jnp.sum(first_row[:, None] >= eend[None, :], axis=1), N_EXPERTS - 1).astype(jnp.int32)
    return meta, tail, blk_exp, nused


def _moe_rows(T):
    worst = T * TOP_K + (T // TOK_TILE) * N_EXPERTS * ROW_PAD + N_EXPERTS * (FFN_BLK - ROW_PAD)
    return -(-worst // FFN_BLK) * FFN_BLK


def _rope_tables(S):
    half = AT_DH // 2
    lane = jnp.arange(128)
    inv = ROPE_THETA ** (-(lane % half).astype(F32) / half)
    ang = jnp.arange(S, dtype=F32)[:, None] * inv[None, :]
    sign = jnp.where((lane % AT_DH) < half, -1.0, 1.0).astype(F32)
    return jnp.cos(ang), jnp.sin(ang) * sign[None, :]


def _pad_cols(a, width):
    return jnp.pad(a, ((0, 0), (0, width - a.shape[1])))


def kernel(x, w_in, ml_i_bias, ml_f_bias, ml_norm_w, ssm_conv_w, ssm_conv_b, ssm_dt_bias, ssm_a_log, ssm_d, ssm_norm_w, gla_gate_w2, gla_gate_b, gla_norm_w, w_out, ln1_g, ln1_b, router_w, router_bias, exp_w_gate, exp_w_up, exp_w_down, sh_w_gate, sh_w_up, sh_w_down, ln2_g, ln2_b):
    B, S, D = x.shape
    T = B * S
    depth = w_in.shape[0]
    assert D == 1024 and S % (AT_SPAN * max(DILATIONS)) == 0 and T % 1024 == 0
    cos, sin = _rope_tables(S)
    xf = x.reshape(T, D)
    for l in range(depth):
        wm, wgt = _regroup_weights(w_in, l)
        brow = _pad_cols(jnp.concatenate([ml_i_bias[l], ml_f_bias[l], ssm_dt_bias[l]])[None, :], 128).astype(F32)
        alog = jnp.pad(ssm_a_log[l].astype(F32), (G_DT, 128 - G_DT - HEADS))[None, :]
        w2 = jnp.pad(gla_gate_w2[l].astype(F32), ((G_GA, 128 - G_GA - GLA_RANK), (0, 0)))

        ml, ssm, at1, at4, at16, gla, gates = _in_proj(xf, wm, wgt, cos, sin, S)
        y_ml = _mlstm(ml, gates, brow, ml_norm_w[l][None, :].astype(F32), B, S)
        y_ssm = _ssd(ssm, gates, brow, alog, ssm_conv_w[l].astype(F32), ssm_conv_b[l][None, :].astype(F32),
                     jnp.repeat(ssm_d[l].astype(F32), SSM_P)[None, :], ssm_norm_w[l][None, :].astype(F32), B, S)
        branches = [_attn(atv, d, B, S) for atv, d in zip((at1, at4, at16), DILATIONS)]
        y_gla = _gla(gla, gates, w2, gla_gate_b[l][None, :].astype(F32), gla_norm_w[l][None, :].astype(F32), B, S)
        x1 = _out_proj(y_ml, y_ssm, branches, y_gla, xf,
                       w_out[l].astype(BF16), ln1_g[l][None, :].astype(F32), ln1_b[l][None, :].astype(F32))
        rowform, colform, cnt = _router(x1, router_w[l].T.astype(F32), router_bias[l][:, None].astype(F32))
        meta, tail, blk_exp, nused = _moe_plan(cnt, T)
        xg = _dispatch(meta, tail, x1, rowform, _moe_rows(T))
        yg = _ffn(blk_exp, nused, xg, exp_w_gate, exp_w_up, exp_w_down, l)
        xf = _combine(meta, x1, colform, yg, sh_w_gate[l].astype(BF16), sh_w_up[l].astype(BF16),
                      sh_w_down[l].astype(BF16), ln2_g[l][None, :].astype(F32), ln2_b[l][None, :].astype(F32))
    return xf.reshape(B, S, D)
```

```python
import functools

import jax
import jax.numpy as jnp
from jax import lax
from jax.experimental import pallas as pl
from jax.experimental.pallas import tpu as pltpu

F32 = jnp.float32
BF16 = jnp.bfloat16
HI = lax.Precision.HIGHEST
NT = (((1,), (1,)), ((), ()))
TN = (((0,), (0,)), ((), ()))

DEPTH = 4
HEADS = 4
ML_DH = 64
SSM_P = 64
SSM_N = 64
SSM_GROUPS = 2
SSM_CONV = 4
AT_DH = 64
DILATIONS = (1, 4, 16)
AT_SPAN = 128
ROPE_THETA = 10000.0
GLA_DK = 32
GLA_DV = 64
GLA_RANK = 16
GLA_TAU = 16.0
N_EXPERTS = 64
TOP_K = 8
N_GROUPS = 8
TOPK_GROUPS = 4
ROUTED_SCALE = 2.5
ALPHA = (2 * DEPTH) ** 0.25
LN_EPS = 1e-5
CHUNK = 64
TOK_TILE = 256
ROUTER_TILE = 512
ROW_PAD = 16
FFN_BLK = 1024
LOC_ROWS = -(-(TOK_TILE * TOP_K + N_EXPERTS * ROW_PAD) // 512) * 512

G_MI, G_MF, G_DT, G_GA = 0, 4, 8, 12

VMEM_LIMIT = 48 * 1024 * 1024


def _log_sigmoid(x):
    return jnp.minimum(x, 0.0) - jnp.log(1.0 + jnp.exp(-jnp.abs(x)))


def _softplus(x):
    return jnp.maximum(x, 0.0) + jnp.log(1.0 + jnp.exp(-jnp.abs(x)))


def _silu(x):
    return x * jax.nn.sigmoid(x)


def _layer_norm(z, g, b):
    mu = jnp.mean(z, axis=-1, keepdims=True)
    zc = z - mu
    var = jnp.mean(zc * zc, axis=-1, keepdims=True)
    return zc * lax.rsqrt(var + LN_EPS) * g + b


def _params(n_axes):
    return pltpu.CompilerParams(dimension_semantics=("arbitrary",) * n_axes, vmem_limit_bytes=VMEM_LIMIT)


W_MI, W_SZ, W_SDT, W_AQ, W_GQ, W_GA = 1024, 1032, 1800, 1804, 2572, 3340
W_MAIN = 3328


def _regroup_kernel(w_ref, wm_ref, wg_ref):
    RB = 256
    for c in range(0, w_ref.shape[0], RB):
        rows = slice(c, c + RB)
        wm_ref[rows, 0:1024] = w_ref[rows, 0:W_MI].astype(BF16)
        wm_ref[rows, 1024:1792] = w_ref[rows, W_SZ:W_SDT].astype(BF16)
        wm_ref[rows, 1792:2560] = w_ref[rows, W_AQ:W_GQ].astype(BF16)
        wm_ref[rows, 2560:W_MAIN] = w_ref[rows, W_GQ:W_GA].astype(BF16)
        wg_ref[rows, :] = jnp.zeros((RB, 128), BF16)
        wg_ref[rows, G_MI:G_DT] = w_ref[rows, W_MI:W_SZ].astype(BF16)
        wg_ref[rows, G_DT:G_GA] = w_ref[rows, W_SDT:W_AQ].astype(BF16)
        wg_ref[rows, G_GA:G_GA + GLA_RANK] = w_ref[rows, W_GA:W_GA + GLA_RANK].astype(BF16)


def _regroup_weights(w_in, layer):
    _, D, N = w_in.shape
    full = lambda i: (0, 0)
    return pl.pallas_call(
        _regroup_kernel,
        grid=(1,),
        in_specs=[pl.BlockSpec((None, D, N), lambda i: (layer, 0, 0))],
        out_specs=[pl.BlockSpec((D, W_MAIN), full), pl.BlockSpec((D, 128), full)],
        out_shape=[jax.ShapeDtypeStruct((D, W_MAIN), BF16), jax.ShapeDtypeStruct((D, 128), BF16)],
        compiler_params=_params(1),
        name="regroup_w_in",
    )(w_in)


def _in_proj_kernel(x_ref, wm_ref, wg_ref, cos_ref, sin_ref, ml_ref, ssm_ref, at1_ref, at4_ref, at16_ref, gla_ref,
                    g_ref, atf_ref):
    at_refs = (at1_ref, at4_ref, at16_ref)
    xb = x_ref[...].astype(BF16)

    def mm(lo):
        return jnp.dot(xb, wm_ref[:, lo:lo + 256], preferred_element_type=F32)

    for c in range(4):
        a = mm(256 * c)
        if c == 1:
            a = a * ML_DH ** -0.5
        ml_ref[:, 256 * c:256 * (c + 1)] = a.astype(BF16)
    for c in range(3):
        ssm_ref[:, 256 * c:256 * (c + 1)] = mm(1024 + 256 * c).astype(BF16)

    cos = cos_ref[...]
    sin = sin_ref[...]
    lane = lax.broadcasted_iota(jnp.int32, cos.shape, 1)
    first_half = (lane % AT_DH) < AT_DH // 2

    def rope(a):
        rot = jnp.where(first_half, pltpu.roll(a, 128 - AT_DH // 2, 1), pltpu.roll(a, AT_DH // 2, 1))
        return a * cos + rot * sin

    tm = x_ref.shape[0]
    for c in range(3):
        a = mm(1792 + 256 * c)
        for hh in range(2):
            tile = a[:, 128 * hh:128 * (hh + 1)]
            if c < 2:
                tile = rope(tile) * (AT_DH ** -0.5 if c == 0 else 1.0)
            atf_ref[2 * c + hh] = tile
    for d, ref in zip(DILATIONS, at_refs):
        for r in range(d):
            for j in range(6):
                rows = atf_ref[j] if d == 1 else atf_ref[j, pl.ds(r, tm // d, stride=d), :]
                ref[:, 768 * r + 128 * j:768 * r + 128 * (j + 1)] = rows.astype(BF16)
    for c in range(3):
        gla_ref[:, 256 * c:256 * (c + 1)] = mm(2560 + 256 * c).astype(BF16)
    g_ref[...] = jnp.dot(xb, wg_ref[...], preferred_element_type=F32)


def _in_proj(x, wm, wg, cos, sin, S, tm=512):
    T, D = x.shape
    nS = S // tm
    row = lambda i: (i, 0)
    full = lambda i: (0, 0)
    return pl.pallas_call(
        _in_proj_kernel,
        grid=(T // tm,),
        in_specs=[pl.BlockSpec((tm, D), row), pl.BlockSpec(wm.shape, full), pl.BlockSpec(wg.shape, full),
                  pl.BlockSpec((tm, 128), lambda i: (i % nS, 0)), pl.BlockSpec((tm, 128), lambda i: (i % nS, 0))],
        out_specs=[pl.BlockSpec((tm, 1024), row), pl.BlockSpec((tm, 768), row)]
        + [pl.BlockSpec((tm // d, 768 * d), row) for d in DILATIONS]
        + [pl.BlockSpec((tm, 768), row), pl.BlockSpec((tm, 128), row)],
        out_shape=[jax.ShapeDtypeStruct((T, 1024), BF16), jax.ShapeDtypeStruct((T, 768), BF16)]
        + [jax.ShapeDtypeStruct((T // d, 768 * d), BF16) for d in DILATIONS]
        + [jax.ShapeDtypeStruct((T, 768), BF16), jax.ShapeDtypeStruct((T, 128), F32)],
        scratch_shapes=[pltpu.VMEM((6, tm, 128), F32)],
        compiler_params=_params(1),
        name="in_proj",
    )(x, wm, wg, cos, sin)


def _tri_consts(L):
    ri = lax.broadcasted_iota(jnp.int32, (L, L), 0)
    ci = lax.broadcasted_iota(jnp.int32, (L, L), 1)
    causal = ri >= ci
    return causal, causal.astype(F32), (ri <= ci).astype(F32)


def _mlstm_kernel(ml_ref, g_ref, brow_ref, nw_ref, y_ref, c_ref, m_ref, *, CH):
    @pl.when(pl.program_id(1) == 0)
    def _init():
        c_ref[...] = jnp.zeros_like(c_ref)
        m_ref[...] = jnp.full_like(m_ref, -jnp.inf)

    L = 2 * CHUNK
    causal, tril, triu = _tri_consts(L)
    lane = lax.broadcasted_iota(jnp.int32, (L, 128), 1)
    is_f = (lane >= G_MF) & (lane < G_MF + HEADS)
    ones_v = jnp.ones((L, ML_DH), BF16)
    nw = nw_ref[...]

    def pair(pi, carry):
        rr = pl.multiple_of(pi * L, L)
        g2 = g_ref[pl.ds(rr, L), :] + brow_ref[...]
        vc = jnp.where(is_f, _log_sigmoid(g2), g2)
        vtc = vc.T[0:8, :]
        b_all = jnp.dot(tril, vc, precision=HI, preferred_element_type=F32)
        bt_all = jnp.dot(vtc, triu, precision=HI, preferred_element_type=F32)
        for h in range(HEADS):
            q = ml_ref[pl.ds(rr, L), h * 64:(h + 1) * 64]
            k = ml_ref[pl.ds(rr, L), 256 + h * 64:256 + (h + 1) * 64]
            v = ml_ref[pl.ds(rr, L), 512 + h * 64:512 + (h + 1) * 64]
            og = ml_ref[pl.ds(rr, L), 768 + h * 64:768 + (h + 1) * 64].astype(F32)
            b_col = b_all[:, G_MF + h:G_MF + h + 1]
            li_col = vc[:, G_MI + h:G_MI + h + 1]
            b_row = bt_all[G_MF + h:G_MF + h + 1, :]
            li_row = vtc[G_MI + h:G_MI + h + 1, :]
            m_prev = m_ref[h][0:1, 0:1]
            cst = c_ref[h]

            log_d = jnp.where(causal, b_col - b_row + li_row, -jnp.inf)
            log_inter = b_col + m_prev
            m_t = jnp.maximum(log_inter, jnp.max(log_d, axis=1, keepdims=True))
            w_inter = jnp.exp(log_inter - m_t)
            s = lax.dot_general(q, k, NT, preferred_element_type=F32) * jnp.exp(log_d - m_t)
            qc = jnp.dot(q, cst.astype(BF16), preferred_element_type=F32)
            num = jnp.dot(s.astype(BF16), v, preferred_element_type=F32) + w_inter * qc[:, 0:64]
            den = jnp.sum(s, axis=1, keepdims=True) + w_inter * qc[:, 64:65]
            hh = num / jnp.maximum(jnp.abs(den), jnp.exp(-m_t))

            b_last = b_col[L - 1:L, :]
            log_s = b_last - b_col + li_col
            m_new = jnp.maximum(b_last + m_prev, jnp.max(log_s, axis=0, keepdims=True))
            w_c = jnp.exp(b_last + m_prev - m_new)
            kw = (k.astype(F32) * jnp.exp(log_s - m_new)).astype(BF16)
            c_ref[h, :, 0:64] = w_c * cst[:, 0:64] + lax.dot_general(kw, v, TN, preferred_element_type=F32)
            c_ref[h, :, 64:128] = w_c * cst[:, 64:128] + lax.dot_general(kw, ones_v, TN,
                                                                          preferred_element_type=F32)
            m_ref[h] = jnp.broadcast_to(m_new, (8, 128))

            hc = hh - jnp.mean(hh, axis=1, keepdims=True)
            hn = hc * lax.rsqrt(jnp.mean(hc * hc, axis=1, keepdims=True) + LN_EPS) * nw[:, h * 64:(h + 1) * 64]
            y_ref[pl.ds(rr, L), h * 64:(h + 1) * 64] = (jax.nn.sigmoid(og) * hn).astype(BF16)
        return carry

    lax.fori_loop(0, CH // L, pair, 0)


def _mlstm(ml, gates, brow, nw, B, S, CH=512):
    T = ml.shape[0]
    nS = S // CH
    row = lambda b, j: (b * nS + j, 0)
    full = lambda b, j: (0, 0)
    return pl.pallas_call(
        functools.partial(_mlstm_kernel, CH=CH),
        grid=(B, nS),
        in_specs=[pl.BlockSpec((CH, 1024), row), pl.BlockSpec((CH, 128), row),
                  pl.BlockSpec((1, 128), full), pl.BlockSpec((1, 256), full)],
        out_specs=pl.BlockSpec((CH, 256), row),
        out_shape=jax.ShapeDtypeStruct((T, 256), BF16),
        scratch_shapes=[pltpu.VMEM((HEADS, ML_DH, 128), F32), pltpu.VMEM((HEADS, 8, 128), F32)],
        compiler_params=_params(2),
        name="mlstm",
    )(ml, gates, brow, nw)


def _ssd_kernel(ssm_ref, g_ref, brow_ref, alog_ref, cw_ref, cb_ref, d_ref, nw_ref, y_ref,
                xbuf_ref, xact_ref, st_ref, *, CH):
    @pl.when(pl.program_id(1) == 0)
    def _init():
        xbuf_ref[0:8, :] = jnp.zeros((8, 512), F32)
        st_ref[...] = jnp.zeros_like(st_ref)

    xbuf_ref[8:CH + 8, :] = ssm_ref[:, 256:768].astype(F32)
    conv = cb_ref[...] + cw_ref[0:1, :] * xbuf_ref[5:5 + CH, :]
    for j in range(1, SSM_CONV):
        conv = conv + cw_ref[j:j + 1, :] * xbuf_ref[5 + j:5 + j + CH, :]
    xact_ref[...] = _silu(conv)
    xbuf_ref[0:8, :] = xbuf_ref[CH:CH + 8, :]

    L = 2 * CHUNK
    causal, tril, triu = _tri_consts(L)
    lane = lax.broadcasted_iota(jnp.int32, (1, 128), 1)
    a_row = jnp.where((lane >= G_DT) & (lane < G_DT + HEADS), -jnp.exp(alog_ref[...]), 0.0)
    dskip = d_ref[...]
    nw = nw_ref[...]

    def pair(pi, carry):
        rr = pl.multiple_of(pi * L, L)
        dt2 = _softplus(g_ref[pl.ds(rr, L), :] + brow_ref[...])
        a2 = dt2 * a_row
        a2_t = a2.T
        acs_all = jnp.dot(tril, a2, precision=HI, preferred_element_type=F32)
        acs_t = jnp.dot(a2_t[G_DT:G_DT + 8, :], triu, precision=HI, preferred_element_type=F32)
        cb = []
        bmat = []
        cmat = []
        for g in range(SSM_GROUPS):
            bm = xact_ref[pl.ds(rr, L), 256 + g * 64:256 + (g + 1) * 64]
            cm = xact_ref[pl.ds(rr, L), 384 + g * 64:384 + (g + 1) * 64].astype(BF16)
            bmat.append(bm)
            cmat.append(cm)
            cb.append(lax.dot_general(cm, bm.astype(BF16), NT, preferred_element_type=F32))
        gated = []
        ssq = jnp.zeros((L, 1), F32)
        for h in range(HEADS):
            g = h // (HEADS // SSM_GROUPS)
            acs_col = acs_all[:, G_DT + h:G_DT + h + 1]
            acs_row = acs_t[h:h + 1, :]
            dt_col = dt2[:, G_DT + h:G_DT + h + 1]
            xh = xact_ref[pl.ds(rr, L), h * 64:(h + 1) * 64]
            xdt = (xh * dt_col).astype(BF16)
            st = st_ref[h]
            mmat = cb[g] * jnp.exp(jnp.where(causal, acs_col - acs_row, -jnp.inf))
            y = jnp.dot(mmat.astype(BF16), xdt, preferred_element_type=F32)
            y = y + jnp.dot(cmat[g], st.astype(BF16), preferred_element_type=F32) * jnp.exp(acs_col)
            y = y + xh * dskip[:, h * 64:(h + 1) * 64]
            acs_last = acs_col[L - 1:L, :]
            bdec = (bmat[g] * jnp.exp(acs_last - acs_col)).astype(BF16)
            st_ref[h] = jnp.exp(acs_last) * st + lax.dot_general(bdec, xdt, TN, preferred_element_type=F32)
            z = ssm_ref[pl.ds(rr, L), h * 64:(h + 1) * 64].astype(F32)
            yg = y * _silu(z)
            ssq = ssq + jnp.sum(yg * yg, axis=1, keepdims=True)
            gated.append(yg)
        scale = lax.rsqrt(ssq / (HEADS * SSM_P) + LN_EPS)
        for h in range(HEADS):
            y_ref[pl.ds(rr, L), h * 64:(h + 1) * 64] = (gated[h] * scale * nw[:, h * 64:(h + 1) * 64]).astype(BF16)
        return carry

    lax.fori_loop(0, CH // L, pair, 0)


def _ssd(ssm, gates, brow, alog, cw, cb, dskip, nw, B, S, CH=512):
    T = ssm.shape[0]
    nS = S // CH
    row = lambda b, j: (b * nS + j, 0)
    full = lambda b, j: (0, 0)
    return pl.pallas_call(
        functools.partial(_ssd_kernel, CH=CH),
        grid=(B, nS),
        in_specs=[pl.BlockSpec((CH, 768), row), pl.BlockSpec((CH, 128), row), pl.BlockSpec((1, 128), full),
                  pl.BlockSpec((1, 128), full), pl.BlockSpec((SSM_CONV, 512), full), pl.BlockSpec((1, 512), full),
                  pl.BlockSpec((1, 256), full), pl.BlockSpec((1, 256), full)],
        out_specs=pl.BlockSpec((CH, 256), row),
        out_shape=jax.ShapeDtypeStruct((T, 256), BF16),
        scratch_shapes=[pltpu.VMEM((CH + 8, 512), F32), pltpu.VMEM((CH, 512), F32),
                        pltpu.VMEM((HEADS, SSM_N, SSM_P), F32)],
        compiler_params=_params(2),
        name="ssd",
    )(ssm, gates, brow, alog, cw, cb, dskip, nw)


def _attn_kernel(at_ref, o_ref, lse_ref, *, N):
    W = AT_SPAN
    ri = lax.broadcasted_iota(jnp.int32, (W, W), 0)
    ci = lax.broadcasted_iota(jnp.int32, (W, W), 1)
    cur_ok = ri >= ci
    prev_ok = ci >= ri
    low_lanes = ci < AT_DH
    ones = jnp.ones((W, W), BF16)

    NB = next(n for n in (4, 2, 1) if (N // W) % n == 0)

    def pair_attention(q2, kc, kp, vc, vp, pmask):
        outs, lses = [], []
        for first in (True, False):
            qh = jnp.where(low_lanes if first else jnp.logical_not(low_lanes), q2, jnp.zeros_like(q2))
            sc = jnp.where(cur_ok, lax.dot_general(qh, kc, NT, preferred_element_type=F32), -jnp.inf)
            sp = jnp.where(pmask, lax.dot_general(qh, kp, NT, preferred_element_type=F32), -jnp.inf)
            m = jnp.max(jnp.maximum(sc, sp), axis=1, keepdims=True)
            pc = jnp.exp(sc - m).astype(BF16)
            pp = jnp.exp(sp - m).astype(BF16)
            acc = jnp.dot(pc, vc, preferred_element_type=F32) + jnp.dot(pp, vp, preferred_element_type=F32)
            den = jnp.dot(pc, ones, preferred_element_type=F32) + jnp.dot(pp, ones, preferred_element_type=F32)
            outs.append(acc / den)
            lses.append(m + jnp.log(den[:, 0:1]))
        return jnp.where(low_lanes, outs[0], outs[1]).astype(BF16), jnp.where(low_lanes, lses[0], lses[1])

    def blk(i, carry):
        work = []
        for u in range(NB):
            n = i * NB + u
            r0 = pl.multiple_of(n * W, W)
            rp = pl.multiple_of(jnp.maximum(n - 1, 0) * W, W)
            for p in range(HEADS // 2):
                lanes = lambda base: slice(base + 128 * p, base + 128 * (p + 1))
                work.append((r0, p, prev_ok & (n > 0),
                             at_ref[pl.ds(r0, W), lanes(0)], at_ref[pl.ds(r0, W), lanes(256)],
                             at_ref[pl.ds(rp, W), lanes(256)], at_ref[pl.ds(r0, W), lanes(512)],
                             at_ref[pl.ds(rp, W), lanes(512)]))
        done = [(r0, p) + pair_attention(q2, kc, kp, vc, vp, pmask) for r0, p, pmask, q2, kc, kp, vc, vp in work]
        for r0, p, o2, lse2 in done:
            o_ref[pl.ds(r0, W), 128 * p:128 * (p + 1)] = o2
            lse_ref[pl.ds(r0, W), 128 * p:128 * (p + 1)] = lse2
        return carry

    assert (N // W) % NB == 0
    lax.fori_loop(0, N // (W * NB), blk, 0)


def _attn(atv, d, B, S):
    N = S // d
    o, lse = pl.pallas_call(
        functools.partial(_attn_kernel, N=N),
        grid=(B, d),
        in_specs=[pl.BlockSpec((None, N, 768), lambda b, r: (b, 0, r))],
        out_specs=[pl.BlockSpec((None, N, 256), lambda b, r: (b, 0, r)),
                   pl.BlockSpec((None, N, 256), lambda b, r: (b, 0, r))],
        out_shape=[jax.ShapeDtypeStruct((B, N, d * 256), BF16), jax.ShapeDtypeStruct((B, N, d * 256), F32)],
        compiler_params=_params(2),
        name=f"attn_d{d}",
    )(atv.reshape(B, N, d * 768))
    return o.reshape(B * N, d * 256), lse.reshape(B * N, d * 256)


def _gla_kernel(gla_ref, g_ref, w2_ref, b2_ref, nw_ref, y_ref, st_ref, *, CH):
    @pl.when(pl.program_id(1) == 0)
    def _init():
        st_ref[...] = jnp.zeros_like(st_ref)

    L = CHUNK
    SB = 16
    ri = lax.broadcasted_iota(jnp.int32, (L, L), 0)
    ci = lax.broadcasted_iota(jnp.int32, (L, L), 1)
    tril_blk = ((ri >= ci) & (ri // SB == ci // SB)).astype(F32)
    tl = lax.broadcasted_iota(jnp.int32, (L, 128), 0) % SB
    er = lax.broadcasted_iota(jnp.int32, (128, 256), 0) // GLA_DK
    ec = lax.broadcasted_iota(jnp.int32, (128, 256), 1) // GLA_DV
    head_expand = (er == ec).astype(BF16)
    sr = lax.broadcasted_iota(jnp.int32, (256, 128), 0) // GLA_DV
    sc = lax.broadcasted_iota(jnp.int32, (256, 128), 1) // GLA_DK
    st_mask = (sr == sc).astype(F32)
    nw = nw_ref[...]

    def chunk(ci_, st):
        rr = pl.multiple_of(ci_ * L, L)
        lg = _log_sigmoid(jnp.dot(g_ref[pl.ds(rr, L), :], w2_ref[...], preferred_element_type=F32)
                          + b2_ref[...]) / GLA_TAU
        c = jnp.dot(tril_blk, lg, precision=HI, preferred_element_type=F32)
        q = gla_ref[pl.ds(rr, L), 0:128].astype(F32) * GLA_DK ** -0.5
        k = gla_ref[pl.ds(rr, L), 128:256].astype(F32)
        vb = gla_ref[pl.ds(rr, L), 256:512]
        v = vb.astype(F32)
        c4 = c.reshape(L // SB, SB, 128)
        k4 = k.reshape(L // SB, SB, 128)
        v4 = v.reshape(L // SB, SB, 256)

        def bcast(x4, j, width):
            return jnp.broadcast_to(x4[:, j:j + 1, :], (L // SB, SB, width)).reshape(L, width)

        o = jnp.zeros((L, 256), F32)
        for j in range(SB):
            dec = jnp.exp(jnp.minimum(c - bcast(c4, j, 128), 0.0))
            p = jnp.where(tl >= j, q * dec * bcast(k4, j, 128), 0.0)
            a = jnp.dot(p.astype(BF16), head_expand, preferred_element_type=F32)
            o = o + a * bcast(v4, j, 256)

        outs = []
        for i in range(L // SB):
            cblk = c[i * SB:(i + 1) * SB, :]
            clast = cblk[SB - 1:SB, :]
            qe = (q[i * SB:(i + 1) * SB, :] * jnp.exp(cblk)).astype(BF16)
            outs.append(o[i * SB:(i + 1) * SB, :] + lax.dot_general(qe, st.astype(BF16), NT,
                                                                    preferred_element_type=F32))
            ke = (k[i * SB:(i + 1) * SB, :] * jnp.exp(clast - cblk)).astype(BF16)
            upd = lax.dot_general(vb[i * SB:(i + 1) * SB, :], ke, TN, preferred_element_type=F32)
            st = st * jnp.exp(clast) + st_mask * upd
        for i in range(L // SB):
            rg = gla_ref[pl.ds(pl.multiple_of(rr + i * SB, SB), SB), 512:768].astype(F32)
            for h in range(HEADS):
                oh = outs[i][:, h * 64:(h + 1) * 64]
                rms = lax.rsqrt(jnp.mean(oh * oh, axis=1, keepdims=True) + LN_EPS)
                y_ref[pl.ds(pl.multiple_of(rr + i * SB, SB), SB), h * 64:(h + 1) * 64] = (
                    oh * rms * nw[:, h * 64:(h + 1) * 64] * _silu(rg[:, h * 64:(h + 1) * 64])).astype(BF16)
        return st

    st_ref[...] = lax.fori_loop(0, CH // L, chunk, st_ref[...], unroll=True)


def _gla(gla, gates, w2, b2, nw, B, S, CH=512):
    T = gla.shape[0]
    nS = S // CH
    row = lambda b, j: (b * nS + j, 0)
    full = lambda b, j: (0, 0)
    return pl.pallas_call(
        functools.partial(_gla_kernel, CH=CH),
        grid=(B, nS),
        in_specs=[pl.BlockSpec((CH, 768), row), pl.BlockSpec((CH, 128), row), pl.BlockSpec((128, 128), full),
                  pl.BlockSpec((1, 128), full), pl.BlockSpec((1, 256), full)],
        out_specs=pl.BlockSpec((CH, 256), row),
        out_shape=jax.ShapeDtypeStruct((T, 256), BF16),
        scratch_shapes=[pltpu.VMEM((HEADS * GLA_DV, HEADS * GLA_DK), F32)],
        compiler_params=_params(2),
        name="gla",
    )(gla, gates, w2, b2, nw)


def _out_proj_kernel(yml_ref, yssm_ref, o1_ref, o4_ref, o16_ref, l1_ref, l4_ref, l16_ref, ygla_ref, x_ref,
                     wo_ref, g_ref, b_ref, x1_ref, os_ref, ls_ref):
    tm = x_ref.shape[0]
    for i, (d, o_ref, l_ref) in enumerate(((DILATIONS[1], o4_ref, l4_ref), (DILATIONS[2], o16_ref, l16_ref))):
        for r in range(d):
            for j in range(2):
                cols = slice(256 * r + 128 * j, 256 * r + 128 * (j + 1))
                os_ref[i, j, pl.ds(r, tm // d, stride=d), :] = o_ref[:, cols].astype(F32)
                ls_ref[i, j, pl.ds(r, tm // d, stride=d), :] = l_ref[:, cols]
    yat = []
    for j in range(2):
        o1 = o1_ref[:, 128 * j:128 * (j + 1)].astype(F32)
        l1 = l1_ref[:, 128 * j:128 * (j + 1)]
        l4, l16 = ls_ref[0, j], ls_ref[1, j]
        mx = jnp.maximum(jnp.maximum(l1, l4), l16)
        e1, e4, e16 = jnp.exp(l1 - mx), jnp.exp(l4 - mx), jnp.exp(l16 - mx)
        yat.append(((e1 * o1 + e4 * os_ref[0, j] + e16 * os_ref[1, j]) / (e1 + e4 + e16)).astype(BF16))
    acc = jnp.dot(yml_ref[...], wo_ref[0:256, :], preferred_element_type=F32)
    acc = acc + jnp.dot(yssm_ref[...], wo_ref[256:512, :], preferred_element_type=F32)
    acc = acc + jnp.dot(yat[0], wo_ref[512:640, :], preferred_element_type=F32)
    acc = acc + jnp.dot(yat[1], wo_ref[640:768, :], preferred_element_type=F32)
    acc = acc + jnp.dot(ygla_ref[...], wo_ref[768:1024, :], preferred_element_type=F32)
    x1_ref[...] = _layer_norm(ALPHA * x_ref[...] + acc, g_ref[...], b_ref[...])


def _out_proj(yml, yssm, branches, ygla, x, wo, g, b, tm=512):
    T, D = x.shape
    row = lambda i: (i, 0)
    full = lambda i: (0, 0)
    small = pl.BlockSpec((tm, 256), row)
    dil = [pl.BlockSpec((tm // d, 256 * d), row) for d in DILATIONS]
    return pl.pallas_call(
        _out_proj_kernel,
        grid=(T // tm,),
        in_specs=[small, small] + dil + dil + [small, pl.BlockSpec((tm, D), row), pl.BlockSpec(wo.shape, full),
                                               pl.BlockSpec((1, D), full), pl.BlockSpec((1, D), full)],
        out_specs=pl.BlockSpec((tm, D), row),
        out_shape=jax.ShapeDtypeStruct((T, D), F32),
        scratch_shapes=[pltpu.VMEM((2, 2, tm, 128), F32), pltpu.VMEM((2, 2, tm, 128), F32)],
        compiler_params=_params(1),
        name="out_proj_ln",
    )(yml, yssm, *[o for o, _ in branches], *[l for _, l in branches], ygla, x, wo, g, b)


def _router_kernel(x_ref, wr_ref, rb_ref, row_ref, col_ref, cnt_ref, rt_ref):
    tm = x_ref.shape[0]
    gsz = N_EXPERTS // N_GROUPS
    logits = lax.dot_general(wr_ref[...], x_ref[...], NT, precision=HI, preferred_element_type=F32)
    scores = jax.nn.sigmoid(logits)
    sel = scores + rb_ref[...]
    eidx = lax.broadcasted_iota(jnp.int32, (gsz, tm), 0)
    big = jnp.int32(1 << 20)
    neg = -jnp.inf

    sel_g = [sel[g * gsz:(g + 1) * gsz, :] for g in range(N_GROUPS)]
    idx_g = [eidx + g * gsz for g in range(N_GROUPS)]
    gscore = []
    for g in range(N_GROUPS):
        v = sel_g[g]
        m1 = jnp.max(v, axis=0, keepdims=True)
        i1 = jnp.min(jnp.where(v == m1, idx_g[g], big), axis=0, keepdims=True)
        m2 = jnp.max(jnp.where(idx_g[g] == i1, neg, v), axis=0, keepdims=True)
        gscore.append(m1 + m2)
    gkeep = [jnp.zeros((1, tm), jnp.bool_) for _ in range(N_GROUPS)]
    for _ in range(TOPK_GROUPS):
        m = functools.reduce(jnp.maximum, gscore)
        gi = functools.reduce(jnp.minimum, [jnp.where(gscore[g] == m, g, big) for g in range(N_GROUPS)])
        for g in range(N_GROUPS):
            hit = gi == g
            gkeep[g] = gkeep[g] | hit
            gscore[g] = jnp.where(hit, neg, gscore[g])
    cand = [jnp.where(gkeep[g], sel_g[g], neg) for g in range(N_GROUPS)]
    chosen = [jnp.zeros((gsz, tm), jnp.bool_) for _ in range(N_GROUPS)]
    picks = []
    for _ in range(TOP_K):
        m = functools.reduce(jnp.maximum, [jnp.max(c, axis=0, keepdims=True) for c in cand])
        ei = functools.reduce(jnp.minimum, [jnp.min(jnp.where(cand[g] == m, idx_g[g], big), axis=0, keepdims=True)
                                            for g in range(N_GROUPS)])
        picks.append(ei)
        for g in range(N_GROUPS):
            hit = idx_g[g] == ei
            chosen[g] = chosen[g] | hit
            cand[g] = jnp.where(hit, neg, cand[g])
    picked = [jnp.where(chosen[g], scores[g * gsz:(g + 1) * gsz, :], 0.0) for g in range(N_GROUPS)]
    tot = functools.reduce(jnp.add, [jnp.sum(p, axis=0, keepdims=True) for p in picked])
    gates = [p / tot * ROUTED_SCALE for p in picked]

    chosen_b = jnp.concatenate([c.astype(F32) for c in chosen] + [jnp.zeros((N_EXPERTS, tm), F32)], axis=0).astype(BF16)
    ti = lax.broadcasted_iota(jnp.int32, (tm, tm), 0)
    tj = lax.broadcasted_iota(jnp.int32, (tm, tm), 1)
    before = ((ti < tj) & (ti // TOK_TILE == tj // TOK_TILE)).astype(BF16)
    rank = jnp.dot(chosen_b[0:N_EXPERTS, :], before, preferred_element_type=F32)
    lower = (lax.broadcasted_iota(jnp.int32, (N_EXPERTS, N_EXPERTS), 0)
             > lax.broadcasted_iota(jnp.int32, (N_EXPERTS, N_EXPERTS), 1)).astype(F32)
    tile_of_lane = lax.broadcasted_iota(jnp.int32, (1, tm), 1) // TOK_TILE
    posmat = rank
    for t in range(tm // TOK_TILE):
        in_tile = chosen_b[:, t * TOK_TILE:(t + 1) * TOK_TILE]
        cnt = jnp.dot(in_tile[0:N_EXPERTS, :], jnp.ones((TOK_TILE, 128), BF16), preferred_element_type=F32)
        padded = jnp.maximum(jnp.floor((cnt + (ROW_PAD - 1)) * (1.0 / ROW_PAD)) * ROW_PAD, float(ROW_PAD))
        gstart = jnp.dot(lower, padded, precision=HI, preferred_element_type=F32)
        posmat = posmat + jnp.where(tile_of_lane == t, gstart[:, 0:1], 0.0)
        cnt_ref[8 * t:8 * (t + 1), :] = lax.dot_general(jnp.ones((8, TOK_TILE), BF16), in_tile, NT,
                                                        preferred_element_type=F32)
    rt_ref[...] = jnp.zeros_like(rt_ref)
    for k in range(TOP_K):
        pos_k = jnp.zeros((1, tm), F32)
        gate_k = jnp.zeros((1, tm), F32)
        for g in range(N_GROUPS):
            hit = idx_g[g] == picks[k]
            pos_k = pos_k + jnp.sum(jnp.where(hit, posmat[g * gsz:(g + 1) * gsz, :], 0.0), axis=0, keepdims=True)
            gate_k = gate_k + jnp.sum(jnp.where(hit, gates[g], 0.0), axis=0, keepdims=True)
        rt_ref[k:k + 1, :] = pos_k
        rt_ref[TOP_K + k:TOP_K + k + 1, :] = gate_k
    row_ref[...] = rt_ref[0:2 * TOP_K, :]
    col_ref[...] = rt_ref[...].T


def _router(x1, wr_t, rb):
    T, D = x1.shape
    tm = ROUTER_TILE
    sub = tm // TOK_TILE
    return pl.pallas_call(
        _router_kernel,
        grid=(T // tm,),
        in_specs=[pl.BlockSpec((tm, D), lambda i: (i, 0)), pl.BlockSpec(wr_t.shape, lambda i: (0, 0)),
                  pl.BlockSpec(rb.shape, lambda i: (0, 0))],
        out_specs=[pl.BlockSpec((2 * TOP_K, tm), lambda i: (0, i)), pl.BlockSpec((tm, 128), lambda i: (i, 0)),
                   pl.BlockSpec((8 * sub, 128), lambda i: (i, 0))],
        out_shape=[jax.ShapeDtypeStruct((2 * TOP_K, T), F32), jax.ShapeDtypeStruct((T, 128), F32),
                   jax.ShapeDtypeStruct((T // TOK_TILE * 8, 128), F32)],
        scratch_shapes=[pltpu.VMEM((128, tm), F32)],
        compiler_params=_params(1),
        name="router",
    )(x1, wr_t, rb)


M_LSTART, M_ROWS, M_GOFF = 0, N_EXPERTS, 2 * N_EXPERTS
M_USED, M_USED_PREV = 3 * N_EXPERTS, 3 * N_EXPERTS + 1
M_COLS = 4 * N_EXPERTS


def _ceil_div(x, n):
    assert n & (n - 1) == 0
    return lax.shift_right_logical(x + (n - 1), n.bit_length() - 1)


def _seg_copy(meta_ref, e, loc_ref, glob_ref, sem, to_global):
    n = pl.multiple_of(meta_ref[0, 0, M_ROWS + e] * ROW_PAD, ROW_PAD)
    ls = pl.multiple_of(meta_ref[0, 0, M_LSTART + e] * ROW_PAD, ROW_PAD)
    go = pl.multiple_of(meta_ref[0, 0, M_GOFF + e] * ROW_PAD, ROW_PAD)
    loc, glob = loc_ref.at[pl.ds(ls, n)], glob_ref.at[pl.ds(go, n)]
    return pltpu.make_async_copy(loc, glob, sem) if to_global else pltpu.make_async_copy(glob, loc, sem)


def _dispatch_kernel(meta_ref, tail_ref, x_ref, row_ref, xg_ref, loc_ref, zero_ref, sems):
    s = pl.program_id(0)
    last = pl.num_programs(0) - 1
    slot = s % 2
    tm = x_ref.shape[0]

    @pl.when(s == 0)
    def _fill_tails():
        zero_ref[...] = jnp.zeros_like(zero_ref)

        def start(e, c):
            n = pl.multiple_of(tail_ref[1, e] * ROW_PAD, ROW_PAD)
            go = pl.multiple_of(tail_ref[0, e] * ROW_PAD, ROW_PAD)

            @pl.when(n > 0)
            def _():
                pltpu.make_async_copy(zero_ref.at[pl.ds(0, n)], xg_ref.at[pl.ds(go, n)], sems.at[2]).start()
            return c
        lax.fori_loop(0, N_EXPERTS, start, 0)

    RC = 256
    xb = x_ref[...].astype(BF16)
    pos = row_ref[0:TOP_K, :]
    hi = jnp.floor(pos * (1.0 / RC))
    lo = pos - hi * RC
    rows = lax.broadcasted_iota(jnp.int32, (RC, tm), 0).astype(F32)
    lo_hit = [jnp.where(lo[k:k + 1, :] == rows, 1.0, 0.0).astype(BF16) for k in range(TOP_K)]
    used = meta_ref[0, 0, M_USED]

    def chunk(c, carry):
        cf = c.astype(F32)
        onehot = lo_hit[0] * jnp.where(hi[0:1, :] == cf, 1.0, 0.0).astype(BF16)
        for k in range(1, TOP_K):
            onehot = onehot + lo_hit[k] * jnp.where(hi[k:k + 1, :] == cf, 1.0, 0.0).astype(BF16)
        loc_ref[slot, pl.ds(pl.multiple_of(c * RC, RC), RC), :] = jnp.dot(
            onehot, xb, preferred_element_type=F32).astype(BF16)
        return carry
    lax.fori_loop(0, _ceil_div(used, RC // ROW_PAD), chunk, 0)

    def start(e, c):
        _seg_copy(meta_ref, e, loc_ref.at[slot], xg_ref, sems.at[slot], True).start()
        return c
    lax.fori_loop(0, N_EXPERTS, start, 0)

    def wait_all(total_col, buf):
        n = pl.multiple_of(meta_ref[0, 0, total_col] * ROW_PAD, ROW_PAD)

        @pl.when(n > 0)
        def _():
            pltpu.make_async_copy(loc_ref.at[buf, pl.ds(0, n)], xg_ref.at[pl.ds(0, n)], sems.at[buf]).wait()

    @pl.when(s > 0)
    def _wait_prev():
        wait_all(M_USED_PREV, 1 - slot)

    @pl.when(s == last)
    def _wait_own():
        wait_all(M_USED, slot)

    @pl.when(s == 0)
    def _wait_tails():
        def wait(e, c):
            n = pl.multiple_of(tail_ref[1, e] * ROW_PAD, ROW_PAD)
            go = pl.multiple_of(tail_ref[0, e] * ROW_PAD, ROW_PAD)

            @pl.when(n > 0)
            def _():
                pltpu.make_async_copy(zero_ref.at[pl.ds(0, n)], xg_ref.at[pl.ds(go, n)], sems.at[2]).wait()
            return c
        lax.fori_loop(0, N_EXPERTS, wait, 0)


def _dispatch(meta, tail, x1, rowform, p_rows):
    T, D = x1.shape
    tm = TOK_TILE
    return pl.pallas_call(
        _dispatch_kernel,
        grid=(T // tm,),
        in_specs=[pl.BlockSpec((1, 1, M_COLS), lambda i: (i, 0, 0), memory_space=pltpu.SMEM),
                  pl.BlockSpec(memory_space=pltpu.SMEM),
                  pl.BlockSpec((tm, D), lambda i: (i, 0)), pl.BlockSpec((2 * TOP_K, tm), lambda i: (0, i))],
        out_specs=pl.BlockSpec(memory_space=pl.ANY),
        out_shape=jax.ShapeDtypeStruct((p_rows, D), BF16),
        scratch_shapes=[pltpu.VMEM((2, LOC_ROWS, D), BF16), pltpu.VMEM((FFN_BLK, D), BF16),
                        pltpu.SemaphoreType.DMA((3,))],
        compiler_params=_params(1),
        name="moe_dispatch",
    )(meta, tail, x1, rowform)


def _ffn_kernel(bexp_ref, nused_ref, x_ref, wg_ref, wu_ref, wd_ref, y_ref, wgb_ref, wub_ref, wdb_ref):
    i = pl.program_id(0)

    @pl.when(i < nused_ref[0])
    def _():
        @pl.when((i == 0) | (bexp_ref[i] != bexp_ref[jnp.maximum(i - 1, 0)]))
        def _new_expert():
            wgb_ref[...] = wg_ref[...].astype(BF16)
            wub_ref[...] = wu_ref[...].astype(BF16)
            wdb_ref[...] = wd_ref[...].astype(BF16)

        xb = x_ref[...]
        a = jnp.dot(xb, wgb_ref[...], preferred_element_type=F32)
        u = jnp.dot(xb, wub_ref[...], preferred_element_type=F32)
        y_ref[...] = jnp.dot((_silu(a) * u).astype(BF16), wdb_ref[...], preferred_element_type=F32).astype(BF16)


def _ffn(blk_exp, nused, xg, wg, wu, wd, layer):
    P, D = xg.shape
    F = wg.shape[3]
    blk = lambda i, be, nu: (jnp.maximum(jnp.minimum(i, nu[0] - 1), 0), 0)
    wsel = lambda i, be, nu: (layer, be[jnp.maximum(jnp.minimum(i, nu[0] - 1), 0)], 0, 0)
    return pl.pallas_call(
        _ffn_kernel,
        grid_spec=pltpu.PrefetchScalarGridSpec(
            num_scalar_prefetch=2,
            grid=(P // FFN_BLK,),
            in_specs=[pl.BlockSpec((FFN_BLK, D), blk), pl.BlockSpec((None, None, D, F), wsel),
                      pl.BlockSpec((None, None, D, F), wsel), pl.BlockSpec((None, None, F, D), wsel)],
            out_specs=pl.BlockSpec((FFN_BLK, D), blk),
            scratch_shapes=[pltpu.VMEM((D, F), BF16), pltpu.VMEM((D, F), BF16), pltpu.VMEM((F, D), BF16)]),
        out_shape=jax.ShapeDtypeStruct((P, D), BF16),
        compiler_params=_params(1),
        name="moe_ffn",
    )(blk_exp, nused, xg, wg, wu, wd)


def _combine_kernel(meta_ref, next_ref, x_ref, col_ref, yg_ref, sg_ref, su_ref, sd_ref, g_ref, b_ref, x2_ref,
                    loc_ref, z_ref, sems):
    s = pl.program_id(0)
    last = pl.num_programs(0) - 1
    slot = s % 2
    tm = x_ref.shape[0]
    CW = 512
    cw_u = CW // ROW_PAD

    def used_rows(m_ref):
        return m_ref[0, 0, M_USED]

    def fetch(m_ref, buf):
        def start(e, c):
            _seg_copy(m_ref, e, loc_ref.at[buf], yg_ref, sems.at[buf], False).start()
            return c
        lax.fori_loop(0, N_EXPERTS, start, 0)
        used = used_rows(m_ref)

        def clear(c, carry):
            loc_ref[buf, pl.ds(pl.multiple_of(c * ROW_PAD, ROW_PAD), ROW_PAD), :] = jnp.zeros(
                (ROW_PAD, loc_ref.shape[2]), BF16)
            return carry
        lax.fori_loop(used, _ceil_div(used, cw_u) * cw_u, clear, 0)

    @pl.when(s == 0)
    def _first():
        fetch(meta_ref, 0)

    @pl.when(s < last)
    def _prefetch():
        fetch(next_ref, 1 - slot)

    x1 = x_ref[...]
    xb = x1.astype(BF16)
    hid = _silu(jnp.dot(xb, sg_ref[...], preferred_element_type=F32)) * jnp.dot(xb, su_ref[...],
                                                                                preferred_element_type=F32)
    z_ref[...] = ALPHA * x1 + jnp.dot(hid.astype(BF16), sd_ref[...], preferred_element_type=F32)

    pos = col_ref[:, 0:TOP_K]
    gate = col_ref[:, TOP_K:2 * TOP_K]
    hi = jnp.floor(pos * (1.0 / CW))
    lo = pos - hi * CW
    lanes = lax.broadcasted_iota(jnp.int32, (tm, CW), 1).astype(F32)
    lo_hit = [jnp.where(lo[:, k:k + 1] == lanes, 1.0, 0.0).astype(BF16) for k in range(TOP_K)]
    n_own = pl.multiple_of(used_rows(meta_ref) * ROW_PAD, ROW_PAD)

    @pl.when(n_own > 0)
    def _wait_own():
        pltpu.make_async_copy(yg_ref.at[pl.ds(0, n_own)], loc_ref.at[slot, pl.ds(0, n_own)], sems.at[slot]).wait()

    def chunk(c, carry):
        cf = c.astype(F32)
        wmat = lo_hit[0] * jnp.where(hi[:, 0:1] == cf, gate[:, 0:1], 0.0).astype(BF16)
        for k in range(1, TOP_K):
            wmat = wmat + lo_hit[k] * jnp.where(hi[:, k:k + 1] == cf, gate[:, k:k + 1], 0.0).astype(BF16)
        rows = loc_ref[slot, pl.ds(pl.multiple_of(c * CW, CW), CW), :]
        z_ref[...] += jnp.dot(wmat, rows, preferred_element_type=F32)
        return carry
    lax.fori_loop(0, _ceil_div(used_rows(meta_ref), cw_u), chunk, 0)
    x2_ref[...] = _layer_norm(z_ref[...], g_ref[...], b_ref[...])


def _combine(meta, x1, colform, yg, sg, su, sd, g, b):
    T, D = x1.shape
    tm = TOK_TILE
    nst = T // tm
    full = lambda i: (0, 0)
    mspec = lambda f: pl.BlockSpec((1, 1, M_COLS), f, memory_space=pltpu.SMEM)
    return pl.pallas_call(
        _combine_kernel,
        grid=(nst,),
        in_specs=[mspec(lambda i: (i, 0, 0)), mspec(lambda i: (jnp.minimum(i + 1, nst - 1), 0, 0)),
                  pl.BlockSpec((tm, D), lambda i: (i, 0)), pl.BlockSpec((tm, 128), lambda i: (i, 0)),
                  pl.BlockSpec(memory_space=pl.ANY),
                  pl.BlockSpec(sg.shape, full), pl.BlockSpec(su.shape, full), pl.BlockSpec(sd.shape, full),
                  pl.BlockSpec((1, D), full), pl.BlockSpec((1, D), full)],
        out_specs=pl.BlockSpec((tm, D), lambda i: (i, 0)),
        out_shape=jax.ShapeDtypeStruct((T, D), F32),
        scratch_shapes=[pltpu.VMEM((2, LOC_ROWS, D), BF16), pltpu.VMEM((tm, D), F32),
                        pltpu.SemaphoreType.DMA((2,))],
        compiler_params=_params(1),
        name="moe_combine_ln",
    )(meta, meta, x1, colform, yg, sg, su, sd, g, b)


def _moe_plan(cnt_out, T):
    nst = T // TOK_TILE
    blk_u = FFN_BLK // ROW_PAD
    cnt = cnt_out.reshape(nst, 8, 128)[:, 0, :N_EXPERTS].astype(jnp.int32)
    rows = jnp.maximum((cnt + ROW_PAD - 1) // ROW_PAD, 1)
    lstart = jnp.cumsum(rows, axis=1) - rows
    tot = rows.sum(axis=0)
    tot_pad = (tot + blk_u - 1) // blk_u * blk_u
    eend = jnp.cumsum(tot_pad)
    ebase = eend - tot_pad
    goff = ebase[None, :] + jnp.cumsum(rows, axis=0) - rows
    used = rows.sum(axis=1, keepdims=True)
    used_prev = jnp.concatenate([jnp.zeros((1, 1), jnp.int32), used[:-1]], axis=0)
    fill = jnp.zeros((nst, M_COLS - M_USED_PREV - 1), jnp.int32)
    meta = jnp.concatenate([lstart, rows, goff, used, used_prev, fill], axis=1).reshape(nst, 1, M_COLS)
    tail = jnp.stack([ebase + tot, tot_pad - tot])
    nblk = _moe_rows(T) // FFN_BLK
    nused = (eend[-1] // blk_u).reshape(1)
    first_row = jnp.arange(nblk, dtype=jnp.int32) * blk_u
    blk_exp = jnp.minimum(jnp.sum(first_row[:, None] >= eend[None, :], axis=1), N_EXPERTS - 1).astype(jnp.int32)
    return meta, tail, blk_exp, nused


def _moe_rows(T):
    worst = T * TOP_K + (T // TOK_TILE) * N_EXPERTS * ROW_PAD + N_EXPERTS * (FFN_BLK - ROW_PAD)
    return -(-worst // FFN_BLK) * FFN_BLK


def _rope_tables(S):
    half = AT_DH // 2
    lane = jnp.arange(128)
    inv = ROPE_THETA ** (-(lane % half).astype(F32) / half)
    ang = jnp.arange(S, dtype=F32)[:, None] * inv[None, :]
    sign = jnp.where((lane % AT_DH) < half, -1.0, 1.0).astype(F32)
    return jnp.cos(ang), jnp.sin(ang) * sign[None, :]


def _pad_cols(a, width):
    return jnp.pad(a, ((0, 0), (0, width - a.shape[1])))


def kernel(x, w_in, ml_i_bias, ml_f_bias, ml_norm_w, ssm_conv_w, ssm_conv_b, ssm_dt_bias, ssm_a_log, ssm_d, ssm_norm_w, gla_gate_w2, gla_gate_b, gla_norm_w, w_out, ln1_g, ln1_b, router_w, router_bias, exp_w_gate, exp_w_up, exp_w_down, sh_w_gate, sh_w_up, sh_w_down, ln2_g, ln2_b):
    B, S, D = x.shape
    T = B * S
    depth = w_in.shape[0]
    assert D == 1024 and S % (AT_SPAN * max(DILATIONS)) == 0 and T % 1024 == 0
    cos, sin = _rope_tables(S)
    xf = x.reshape(T, D)
    for l in range(depth):
        wm, wgt = _regroup_weights(w_in, l)
        brow = _pad_cols(jnp.concatenate([ml_i_bias[l], ml_f_bias[l], ssm_dt_bias[l]])[None, :], 128).astype(F32)
        alog = jnp.pad(ssm_a_log[l].astype(F32), (G_DT, 128 - G_DT - HEADS))[None, :]
        w2 = jnp.pad(gla_gate_w2[l].astype(F32), ((G_GA, 128 - G_GA - GLA_RANK), (0, 0)))

        ml, ssm, at1, at4, at16, gla, gates = _in_proj(xf, wm, wgt, cos, sin, S)
        y_ml = _mlstm(ml, gates, brow, ml_norm_w[l][None, :].astype(F32), B, S)
        y_ssm = _ssd(ssm, gates, brow, alog, ssm_conv_w[l].astype(F32), ssm_conv_b[l][None, :].astype(F32),
                     jnp.repeat(ssm_d[l].astype(F32), SSM_P)[None, :], ssm_norm_w[l][None, :].astype(F32), B, S)
        branches = [_attn(atv, d, B, S) for atv, d in zip((at1, at4, at16), DILATIONS)]
        y_gla = _gla(gla, gates, w2, gla_gate_b[l][None, :].astype(F32), gla_norm_w[l][None, :].astype(F32), B, S)
        x1 = _out_proj(y_ml, y_ssm, branches, y_gla, xf,
                       w_out[l].astype(BF16), ln1_g[l][None, :].astype(F32), ln1_b[l][None, :].astype(F32))
        rowform, colform, cnt = _router(x1, router_w[l].T.astype(F32), router_bias[l][:, None].astype(F32))
        meta, tail, blk_exp, nused = _moe_plan(cnt, T)
        xg = _dispatch(meta, tail, x1, rowform, _moe_rows(T))
        yg = _ffn(blk_exp, nused, xg, exp_w_gate, exp_w_up, exp_w_down, l)
        xf = _combine(meta, x1, colform, yg, sh_w_gate[l].astype(BF16), sh_w_up[l].astype(BF16),
                      sh_w_down[l].astype(BF16), ln2_g[l][None, :].astype(F32), ln2_b[l][None, :].astype(F32))
    return xf.reshape(B, S, D)
```

```python
import functools

import jax
import jax.numpy as jnp
from jax import lax
from jax.experimental import pallas as pl
from jax.experimental.pallas import tpu as pltpu

F32 = jnp.float32
BF16 = jnp.bfloat16
HI = lax.Precision.HIGHEST
NT = (((1,), (1,)), ((), ()))
TN = (((0,), (0,)), ((), ()))

DEPTH = 4
HEADS = 4
ML_DH = 64
SSM_P = 64
SSM_N = 64
SSM_GROUPS = 2
SSM_CONV = 4
AT_DH = 64
DILATIONS = (1, 4, 16)
AT_SPAN = 128
ROPE_THETA = 10000.0
GLA_DK = 32
GLA_DV = 64
GLA_RANK = 16
GLA_TAU = 16.0
N_EXPERTS = 64
TOP_K = 8
N_GROUPS = 8
TOPK_GROUPS = 4
ROUTED_SCALE = 2.5
ALPHA = (2 * DEPTH) ** 0.25
LN_EPS = 1e-5
CHUNK = 64
TOK_TILE = 256
ROUTER_TILE = 512
ROW_PAD = 16
FFN_BLK = 1024
LOC_ROWS = -(-(TOK_TILE * TOP_K + N_EXPERTS * ROW_PAD) // 512) * 512

G_MI, G_MF, G_DT, G_GA = 0, 4, 8, 12

VMEM_LIMIT = 48 * 1024 * 1024


def _log_sigmoid(x):
    return jnp.minimum(x, 0.0) - jnp.log(1.0 + jnp.exp(-jnp.abs(x)))


def _softplus(x):
    return jnp.maximum(x, 0.0) + jnp.log(1.0 + jnp.exp(-jnp.abs(x)))


def _silu(x):
    return x * jax.nn.sigmoid(x)


def _layer_norm(z, g, b):
    mu = jnp.mean(z, axis=-1, keepdims=True)
    zc = z - mu
    var = jnp.mean(zc * zc, axis=-1, keepdims=True)
    return zc * lax.rsqrt(var + LN_EPS) * g + b


def _params(n_axes):
    return pltpu.CompilerParams(dimension_semantics=("arbitrary",) * n_axes, vmem_limit_bytes=VMEM_LIMIT)


W_MI, W_SZ, W_SDT, W_AQ, W_GQ, W_GA = 1024, 1032, 1800, 1804, 2572, 3340
W_MAIN = 3328


def _regroup_kernel(w_ref, wm_ref, wg_ref):
    RB = 256
    for c in range(0, w_ref.shape[0], RB):
        rows = slice(c, c + RB)
        wm_ref[rows, 0:1024] = w_ref[rows, 0:W_MI].astype(BF16)
        wm_ref[rows, 1024:1792] = w_ref[rows, W_SZ:W_SDT].astype(BF16)
        wm_ref[rows, 1792:2560] = w_ref[rows, W_AQ:W_GQ].astype(BF16)
        wm_ref[rows, 2560:W_MAIN] = w_ref[rows, W_GQ:W_GA].astype(BF16)
        wg_ref[rows, :] = jnp.zeros((RB, 128), BF16)
        wg_ref[rows, G_MI:G_DT] = w_ref[rows, W_MI:W_SZ].astype(BF16)
        wg_ref[rows, G_DT:G_GA] = w_ref[rows, W_SDT:W_AQ].astype(BF16)
        wg_ref[rows, G_GA:G_GA + GLA_RANK] = w_ref[rows, W_GA:W_GA + GLA_RANK].astype(BF16)


def _regroup_weights(w_in, layer):
    _, D, N = w_in.shape
    full = lambda i: (0, 0)
    return pl.pallas_call(
        _regroup_kernel,
        grid=(1,),
        in_specs=[pl.BlockSpec((None, D, N), lambda i: (layer, 0, 0))],
        out_specs=[pl.BlockSpec((D, W_MAIN), full), pl.BlockSpec((D, 128), full)],
        out_shape=[jax.ShapeDtypeStruct((D, W_MAIN), BF16), jax.ShapeDtypeStruct((D, 128), BF16)],
        compiler_params=_params(1),
        name="regroup_w_in",
    )(w_in)


def _in_proj_kernel(x_ref, wm_ref, wg_ref, cos_ref, sin_ref, ml_ref, ssm_ref, at1_ref, at4_ref, at16_ref, gla_ref,
                    g_ref, atf_ref):
    at_refs = (at1_ref, at4_ref, at16_ref)
    xb = x_ref[...].astype(BF16)

    def mm(lo):
        return jnp.dot(xb, wm_ref[:, lo:lo + 256], preferred_element_type=F32)

    for c in range(4):
        a = mm(256 * c)
        if c == 1:
            a = a * ML_DH ** -0.5
        ml_ref[:, 256 * c:256 * (c + 1)] = a.astype(BF16)
    for c in range(3):
        ssm_ref[:, 256 * c:256 * (c + 1)] = mm(1024 + 256 * c).astype(BF16)

    cos = cos_ref[...]
    sin = sin_ref[...]
    lane = lax.broadcasted_iota(jnp.int32, cos.shape, 1)
    first_half = (lane % AT_DH) < AT_DH // 2

    def rope(a):
        rot = jnp.where(first_half, pltpu.roll(a, 128 - AT_DH // 2, 1), pltpu.roll(a, AT_DH // 2, 1))
        return a * cos + rot * sin

    tm = x_ref.shape[0]
    for c in range(3):
        a = mm(1792 + 256 * c)
        for hh in range(2):
            tile = a[:, 128 * hh:128 * (hh + 1)]
            if c < 2:
                tile = rope(tile) * (AT_DH ** -0.5 if c == 0 else 1.0)
            atf_ref[2 * c + hh] = tile
    for d, ref in zip(DILATIONS, at_refs):
        for r in range(d):
            for j in range(6):
                rows = atf_ref[j] if d == 1 else atf_ref[j, pl.ds(r, tm // d, stride=d), :]
                ref[:, 768 * r + 128 * j:768 * r + 128 * (j + 1)] = rows.astype(BF16)
    for c in range(3):
        gla_ref[:, 256 * c:256 * (c + 1)] = mm(2560 + 256 * c).astype(BF16)
    g_ref[...] = jnp.dot(xb, wg_ref[...], preferred_element_type=F32)


def _in_proj(x, wm, wg, cos, sin, S, tm=512):
    T, D = x.shape
    nS = S // tm
    row = lambda i: (i, 0)
    full = lambda i: (0, 0)
    return pl.pallas_call(
        _in_proj_kernel,
        grid=(T // tm,),
        in_specs=[pl.BlockSpec((tm, D), row), pl.BlockSpec(wm.shape, full), pl.BlockSpec(wg.shape, full),
                  pl.BlockSpec((tm, 128), lambda i: (i % nS, 0)), pl.BlockSpec((tm, 128), lambda i: (i % nS, 0))],
        out_specs=[pl.BlockSpec((tm, 1024), row), pl.BlockSpec((tm, 768), row)]
        + [pl.BlockSpec((tm // d, 768 * d), row) for d in DILATIONS]
        + [pl.BlockSpec((tm, 768), row), pl.BlockSpec((tm, 128), row)],
        out_shape=[jax.ShapeDtypeStruct((T, 1024), BF16), jax.ShapeDtypeStruct((T, 768), BF16)]
        + [jax.ShapeDtypeStruct((T // d, 768 * d), BF16) for d in DILATIONS]
        + [jax.ShapeDtypeStruct((T, 768), BF16), jax.ShapeDtypeStruct((T, 128), F32)],
        scratch_shapes=[pltpu.VMEM((6, tm, 128), F32)],
        compiler_params=_params(1),
        name="in_proj",
    )(x, wm, wg, cos, sin)


def _tri_consts(L):
    ri = lax.broadcasted_iota(jnp.int32, (L, L), 0)
    ci = lax.broadcasted_iota(jnp.int32, (L, L), 1)
    causal = ri >= ci
    return causal, causal.astype(F32), (ri <= ci).astype(F32)


def _mlstm_kernel(ml_ref, g_ref, brow_ref, nw_ref, y_ref, c_ref, m_ref, *, CH):
    @pl.when(pl.program_id(1) == 0)
    def _init():
        c_ref[...] = jnp.zeros_like(c_ref)
        m_ref[...] = jnp.full_like(m_ref, -jnp.inf)

    L = 2 * CHUNK
    causal, tril, triu = _tri_consts(L)
    lane = lax.broadcasted_iota(jnp.int32, (L, 128), 1)
    is_f = (lane >= G_MF) & (lane < G_MF + HEADS)
    ones_v = jnp.ones((L, ML_DH), BF16)
    nw = nw_ref[...]

    def pair(pi, carry):
        rr = pl.multiple_of(pi * L, L)
        g2 = g_ref[pl.ds(rr, L), :] + brow_ref[...]
        vc = jnp.where(is_f, _log_sigmoid(g2), g2)
        vtc = vc.T[0:8, :]
        b_all = jnp.dot(tril, vc, precision=HI, preferred_element_type=F32)
        bt_all = jnp.dot(vtc, triu, precision=HI, preferred_element_type=F32)
        for h in range(HEADS):
            q = ml_ref[pl.ds(rr, L), h * 64:(h + 1) * 64]
            k = ml_ref[pl.ds(rr, L), 256 + h * 64:256 + (h + 1) * 64]
            v = ml_ref[pl.ds(rr, L), 512 + h * 64:512 + (h + 1) * 64]
            og = ml_ref[pl.ds(rr, L), 768 + h * 64:768 + (h + 1) * 64].astype(F32)
            b_col = b_all[:, G_MF + h:G_MF + h + 1]
            li_col = vc[:, G_MI + h:G_MI + h + 1]
            b_row = bt_all[G_MF + h:G_MF + h + 1, :]
            li_row = vtc[G_MI + h:G_MI + h + 1, :]
            m_prev = m_ref[h][0:1, 0:1]
            cst = c_ref[h]

            log_d = jnp.where(causal, b_col - b_row + li_row, -jnp.inf)
            log_inter = b_col + m_prev
            m_t = jnp.maximum(log_inter, jnp.max(log_d, axis=1, keepdims=True))
            w_inter = jnp.exp(log_inter - m_t)
            s = lax.dot_general(q, k, NT, preferred_element_type=F32) * jnp.exp(log_d - m_t)
            qc = jnp.dot(q, cst.astype(BF16), preferred_element_type=F32)
            num = jnp.dot(s.astype(BF16), v, preferred_element_type=F32) + w_inter * qc[:, 0:64]
            den = jnp.sum(s, axis=1, keepdims=True) + w_inter * qc[:, 64:65]
            hh = num / jnp.maximum(jnp.abs(den), jnp.exp(-m_t))

            b_last = b_col[L - 1:L, :]
            log_s = b_last - b_col + li_col
            m_new = jnp.maximum(b_last + m_prev, jnp.max(log_s, axis=0, keepdims=True))
            w_c = jnp.exp(b_last + m_prev - m_new)
            kw = (k.astype(F32) * jnp.exp(log_s - m_new)).astype(BF16)
            c_ref[h, :, 0:64] = w_c * cst[:, 0:64] + lax.dot_general(kw, v, TN, preferred_element_type=F32)
            c_ref[h, :, 64:128] = w_c * cst[:, 64:128] + lax.dot_general(kw, ones_v, TN,
                                                                          preferred_element_type=F32)
            m_ref[h] = jnp.broadcast_to(m_new, (8, 128))

            hc = hh - jnp.mean(hh, axis=1, keepdims=True)
            hn = hc * lax.rsqrt(jnp.mean(hc * hc, axis=1, keepdims=True) + LN_EPS) * nw[:, h * 64:(h + 1) * 64]
            y_ref[pl.ds(rr, L), h * 64:(h + 1) * 64] = (jax.nn.sigmoid(og) * hn).astype(BF16)
        return carry

    lax.fori_loop(0, CH // L, pair, 0)


def _mlstm(ml, gates, brow, nw, B, S, CH=512):
    T = ml.shape[0]
    nS = S // CH
    row = lambda b, j: (b * nS + j, 0)
    full = lambda b, j: (0, 0)
    return pl.pallas_call(
        functools.partial(_mlstm_kernel, CH=CH),
        grid=(B, nS),
        in_specs=[pl.BlockSpec((CH, 1024), row), pl.BlockSpec((CH, 128), row),
                  pl.BlockSpec((1, 128), full), pl.BlockSpec((1, 256), full)],
        out_specs=pl.BlockSpec((CH, 256), row),
        out_shape=jax.ShapeDtypeStruct((T, 256), BF16),
        scratch_shapes=[pltpu.VMEM((HEADS, ML_DH, 128), F32), pltpu.VMEM((HEADS, 8, 128), F32)],
        compiler_params=_params(2),
        name="mlstm",
    )(ml, gates, brow, nw)


def _ssd_kernel(ssm_ref, g_ref, brow_ref, alog_ref, cw_ref, cb_ref, d_ref, nw_ref, y_ref,
                xbuf_ref, xact_ref, st_ref, *, CH):
    @pl.when(pl.program_id(1) == 0)
    def _init():
        xbuf_ref[0:8, :] = jnp.zeros((8, 512), F32)
        st_ref[...] = jnp.zeros_like(st_ref)

    xbuf_ref[8:CH + 8, :] = ssm_ref[:, 256:768].astype(F32)
    conv = cb_ref[...] + cw_ref[0:1, :] * xbuf_ref[5:5 + CH, :]
    for j in range(1, SSM_CONV):
        conv = conv + cw_ref[j:j + 1, :] * xbuf_ref[5 + j:5 + j + CH, :]
    xact_ref[...] = _silu(conv)
    xbuf_ref[0:8, :] = xbuf_ref[CH:CH + 8, :]

    L = 2 * CHUNK
    causal, tril, triu = _tri_consts(L)
    lane = lax.broadcasted_iota(jnp.int32, (1, 128), 1)
    a_row = jnp.where((lane >= G_DT) & (lane < G_DT + HEADS), -jnp.exp(alog_ref[...]), 0.0)
    dskip = d_ref[...]
    nw = nw_ref[...]

    def pair(pi, carry):
        rr = pl.multiple_of(pi * L, L)
        dt2 = _softplus(g_ref[pl.ds(rr, L), :] + brow_ref[...])
        a2 = dt2 * a_row
        a2_t = a2.T
        acs_all = jnp.dot(tril, a2, precision=HI, preferred_element_type=F32)
        acs_t = jnp.dot(a2_t[G_DT:G_DT + 8, :], triu, precision=HI, preferred_element_type=F32)
        cb = []
        bmat = []
        cmat = []
        for g in range(SSM_GROUPS):
            bm = xact_ref[pl.ds(rr, L), 256 + g * 64:256 + (g + 1) * 64]
            cm = xact_ref[pl.ds(rr, L), 384 + g * 64:384 + (g + 1) * 64].astype(BF16)
            bmat.append(bm)
            cmat.append(cm)
            cb.append(lax.dot_general(cm, bm.astype(BF16), NT, preferred_element_type=F32))
        gated = []
        ssq = jnp.zeros((L, 1), F32)
        for h in range(HEADS):
            g = h // (HEADS // SSM_GROUPS)
            acs_col = acs_all[:, G_DT + h:G_DT + h + 1]
            acs_row = acs_t[h:h + 1, :]
            dt_col = dt2[:, G_DT + h:G_DT + h + 1]
            xh = xact_ref[pl.ds(rr, L), h * 64:(h + 1) * 64]
            xdt = (xh * dt_col).astype(BF16)
            st = st_ref[h]
            mmat = cb[g] * jnp.exp(jnp.where(causal, acs_col - acs_row, -jnp.inf))
            y = jnp.dot(mmat.astype(BF16), xdt, preferred_element_type=F32)
            y = y + jnp.dot(cmat[g], st.astype(BF16), preferred_element_type=F32) * jnp.exp(acs_col)
            y = y + xh * dskip[:, h * 64:(h + 1) * 64]
            acs_last = acs_col[L - 1:L, :]
            bdec = (bmat[g] * jnp.exp(acs_last - acs_col)).astype(BF16)
            st_ref[h] = jnp.exp(acs_last) * st + lax.dot_general(bdec, xdt, TN, preferred_element_type=F32)
            z = ssm_ref[pl.ds(rr, L), h * 64:(h + 1) * 64].astype(F32)
            yg = y * _silu(z)
            ssq = ssq + jnp.sum(yg * yg, axis=1, keepdims=True)
            gated.append(yg)
        scale = lax.rsqrt(ssq / (HEADS * SSM_P) + LN_EPS)
        for h in range(HEADS):
            y_ref[pl.ds(rr, L), h * 64:(h + 1) * 64] = (gated[h] * scale * nw[:, h * 64:(h + 1) * 64]).astype(BF16)
        return carry

    lax.fori_loop(0, CH // L, pair, 0, unroll=True)


def _ssd(ssm, gates, brow, alog, cw, cb, dskip, nw, B, S, CH=512):
    T = ssm.shape[0]
    nS = S // CH
    row = lambda b, j: (b * nS + j, 0)
    full = lambda b, j: (0, 0)
    return pl.pallas_call(
        functools.partial(_ssd_kernel, CH=CH),
        grid=(B, nS),
        in_specs=[pl.BlockSpec((CH, 768), row), pl.BlockSpec((CH, 128), row), pl.BlockSpec((1, 128), full),
                  pl.BlockSpec((1, 128), full), pl.BlockSpec((SSM_CONV, 512), full), pl.BlockSpec((1, 512), full),
                  pl.BlockSpec((1, 256), full), pl.BlockSpec((1, 256), full)],
        out_specs=pl.BlockSpec((CH, 256), row),
        out_shape=jax.ShapeDtypeStruct((T, 256), BF16),
        scratch_shapes=[pltpu.VMEM((CH + 8, 512), F32), pltpu.VMEM((CH, 512), F32),
                        pltpu.VMEM((HEADS, SSM_N, SSM_P), F32)],
        compiler_params=_params(2),
        name="ssd",
    )(ssm, gates, brow, alog, cw, cb, dskip, nw)


def _attn_kernel(at_ref, o_ref, lse_ref, *, N):
    W = AT_SPAN
    ri = lax.broadcasted_iota(jnp.int32, (W, W), 0)
    ci = lax.broadcasted_iota(jnp.int32, (W, W), 1)
    cur_ok = ri >= ci
    prev_ok = ci >= ri
    low_lanes = ci < AT_DH
    ones = jnp.ones((W, W), BF16)

    NB = next(n for n in (8, 4, 2, 1) if (N // W) % n == 0)

    def pair_attention(q2, kc, kp, vc, vp, pmask):
        outs, lses = [], []
        for first in (True, False):
            qh = jnp.where(low_lanes if first else jnp.logical_not(low_lanes), q2, jnp.zeros_like(q2))
            sc = jnp.where(cur_ok, lax.dot_general(qh, kc, NT, preferred_element_type=F32), -jnp.inf)
            sp = jnp.where(pmask, lax.dot_general(qh, kp, NT, preferred_element_type=F32), -jnp.inf)
            m = jnp.max(jnp.maximum(sc, sp), axis=1, keepdims=True)
            pc = jnp.exp(sc - m).astype(BF16)
            pp = jnp.exp(sp - m).astype(BF16)
            acc = jnp.dot(pc, vc, preferred_element_type=F32) + jnp.dot(pp, vp, preferred_element_type=F32)
            den = jnp.dot(pc, ones, preferred_element_type=F32) + jnp.dot(pp, ones, preferred_element_type=F32)
            outs.append(acc / den)
            lses.append(m + jnp.log(den[:, 0:1]))
        return jnp.where(low_lanes, outs[0], outs[1]).astype(BF16), jnp.where(low_lanes, lses[0], lses[1])

    def blk(i, carry):
        work = []
        for u in range(NB):
            n = i * NB + u
            r0 = pl.multiple_of(n * W, W)
            rp = pl.multiple_of(jnp.maximum(n - 1, 0) * W, W)
            for p in range(HEADS // 2):
                lanes = lambda base: slice(base + 128 * p, base + 128 * (p + 1))
                work.append((r0, p, prev_ok & (n > 0),
                             at_ref[pl.ds(r0, W), lanes(0)], at_ref[pl.ds(r0, W), lanes(256)],
                             at_ref[pl.ds(rp, W), lanes(256)], at_ref[pl.ds(r0, W), lanes(512)],
                             at_ref[pl.ds(rp, W), lanes(512)]))
        done = [(r0, p) + pair_attention(q2, kc, kp, vc, vp, pmask) for r0, p, pmask, q2, kc, kp, vc, vp in work]
        for r0, p, o2, lse2 in done:
            o_ref[pl.ds(r0, W), 128 * p:128 * (p + 1)] = o2
            lse_ref[pl.ds(r0, W), 128 * p:128 * (p + 1)] = lse2
        return carry

    assert (N // W) % NB == 0
    lax.fori_loop(0, N // (W * NB), blk, 0)


def _attn(atv, d, B, S):
    N = S // d
    o, lse = pl.pallas_call(
        functools.partial(_attn_kernel, N=N),
        grid=(B, d),
        in_specs=[pl.BlockSpec((None, N, 768), lambda b, r: (b, 0, r))],
        out_specs=[pl.BlockSpec((None, N, 256), lambda b, r: (b, 0, r)),
                   pl.BlockSpec((None, N, 256), lambda b, r: (b, 0, r))],
        out_shape=[jax.ShapeDtypeStruct((B, N, d * 256), BF16), jax.ShapeDtypeStruct((B, N, d * 256), F32)],
        compiler_params=_params(2),
        name=f"attn_d{d}",
    )(atv.reshape(B, N, d * 768))
    return o.reshape(B * N, d * 256), lse.reshape(B * N, d * 256)


def _gla_kernel(gla_ref, g_ref, w2_ref, b2_ref, nw_ref, y_ref, st_ref, *, CH):
    @pl.when(pl.program_id(1) == 0)
    def _init():
        st_ref[...] = jnp.zeros_like(st_ref)

    L = CHUNK
    SB = 16
    ri = lax.broadcasted_iota(jnp.int32, (L, L), 0)
    ci = lax.broadcasted_iota(jnp.int32, (L, L), 1)
    tril_blk = ((ri >= ci) & (ri // SB == ci // SB)).astype(F32)
    tl = lax.broadcasted_iota(jnp.int32, (L, 128), 0) % SB
    er = lax.broadcasted_iota(jnp.int32, (128, 256), 0) // GLA_DK
    ec = lax.broadcasted_iota(jnp.int32, (128, 256), 1) // GLA_DV
    head_expand = (er == ec).astype(BF16)
    sr = lax.broadcasted_iota(jnp.int32, (256, 128), 0) // GLA_DV
    sc = lax.broadcasted_iota(jnp.int32, (256, 128), 1) // GLA_DK
    st_mask = (sr == sc).astype(F32)
    nw = nw_ref[...]

    def chunk(ci_, st):
        rr = pl.multiple_of(ci_ * L, L)
        lg = _log_sigmoid(jnp.dot(g_ref[pl.ds(rr, L), :], w2_ref[...], preferred_element_type=F32)
                          + b2_ref[...]) / GLA_TAU
        c = jnp.dot(tril_blk, lg, precision=HI, preferred_element_type=F32)
        q = gla_ref[pl.ds(rr, L), 0:128].astype(F32) * GLA_DK ** -0.5
        k = gla_ref[pl.ds(rr, L), 128:256].astype(F32)
        vb = gla_ref[pl.ds(rr, L), 256:512]
        v = vb.astype(F32)
        c4 = c.reshape(L // SB, SB, 128)
        k4 = k.reshape(L // SB, SB, 128)
        v4 = v.reshape(L // SB, SB, 256)

        def bcast(x4, j, width):
            return jnp.broadcast_to(x4[:, j:j + 1, :], (L // SB, SB, width)).reshape(L, width)

        o = jnp.zeros((L, 256), F32)
        for j in range(SB):
            dec = jnp.exp(jnp.minimum(c - bcast(c4, j, 128), 0.0))
            p = jnp.where(tl >= j, q * dec * bcast(k4, j, 128), 0.0)
            a = jnp.dot(p.astype(BF16), head_expand, preferred_element_type=F32)
            o = o + a * bcast(v4, j, 256)

        outs = []
        for i in range(L // SB):
            cblk = c[i * SB:(i + 1) * SB, :]
            clast = cblk[SB - 1:SB, :]
            qe = (q[i * SB:(i + 1) * SB, :] * jnp.exp(cblk)).astype(BF16)
            outs.append(o[i * SB:(i + 1) * SB, :] + lax.dot_general(qe, st.astype(BF16), NT,
                                                                    preferred_element_type=F32))
            ke = (k[i * SB:(i + 1) * SB, :] * jnp.exp(clast - cblk)).astype(BF16)
            upd = lax.dot_general(vb[i * SB:(i + 1) * SB, :], ke, TN, preferred_element_type=F32)
            st = st * jnp.exp(clast) + st_mask * upd
        for i in range(L // SB):
            rg = gla_ref[pl.ds(pl.multiple_of(rr + i * SB, SB), SB), 512:768].astype(F32)
            for h in range(HEADS):
                oh = outs[i][:, h * 64:(h + 1) * 64]
                rms = lax.rsqrt(jnp.mean(oh * oh, axis=1, keepdims=True) + LN_EPS)
                y_ref[pl.ds(pl.multiple_of(rr + i * SB, SB), SB), h * 64:(h + 1) * 64] = (
                    oh * rms * nw[:, h * 64:(h + 1) * 64] * _silu(rg[:, h * 64:(h + 1) * 64])).astype(BF16)
        return st

    st_ref[...] = lax.fori_loop(0, CH // L, chunk, st_ref[...], unroll=True)


def _gla(gla, gates, w2, b2, nw, B, S, CH=1024):
    T = gla.shape[0]
    nS = S // CH
    row = lambda b, j: (b * nS + j, 0)
    full = lambda b, j: (0, 0)
    return pl.pallas_call(
        functools.partial(_gla_kernel, CH=CH),
        grid=(B, nS),
        in_specs=[pl.BlockSpec((CH, 768), row), pl.BlockSpec((CH, 128), row), pl.BlockSpec((128, 128), full),
                  pl.BlockSpec((1, 128), full), pl.BlockSpec((1, 256), full)],
        out_specs=pl.BlockSpec((CH, 256), row),
        out_shape=jax.ShapeDtypeStruct((T, 256), BF16),
        scratch_shapes=[pltpu.VMEM((HEADS * GLA_DV, HEADS * GLA_DK), F32)],
        compiler_params=_params(2),
        name="gla",
    )(gla, gates, w2, b2, nw)


def _out_proj_kernel(yml_ref, yssm_ref, o1_ref, o4_ref, o16_ref, l1_ref, l4_ref, l16_ref, ygla_ref, x_ref,
                     wo_ref, g_ref, b_ref, x1_ref, os_ref, ls_ref):
    tm = x_ref.shape[0]
    for i, (d, o_ref, l_ref) in enumerate(((DILATIONS[1], o4_ref, l4_ref), (DILATIONS[2], o16_ref, l16_ref))):
        for r in range(d):
            for j in range(2):
                cols = slice(256 * r + 128 * j, 256 * r + 128 * (j + 1))
                os_ref[i, j, pl.ds(r, tm // d, stride=d), :] = o_ref[:, cols].astype(F32)
                ls_ref[i, j, pl.ds(r, tm // d, stride=d), :] = l_ref[:, cols]
    yat = []
    for j in range(2):
        o1 = o1_ref[:, 128 * j:128 * (j + 1)].astype(F32)
        l1 = l1_ref[:, 128 * j:128 * (j + 1)]
        l4, l16 = ls_ref[0, j], ls_ref[1, j]
        mx = jnp.maximum(jnp.maximum(l1, l4), l16)
        e1, e4, e16 = jnp.exp(l1 - mx), jnp.exp(l4 - mx), jnp.exp(l16 - mx)
        yat.append(((e1 * o1 + e4 * os_ref[0, j] + e16 * os_ref[1, j]) / (e1 + e4 + e16)).astype(BF16))
    acc = jnp.dot(yml_ref[...], wo_ref[0:256, :], preferred_element_type=F32)
    acc = acc + jnp.dot(yssm_ref[...], wo_ref[256:512, :], preferred_element_type=F32)
    acc = acc + jnp.dot(yat[0], wo_ref[512:640, :], preferred_element_type=F32)
    acc = acc + jnp.dot(yat[1], wo_ref[640:768, :], preferred_element_type=F32)
    acc = acc + jnp.dot(ygla_ref[...], wo_ref[768:1024, :], preferred_element_type=F32)
    x1_ref[...] = _layer_norm(ALPHA * x_ref[...] + acc, g_ref[...], b_ref[...])


def _out_proj(yml, yssm, branches, ygla, x, wo, g, b, tm=512):
    T, D = x.shape
    row = lambda i: (i, 0)
    full = lambda i: (0, 0)
    small = pl.BlockSpec((tm, 256), row)
    dil = [pl.BlockSpec((tm // d, 256 * d), row) for d in DILATIONS]
    return pl.pallas_call(
        _out_proj_kernel,
        grid=(T // tm,),
        in_specs=[small, small] + dil + dil + [small, pl.BlockSpec((tm, D), row), pl.BlockSpec(wo.shape, full),
                                               pl.BlockSpec((1, D), full), pl.BlockSpec((1, D), full)],
        out_specs=pl.BlockSpec((tm, D), row),
        out_shape=jax.ShapeDtypeStruct((T, D), F32),
        scratch_shapes=[pltpu.VMEM((2, 2, tm, 128), F32), pltpu.VMEM((2, 2, tm, 128), F32)],
        compiler_params=_params(1),
        name="out_proj_ln",
    )(yml, yssm, *[o for o, _ in branches], *[l for _, l in branches], ygla, x, wo, g, b)


def _router_kernel(x_ref, wr_ref, rb_ref, row_ref, col_ref, cnt_ref, rt_ref):
    tm = x_ref.shape[0]
    gsz = N_EXPERTS // N_GROUPS
    logits = lax.dot_general(wr_ref[...], x_ref[...], NT, precision=HI, preferred_element_type=F32)
    scores = jax.nn.sigmoid(logits)
    sel = scores + rb_ref[...]
    eidx = lax.broadcasted_iota(jnp.int32, (gsz, tm), 0)
    big = jnp.int32(1 << 20)
    neg = -jnp.inf

    sel_g = [sel[g * gsz:(g + 1) * gsz, :] for g in range(N_GROUPS)]
    idx_g = [eidx + g * gsz for g in range(N_GROUPS)]
    gscore = []
    for g in range(N_GROUPS):
        v = sel_g[g]
        m1 = jnp.max(v, axis=0, keepdims=True)
        i1 = jnp.min(jnp.where(v == m1, idx_g[g], big), axis=0, keepdims=True)
        m2 = jnp.max(jnp.where(idx_g[g] == i1, neg, v), axis=0, keepdims=True)
        gscore.append(m1 + m2)
    gkeep = [jnp.zeros((1, tm), jnp.bool_) for _ in range(N_GROUPS)]
    for _ in range(TOPK_GROUPS):
        m = functools.reduce(jnp.maximum, gscore)
        gi = functools.reduce(jnp.minimum, [jnp.where(gscore[g] == m, g, big) for g in range(N_GROUPS)])
        for g in range(N_GROUPS):
            hit = gi == g
            gkeep[g] = gkeep[g] | hit
            gscore[g] = jnp.where(hit, neg, gscore[g])
    cand = [jnp.where(gkeep[g], sel_g[g], neg) for g in range(N_GROUPS)]
    chosen = [jnp.zeros((gsz, tm), jnp.bool_) for _ in range(N_GROUPS)]
    picks = []
    for _ in range(TOP_K):
        m = functools.reduce(jnp.maximum, [jnp.max(c, axis=0, keepdims=True) for c in cand])
        ei = functools.reduce(jnp.minimum, [jnp.min(jnp.where(cand[g] == m, idx_g[g], big), axis=0, keepdims=True)
                                            for g in range(N_GROUPS)])
        picks.append(ei)
        for g in range(N_GROUPS):
            hit = idx_g[g] == ei
            chosen[g] = chosen[g] | hit
            cand[g] = jnp.where(hit, neg, cand[g])
    picked = [jnp.where(chosen[g], scores[g * gsz:(g + 1) * gsz, :], 0.0) for g in range(N_GROUPS)]
    tot = functools.reduce(jnp.add, [jnp.sum(p, axis=0, keepdims=True) for p in picked])
    gates = [p / tot * ROUTED_SCALE for p in picked]

    chosen_b = jnp.concatenate([c.astype(F32) for c in chosen] + [jnp.zeros((N_EXPERTS, tm), F32)], axis=0).astype(BF16)
    ti = lax.broadcasted_iota(jnp.int32, (tm, tm), 0)
    tj = lax.broadcasted_iota(jnp.int32, (tm, tm), 1)
    before = ((ti < tj) & (ti // TOK_TILE == tj // TOK_TILE)).astype(BF16)
    rank = jnp.dot(chosen_b[0:N_EXPERTS, :], before, preferred_element_type=F32)
    lower = (lax.broadcasted_iota(jnp.int32, (N_EXPERTS, N_EXPERTS), 0)
             > lax.broadcasted_iota(jnp.int32, (N_EXPERTS, N_EXPERTS), 1)).astype(F32)
    tile_of_lane = lax.broadcasted_iota(jnp.int32, (1, tm), 1) // TOK_TILE
    posmat = rank
    for t in range(tm // TOK_TILE):
        in_tile = chosen_b[:, t * TOK_TILE:(t + 1) * TOK_TILE]
        cnt = jnp.dot(in_tile[0:N_EXPERTS, :], jnp.ones((TOK_TILE, 128), BF16), preferred_element_type=F32)
        padded = jnp.maximum(jnp.floor((cnt + (ROW_PAD - 1)) * (1.0 / ROW_PAD)) * ROW_PAD, float(ROW_PAD))
        gstart = jnp.dot(lower, padded, precision=HI, preferred_element_type=F32)
        posmat = posmat + jnp.where(tile_of_lane == t, gstart[:, 0:1], 0.0)
        cnt_ref[8 * t:8 * (t + 1), :] = lax.dot_general(jnp.ones((8, TOK_TILE), BF16), in_tile, NT,
                                                        preferred_element_type=F32)
    rt_ref[...] = jnp.zeros_like(rt_ref)
    for k in range(TOP_K):
        pos_k = jnp.zeros((1, tm), F32)
        gate_k = jnp.zeros((1, tm), F32)
        for g in range(N_GROUPS):
            hit = idx_g[g] == picks[k]
            pos_k = pos_k + jnp.sum(jnp.where(hit, posmat[g * gsz:(g + 1) * gsz, :], 0.0), axis=0, keepdims=True)
            gate_k = gate_k + jnp.sum(jnp.where(hit, gates[g], 0.0), axis=0, keepdims=True)
        rt_ref[k:k + 1, :] = pos_k
        rt_ref[TOP_K + k:TOP_K + k + 1, :] = gate_k
    row_ref[...] = rt_ref[0:2 * TOP_K, :]
    col_ref[...] = rt_ref[...].T


def _router(x1, wr_t, rb):
    T, D = x1.shape
    tm = ROUTER_TILE
    sub = tm // TOK_TILE
    return pl.pallas_call(
        _router_kernel,
        grid=(T // tm,),
        in_specs=[pl.BlockSpec((tm, D), lambda i: (i, 0)), pl.BlockSpec(wr_t.shape, lambda i: (0, 0)),
                  pl.BlockSpec(rb.shape, lambda i: (0, 0))],
        out_specs=[pl.BlockSpec((2 * TOP_K, tm), lambda i: (0, i)), pl.BlockSpec((tm, 128), lambda i: (i, 0)),
                   pl.BlockSpec((8 * sub, 128), lambda i: (i, 0))],
        out_shape=[jax.ShapeDtypeStruct((2 * TOP_K, T), F32), jax.ShapeDtypeStruct((T, 128), F32),
                   jax.ShapeDtypeStruct((T // TOK_TILE * 8, 128), F32)],
        scratch_shapes=[pltpu.VMEM((128, tm), F32)],
        compiler_params=_params(1),
        name="router",
    )(x1, wr_t, rb)


M_LSTART, M_ROWS, M_GOFF = 0, N_EXPERTS, 2 * N_EXPERTS
M_USED, M_USED_PREV = 3 * N_EXPERTS, 3 * N_EXPERTS + 1
M_COLS = 4 * N_EXPERTS


def _ceil_div(x, n):
    assert n & (n - 1) == 0
    return lax.shift_right_logical(x + (n - 1), n.bit_length() - 1)


def _seg_copy(meta_ref, e, loc_ref, glob_ref, sem, to_global):
    n = pl.multiple_of(meta_ref[0, 0, M_ROWS + e] * ROW_PAD, ROW_PAD)
    ls = pl.multiple_of(meta_ref[0, 0, M_LSTART + e] * ROW_PAD, ROW_PAD)
    go = pl.multiple_of(meta_ref[0, 0, M_GOFF + e] * ROW_PAD, ROW_PAD)
    loc, glob = loc_ref.at[pl.ds(ls, n)], glob_ref.at[pl.ds(go, n)]
    return pltpu.make_async_copy(loc, glob, sem) if to_global else pltpu.make_async_copy(glob, loc, sem)


def _dispatch_kernel(meta_ref, tail_ref, x_ref, row_ref, xg_ref, loc_ref, zero_ref, sems):
    s = pl.program_id(0)
    last = pl.num_programs(0) - 1
    slot = s % 2
    tm = x_ref.shape[0]

    @pl.when(s == 0)
    def _fill_tails():
        zero_ref[...] = jnp.zeros_like(zero_ref)

        def start(e, c):
            n = pl.multiple_of(tail_ref[1, e] * ROW_PAD, ROW_PAD)
            go = pl.multiple_of(tail_ref[0, e] * ROW_PAD, ROW_PAD)

            @pl.when(n > 0)
            def _():
                pltpu.make_async_copy(zero_ref.at[pl.ds(0, n)], xg_ref.at[pl.ds(go, n)], sems.at[2]).start()
            return c
        lax.fori_loop(0, N_EXPERTS, start, 0)

    RC = 256
    xb = x_ref[...].astype(BF16)
    pos = row_ref[0:TOP_K, :]
    hi = jnp.floor(pos * (1.0 / RC))
    lo = pos - hi * RC
    rows = lax.broadcasted_iota(jnp.int32, (RC, tm), 0).astype(F32)
    lo_hit = [jnp.where(lo[k:k + 1, :] == rows, 1.0, 0.0).astype(BF16) for k in range(TOP_K)]
    used = meta_ref[0, 0, M_USED]

    def chunk(c, carry):
        cf = c.astype(F32)
        onehot = lo_hit[0] * jnp.where(hi[0:1, :] == cf, 1.0, 0.0).astype(BF16)
        for k in range(1, TOP_K):
            onehot = onehot + lo_hit[k] * jnp.where(hi[k:k + 1, :] == cf, 1.0, 0.0).astype(BF16)
        loc_ref[slot, pl.ds(pl.multiple_of(c * RC, RC), RC), :] = jnp.dot(
            onehot, xb, preferred_element_type=F32).astype(BF16)
        return carry
    lax.fori_loop(0, _ceil_div(used, RC // ROW_PAD), chunk, 0)

    def start(e, c):
        _seg_copy(meta_ref, e, loc_ref.at[slot], xg_ref, sems.at[slot], True).start()
        return c
    lax.fori_loop(0, N_EXPERTS, start, 0)

    def wait_all(total_col, buf):
        n = pl.multiple_of(meta_ref[0, 0, total_col] * ROW_PAD, ROW_PAD)

        @pl.when(n > 0)
        def _():
            pltpu.make_async_copy(loc_ref.at[buf, pl.ds(0, n)], xg_ref.at[pl.ds(0, n)], sems.at[buf]).wait()

    @pl.when(s > 0)
    def _wait_prev():
        wait_all(M_USED_PREV, 1 - slot)

    @pl.when(s == last)
    def _wait_own():
        wait_all(M_USED, slot)

    @pl.when(s == 0)
    def _wait_tails():
        def wait(e, c):
            n = pl.multiple_of(tail_ref[1, e] * ROW_PAD, ROW_PAD)
            go = pl.multiple_of(tail_ref[0, e] * ROW_PAD, ROW_PAD)

            @pl.when(n > 0)
            def _():
                pltpu.make_async_copy(zero_ref.at[pl.ds(0, n)], xg_ref.at[pl.ds(go, n)], sems.at[2]).wait()
            return c
        lax.fori_loop(0, N_EXPERTS, wait, 0)


def _dispatch(meta, tail, x1, rowform, p_rows):
    T, D = x1.shape
    tm = TOK_TILE
    return pl.pallas_call(
        _dispatch_kernel,
        grid=(T // tm,),
        in_specs=[pl.BlockSpec((1, 1, M_COLS), lambda i: (i, 0, 0), memory_space=pltpu.SMEM),
                  pl.BlockSpec(memory_space=pltpu.SMEM),
                  pl.BlockSpec((tm, D), lambda i: (i, 0)), pl.BlockSpec((2 * TOP_K, tm), lambda i: (0, i))],
        out_specs=pl.BlockSpec(memory_space=pl.ANY),
        out_shape=jax.ShapeDtypeStruct((p_rows, D), BF16),
        scratch_shapes=[pltpu.VMEM((2, LOC_ROWS, D), BF16), pltpu.VMEM((FFN_BLK, D), BF16),
                        pltpu.SemaphoreType.DMA((3,))],
        compiler_params=_params(1),
        name="moe_dispatch",
    )(meta, tail, x1, rowform)


def _ffn_kernel(bexp_ref, nused_ref, x_ref, wg_ref, wu_ref, wd_ref, y_ref, wgb_ref, wub_ref, wdb_ref):
    i = pl.program_id(0)

    @pl.when(i < nused_ref[0])
    def _():
        @pl.when((i == 0) | (bexp_ref[i] != bexp_ref[jnp.maximum(i - 1, 0)]))
        def _new_expert():
            wgb_ref[...] = wg_ref[...].astype(BF16)
            wub_ref[...] = wu_ref[...].astype(BF16)
            wdb_ref[...] = wd_ref[...].astype(BF16)

        xb = x_ref[...]
        a = jnp.dot(xb, wgb_ref[...], preferred_element_type=F32)
        u = jnp.dot(xb, wub_ref[...], preferred_element_type=F32)
        y_ref[...] = jnp.dot((_silu(a) * u).astype(BF16), wdb_ref[...], preferred_element_type=F32).astype(BF16)


def _ffn(blk_exp, nused, xg, wg, wu, wd, layer):
    P, D = xg.shape
    F = wg.shape[3]
    blk = lambda i, be, nu: (jnp.maximum(jnp.minimum(i, nu[0] - 1), 0), 0)
    wsel = lambda i, be, nu: (layer, be[jnp.maximum(jnp.minimum(i, nu[0] - 1), 0)], 0, 0)
    return pl.pallas_call(
        _ffn_kernel,
        grid_spec=pltpu.PrefetchScalarGridSpec(
            num_scalar_prefetch=2,
            grid=(P // FFN_BLK,),
            in_specs=[pl.BlockSpec((FFN_BLK, D), blk), pl.BlockSpec((None, None, D, F), wsel),
                      pl.BlockSpec((None, None, D, F), wsel), pl.BlockSpec((None, None, F, D), wsel)],
            out_specs=pl.BlockSpec((FFN_BLK, D), blk),
            scratch_shapes=[pltpu.VMEM((D, F), BF16), pltpu.VMEM((D, F), BF16), pltpu.VMEM((F, D), BF16)]),
        out_shape=jax.ShapeDtypeStruct((P, D), BF16),
        compiler_params=_params(1),
        name="moe_ffn",
    )(blk_exp, nused, xg, wg, wu, wd)


def _combine_kernel(meta_ref, next_ref, x_ref, col_ref, yg_ref, sg_ref, su_ref, sd_ref, g_ref, b_ref, x2_ref,
                    loc_ref, z_ref, sems):
    s = pl.program_id(0)
    last = pl.num_programs(0) - 1
    slot = s % 2
    tm = x_ref.shape[0]
    CW = 512
    cw_u = CW // ROW_PAD

    def used_rows(m_ref):
        return m_ref[0, 0, M_USED]

    def fetch(m_ref, buf):
        def start(e, c):
            _seg_copy(m_ref, e, loc_ref.at[buf], yg_ref, sems.at[buf], False).start()
            return c
        lax.fori_loop(0, N_EXPERTS, start, 0)
        used = used_rows(m_ref)

        def clear(c, carry):
            loc_ref[buf, pl.ds(pl.multiple_of(c * ROW_PAD, ROW_PAD), ROW_PAD), :] = jnp.zeros(
                (ROW_PAD, loc_ref.shape[2]), BF16)
            return carry
        lax.fori_loop(used, _ceil_div(used, cw_u) * cw_u, clear, 0)

    @pl.when(s == 0)
    def _first():
        fetch(meta_ref, 0)

    @pl.when(s < last)
    def _prefetch():
        fetch(next_ref, 1 - slot)

    x1 = x_ref[...]
    xb = x1.astype(BF16)
    hid = _silu(jnp.dot(xb, sg_ref[...], preferred_element_type=F32)) * jnp.dot(xb, su_ref[...],
                                                                                preferred_element_type=F32)
    z_ref[...] = ALPHA * x1 + jnp.dot(hid.astype(BF16), sd_ref[...], preferred_element_type=F32)

    pos = col_ref[:, 0:TOP_K]
    gate = col_ref[:, TOP_K:2 * TOP_K]
    hi = jnp.floor(pos * (1.0 / CW))
    lo = pos - hi * CW
    lanes = lax.broadcasted_iota(jnp.int32, (tm, CW), 1).astype(F32)
    lo_hit = [jnp.where(lo[:, k:k + 1] == lanes, 1.0, 0.0).astype(BF16) for k in range(TOP_K)]
    n_own = pl.multiple_of(used_rows(meta_ref) * ROW_PAD, ROW_PAD)

    @pl.when(n_own > 0)
    def _wait_own():
        pltpu.make_async_copy(yg_ref.at[pl.ds(0, n_own)], loc_ref.at[slot, pl.ds(0, n_own)], sems.at[slot]).wait()

    def chunk(c, carry):
        cf = c.astype(F32)
        wmat = lo_hit[0] * jnp.where(hi[:, 0:1] == cf, gate[:, 0:1], 0.0).astype(BF16)
        for k in range(1, TOP_K):
            wmat = wmat + lo_hit[k] * jnp.where(hi[:, k:k + 1] == cf, gate[:, k:k + 1], 0.0).astype(BF16)
        rows = loc_ref[slot, pl.ds(pl.multiple_of(c * CW, CW), CW), :]
        z_ref[...] += jnp.dot(wmat, rows, preferred_element_type=F32)
        return carry
    lax.fori_loop(0, _ceil_div(used_rows(meta_ref), cw_u), chunk, 0)
    x2_ref[...] = _layer_norm(z_ref[...], g_ref[...], b_ref[...])


def _combine(meta, x1, colform, yg, sg, su, sd, g, b):
    T, D = x1.shape
    tm = TOK_TILE
    nst = T // tm
    full = lambda i: (0, 0)
    mspec = lambda f: pl.BlockSpec((1, 1, M_COLS), f, memory_space=pltpu.SMEM)
    return pl.pallas_call(
        _combine_kernel,
        grid=(nst,),
        in_specs=[mspec(lambda i: (i, 0, 0)), mspec(lambda i: (jnp.minimum(i + 1, nst - 1), 0, 0)),
                  pl.BlockSpec((tm, D), lambda i: (i, 0)), pl.BlockSpec((tm, 128), lambda i: (i, 0)),
                  pl.BlockSpec(memory_space=pl.ANY),
                  pl.BlockSpec(sg.shape, full), pl.BlockSpec(su.shape, full), pl.BlockSpec(sd.shape, full),
                  pl.BlockSpec((1, D), full), pl.BlockSpec((1, D), full)],
        out_specs=pl.BlockSpec((tm, D), lambda i: (i, 0)),
        out_shape=jax.ShapeDtypeStruct((T, D), F32),
        scratch_shapes=[pltpu.VMEM((2, LOC_ROWS, D), BF16), pltpu.VMEM((tm, D), F32),
                        pltpu.SemaphoreType.DMA((2,))],
        compiler_params=_params(1),
        name="moe_combine_ln",
    )(meta, meta, x1, colform, yg, sg, su, sd, g, b)


def _moe_plan(cnt_out, T):
    nst = T // TOK_TILE
    blk_u = FFN_BLK // ROW_PAD
    cnt = cnt_out.reshape(nst, 8, 128)[:, 0, :N_EXPERTS].astype(jnp.int32)
    rows = jnp.maximum((cnt + ROW_PAD - 1) // ROW_PAD, 1)
    lstart = jnp.cumsum(rows, axis=1) - rows
    tot = rows.sum(axis=0)
    tot_pad = (tot + blk_u - 1) // blk_u * blk_u
    eend = jnp.cumsum(tot_pad)
    ebase = eend - tot_pad
    goff = ebase[None, :] + jnp.cumsum(rows, axis=0) - rows
    used = rows.sum(axis=1, keepdims=True)
    used_prev = jnp.concatenate([jnp.zeros((1, 1), jnp.int32), used[:-1]], axis=0)
    fill = jnp.zeros((nst, M_COLS - M_USED_PREV - 1), jnp.int32)
    meta = jnp.concatenate([lstart, rows, goff, used, used_prev, fill], axis=1).reshape(nst, 1, M_COLS)
    tail = jnp.stack([ebase + tot, tot_pad - tot])
    nblk = _moe_rows(T) // FFN_BLK
    nused = (eend[-1] // blk_u).reshape(1)
    first_row = jnp.arange(nblk, dtype=jnp.int32) * blk_u
    blk_exp = jnp.minimum(jnp.sum(first_row[:, None] >= eend[None, :], axis=1), N_EXPERTS - 1).astype(jnp.int32)
    return meta, tail, blk_exp, nused


def _moe_rows(T):
    worst = T * TOP_K + (T // TOK_TILE) * N_EXPERTS * ROW_PAD + N_EXPERTS * (FFN_BLK - ROW_PAD)
    return -(-worst // FFN_BLK) * FFN_BLK


def _rope_tables(S):
    half = AT_DH // 2
    lane = jnp.arange(128)
    inv = ROPE_THETA ** (-(lane % half).astype(F32) / half)
    ang = jnp.arange(S, dtype=F32)[:, None] * inv[None, :]
    sign = jnp.where((lane % AT_DH) < half, -1.0, 1.0).astype(F32)
    return jnp.cos(ang), jnp.sin(ang) * sign[None, :]


def _pad_cols(a, width):
    return jnp.pad(a, ((0, 0), (0, width - a.shape[1])))


def kernel(x, w_in, ml_i_bias, ml_f_bias, ml_norm_w, ssm_conv_w, ssm_conv_b, ssm_dt_bias, ssm_a_log, ssm_d, ssm_norm_w, gla_gate_w2, gla_gate_b, gla_norm_w, w_out, ln1_g, ln1_b, router_w, router_bias, exp_w_gate, exp_w_up, exp_w_down, sh_w_gate, sh_w_up, sh_w_down, ln2_g, ln2_b):
    B, S, D = x.shape
    T = B * S
    depth = w_in.shape[0]
    assert D == 1024 and S % (AT_SPAN * max(DILATIONS)) == 0 and T % 1024 == 0
    cos, sin = _rope_tables(S)
    xf = x.reshape(T, D)
    for l in range(depth):
        wm, wgt = _regroup_weights(w_in, l)
        brow = _pad_cols(jnp.concatenate([ml_i_bias[l], ml_f_bias[l], ssm_dt_bias[l]])[None, :], 128).astype(F32)
        alog = jnp.pad(ssm_a_log[l].astype(F32), (G_DT, 128 - G_DT - HEADS))[None, :]
        w2 = jnp.pad(gla_gate_w2[l].astype(F32), ((G_GA, 128 - G_GA - GLA_RANK), (0, 0)))

        ml, ssm, at1, at4, at16, gla, gates = _in_proj(xf, wm, wgt, cos, sin, S)
        y_ml = _mlstm(ml, gates, brow, ml_norm_w[l][None, :].astype(F32), B, S)
        y_ssm = _ssd(ssm, gates, brow, alog, ssm_conv_w[l].astype(F32), ssm_conv_b[l][None, :].astype(F32),
                     jnp.repeat(ssm_d[l].astype(F32), SSM_P)[None, :], ssm_norm_w[l][None, :].astype(F32), B, S)
        branches = [_attn(atv, d, B, S) for atv, d in zip((at1, at4, at16), DILATIONS)]
        y_gla = _gla(gla, gates, w2, gla_gate_b[l][None, :].astype(F32), gla_norm_w[l][None, :].astype(F32), B, S)
        x1 = _out_proj(y_ml, y_ssm, branches, y_gla, xf,
                       w_out[l].astype(BF16), ln1_g[l][None, :].astype(F32), ln1_b[l][None, :].astype(F32))
        rowform, colform, cnt = _router(x1, router_w[l].T.astype(F32), router_bias[l][:, None].astype(F32))
        meta, tail, blk_exp, nused = _moe_plan(cnt, T)
        xg = _dispatch(meta, tail, x1, rowform, _moe_rows(T))
        yg = _ffn(blk_exp, nused, xg, exp_w_gate, exp_w_up, exp_w_down, l)
        xf = _combine(meta, x1, colform, yg, sh_w_gate[l].astype(BF16), sh_w_up[l].astype(BF16),
                      sh_w_down[l].astype(BF16), ln2_g[l][None, :].astype(F32), ln2_b[l][None, :].astype(F32))
    return xf.reshape(B, S, D)
```
